```python
import math
import jax
import jax.numpy as jnp
from jax import lax
import numpy as np

D_MODEL = 1024
BATCH = 8
SEQ = 2048
DEPTH = 4
DEC_BATCH = 128
DEC_SEQ = 4
PAST_LEN = 8192
PAGE_SIZE = 128

N_MIXERS = 3
N_SWA = (DEPTH + 2) // 3
N_GDN = (DEPTH + 1) // 3
N_HGRN = DEPTH // 3

SWA_HEAD_DIM = 64
SWA_HEADS = D_MODEL // SWA_HEAD_DIM
SWA_KV_HEADS = SWA_HEADS // 4
SWA_GROUP = SWA_HEADS // SWA_KV_HEADS
WINDOW = 128
ROT_DIM = SWA_HEAD_DIM // 4
ROPE_THETA = 500000.0
SWA_IN = (SWA_HEADS + 2 * SWA_KV_HEADS) * SWA_HEAD_DIM

GDN_HEAD_DIM = 128
GDN_QK_HEADS = D_MODEL // GDN_HEAD_DIM
GDN_V_HEADS = 2 * GDN_QK_HEADS
GDN_KEY_DIM = GDN_QK_HEADS * GDN_HEAD_DIM
GDN_V_DIM = GDN_V_HEADS * GDN_HEAD_DIM
GDN_CONV_DIM = 2 * GDN_KEY_DIM + GDN_V_DIM
GDN_CONV = 4
GDN_CHUNK = 64
GDN_IN = GDN_CONV_DIM + GDN_V_DIM + 2 * GDN_V_HEADS

HG_HEAD_DIM = 128
HG_HEADS = D_MODEL // HG_HEAD_DIM
HG_DIM = HG_HEADS * HG_HEAD_DIM
HG_CHUNK = 32

D_FF = ((8 * D_MODEL // 3 + 127) // 128) * 128
NORM_EPS = 1e-6
NEG_INF = -1e30

kernel_name = "hybrid_swa_gdn_hgrn2_macaron_step"


def _rmsnorm(x, g):
    xf = x.astype(jnp.float32)
    y = xf * lax.rsqrt(jnp.mean(xf * xf, axis=-1, keepdims=True) + NORM_EPS)
    return (y * g.astype(jnp.float32)).astype(x.dtype)


def _l2norm(x):
    return x * lax.rsqrt(jnp.sum(x * x, axis=-1, keepdims=True) + NORM_EPS)


def _swiglu(h, w_in, w_out):
    gate, up = jnp.split(h @ w_in, 2, axis=-1)
    return (jax.nn.silu(gate) * up) @ w_out


def _rope_partial(x, pos):
    half = ROT_DIM // 2
    inv_freq = ROPE_THETA ** (-jnp.arange(0, ROT_DIM, 2, dtype=jnp.float32) / ROT_DIM)
    ang = pos.astype(jnp.float32)[:, None] * inv_freq[None, :]
    cos = jnp.cos(ang)[None, :, None, :]
    sin = jnp.sin(ang)[None, :, None, :]
    xr = x[..., :ROT_DIM].astype(jnp.float32)
    x1, x2 = xr[..., :half], xr[..., half:]
    rot = jnp.concatenate([x1 * cos - x2 * sin, x2 * cos + x1 * sin], axis=-1)
    return jnp.concatenate([rot.astype(x.dtype), x[..., ROT_DIM:]], axis=-1)


def _sink_attention(q, k, v, mask, sinks):
    s = jnp.einsum('...qhgd,...khd->...hgqk', q.astype(jnp.float32), k.astype(jnp.float32))
    s = jnp.where(mask, s * (SWA_HEAD_DIM ** -0.5), NEG_INF)
    sink = sinks.astype(jnp.float32).reshape(SWA_KV_HEADS, SWA_GROUP, 1, 1)
    m = jnp.maximum(jnp.max(s, axis=-1, keepdims=True), sink)
    p = jnp.exp(s - m)
    denom = jnp.sum(p, axis=-1, keepdims=True) + jnp.exp(sink - m)
    return jnp.einsum('...hgqk,...khd->...qhgd', p / denom, v.astype(jnp.float32))


def _swa_mixer(h, pos0, w_in, w_out, sinks, cache_k, cache_v):
    b, l, _ = h.shape
    q, k, v = jnp.split(h @ w_in, [SWA_HEADS * SWA_HEAD_DIM, (SWA_HEADS + SWA_KV_HEADS) * SWA_HEAD_DIM], axis=-1)
    pos = pos0 + jnp.arange(l)
    q = _rope_partial(q.reshape(b, l, SWA_HEADS, SWA_HEAD_DIM), pos)
    k = _rope_partial(k.reshape(b, l, SWA_KV_HEADS, SWA_HEAD_DIM), pos)
    v = v.reshape(b, l, SWA_KV_HEADS, SWA_HEAD_DIM)
    q = q.reshape(b, l, SWA_KV_HEADS, SWA_GROUP, SWA_HEAD_DIM)
    if cache_k is None:
        nb = l // WINDOW
        qb = q.reshape(b, nb, WINDOW, SWA_KV_HEADS, SWA_GROUP, SWA_HEAD_DIM)
        kb = k.reshape(b, nb, WINDOW, SWA_KV_HEADS, SWA_HEAD_DIM)
        vb = v.reshape(b, nb, WINDOW, SWA_KV_HEADS, SWA_HEAD_DIM)
        padw = ((0, 0), (1, 0), (0, 0), (0, 0), (0, 0))
        kk = jnp.concatenate([jnp.pad(kb, padw)[:, :-1], kb], axis=2)
        vv = jnp.concatenate([jnp.pad(vb, padw)[:, :-1], vb], axis=2)
        qi = jnp.arange(WINDOW)[:, None]
        kj = jnp.arange(2 * WINDOW)[None, :]
        d = qi + WINDOW - kj
        band = (d >= 0) & (d < WINDOW)
        valid = (jnp.arange(nb)[:, None, None] > 0) | (kj[None] >= WINDOW)
        mask = (band[None] & valid)[None, :, None, None]
        o = _sink_attention(qb, kk, vv, mask, sinks).reshape(b, l, SWA_HEADS * SWA_HEAD_DIM)
        k_buf, v_buf = k[:, l - WINDOW:], v[:, l - WINDOW:]
    else:
        kk = jnp.concatenate([cache_k.astype(k.dtype), k], axis=1)
        vv = jnp.concatenate([cache_v.astype(v.dtype), v], axis=1)
        qi = jnp.arange(l)[:, None]
        kj = jnp.arange(WINDOW + l)[None, :]
        d = qi + WINDOW - kj
        mask = (d >= 0) & (d < WINDOW)
        o = _sink_attention(q, kk, vv, mask, sinks).reshape(b, l, SWA_HEADS * SWA_HEAD_DIM)
        k_buf, v_buf = kk[:, l:], vv[:, l:]
    return o.astype(h.dtype) @ w_out, k_buf, v_buf


def _to_chunks(x, c, n):
    bsz, l = x.shape[:2]
    x = jnp.pad(x, [(0, 0), (0, n * c - l)] + [(0, 0)] * (x.ndim - 2))
    x = x.reshape((bsz, n, c) + x.shape[2:])
    return jnp.swapaxes(jnp.moveaxis(x, 1, 0), 2, 3)


def _from_chunks(o, l):
    o = jnp.moveaxis(jnp.swapaxes(o, 2, 3), 0, 1)
    return o.reshape((o.shape[0], -1) + o.shape[3:])[:, :l]


def _gated_delta_rule(q, k, v, g, beta, s0):
    l = q.shape[1]
    c = min(GDN_CHUNK, l)
    n = -(-l // c)
    dv = v.shape[-1]
    incl = jnp.tril(jnp.ones((c, c), dtype=bool))
    strict = jnp.tril(jnp.ones((c, c), dtype=bool), -1)
    eye = jnp.eye(c, dtype=jnp.float32)

    def step(s, xs):
        qc, kc, vc, gc, bc = xs
        gcum = jnp.cumsum(gc, axis=-1)
        decay = jnp.exp(jnp.where(incl, gcum[..., :, None] - gcum[..., None, :], -jnp.inf))
        kb = kc * bc[..., None]
        a = jnp.where(strict, jnp.einsum('bhid,bhjd->bhij', kb, kc) * decay, 0.0)
        rhs = jnp.concatenate([vc * bc[..., None], kb * jnp.exp(gcum)[..., None]], axis=-1)
        sol = lax.linalg.triangular_solve(eye + a, rhs, left_side=True, lower=True, unit_diagonal=True)
        u, w = sol[..., :dv], sol[..., dv:]
        v_new = u - jnp.einsum('bhik,bhkv->bhiv', w, s)
        attn = jnp.einsum('bhid,bhjd->bhij', qc, kc) * decay
        o = (jnp.einsum('bhik,bhkv->bhiv', qc * jnp.exp(gcum)[..., None], s)
             + jnp.einsum('bhij,bhjv->bhiv', attn, v_new))
        g_last = gcum[..., -1:]
        s = (s * jnp.exp(g_last)[..., None]
             + jnp.einsum('bhik,bhiv->bhkv', kc * jnp.exp(g_last - gcum)[..., None], v_new))
        return s, o

    xs = (_to_chunks(q, c, n), _to_chunks(k, c, n), _to_chunks(v, c, n),
          _to_chunks(g, c, n), _to_chunks(beta, c, n))
    s_fin, o = lax.scan(step, s0, xs)
    return _from_chunks(o, l), s_fin


def _gated_linear_recurrence(q, k, v, log_f, s0):
    l = q.shape[1]
    c = min(HG_CHUNK, l)
    n = -(-l // c)
    incl = jnp.tril(jnp.ones((c, c), dtype=bool))[:, :, None]

    def step(s, xs):
        qc, kc, vc, lc = xs
        bcum = jnp.cumsum(lc, axis=2)
        diff = bcum[:, :, :, None, :] - bcum[:, :, None, :, :]
        decay = jnp.exp(jnp.where(incl, diff, -jnp.inf))
        attn = jnp.einsum('bhid,bhjd,bhijd->bhij', qc, kc, decay)
        o = (jnp.einsum('bhik,bhkv->bhiv', qc * jnp.exp(bcum), s)
             + jnp.einsum('bhij,bhjv->bhiv', attn, vc))
        b_last = bcum[:, :, -1:]
        s = (s * jnp.exp(bcum[:, :, -1])[..., None]
             + jnp.einsum('bhjk,bhjv->bhkv', kc * jnp.exp(b_last - bcum), vc))
        return s, o

    xs = (_to_chunks(q, c, n), _to_chunks(k, c, n), _to_chunks(v, c, n), _to_chunks(log_f, c, n))
    s_fin, o = lax.scan(step, s0, xs)
    return _from_chunks(o, l), s_fin


def _gdn_mixer(h, w_in, conv_w, a_log, dt_bias, norm_g, w_out, s0, conv0):
    b, l, _ = h.shape
    qkv, z, beta_in, a_in = jnp.split(
        h @ w_in, [GDN_CONV_DIM, GDN_CONV_DIM + GDN_V_DIM, GDN_CONV_DIM + GDN_V_DIM + GDN_V_HEADS], axis=-1)
    if conv0 is None:
        conv0 = jnp.zeros((b, GDN_CONV - 1, GDN_CONV_DIM), h.dtype)
    xp = jnp.concatenate([conv0.astype(h.dtype), qkv], axis=1)
    acc = xp[:, 0:l] * conv_w[0]
    for t in range(1, GDN_CONV):
        acc = acc + xp[:, t:t + l] * conv_w[t]
    qkv_c = jax.nn.silu(acc.astype(jnp.float32))
    q, k, v = jnp.split(qkv_c, [GDN_KEY_DIM, 2 * GDN_KEY_DIM], axis=-1)
    rep = GDN_V_HEADS // GDN_QK_HEADS
    q = jnp.repeat(_l2norm(q.reshape(b, l, GDN_QK_HEADS, GDN_HEAD_DIM)), rep, axis=2) * (GDN_HEAD_DIM ** -0.5)
    k = jnp.repeat(_l2norm(k.reshape(b, l, GDN_QK_HEADS, GDN_HEAD_DIM)), rep, axis=2)
    v = v.reshape(b, l, GDN_V_HEADS, GDN_HEAD_DIM)
    beta = jax.nn.sigmoid(beta_in.astype(jnp.float32))
    g = -jnp.exp(a_log.astype(jnp.float32)) * jax.nn.softplus(a_in.astype(jnp.float32) + dt_bias.astype(jnp.float32))
    if s0 is None:
        s_init = jnp.zeros((b, GDN_V_HEADS, GDN_HEAD_DIM, GDN_HEAD_DIM), jnp.float32)
    else:
        s_init = s0.astype(jnp.float32)
    o, s_fin = _gated_delta_rule(q, k, v, g, beta, s_init)
    o = _rmsnorm(o, norm_g) * jax.nn.silu(z.astype(jnp.float32).reshape(b, l, GDN_V_HEADS, GDN_HEAD_DIM))
    y = o.reshape(b, l, GDN_V_DIM).astype(h.dtype) @ w_out
    return y, s_fin.astype(h.dtype), xp[:, l:]


def _hgrn_mixer(h, lb, w_in, norm_g, w_out, s0):
    b, l, _ = h.shape
    shp = (b, l, HG_HEADS, HG_HEAD_DIM)
    q, fz, i_in, gate = jnp.split((h @ w_in).astype(jnp.float32), 4, axis=-1)
    lb = lb.reshape(HG_HEADS, HG_HEAD_DIM)
    fz = fz.reshape(shp)
    log_f = jnp.logaddexp(jnp.log(lb), jnp.log1p(-lb) + jax.nn.log_sigmoid(fz))
    k = (1.0 - lb) * jax.nn.sigmoid(-fz)
    q = jax.nn.silu(q).reshape(shp)
    v = i_in.reshape(shp)
    if s0 is None:
        s_init = jnp.zeros((b, HG_HEADS, HG_HEAD_DIM, HG_HEAD_DIM), jnp.float32)
    else:
        s_init = s0.astype(jnp.float32)
    o, s_fin = _gated_linear_recurrence(q, k, v, log_f, s_init)
    o = _rmsnorm(o, norm_g) * jax.nn.silu(gate.reshape(shp))
    y = o.reshape(b, l, HG_DIM).astype(h.dtype) @ w_out
    return y, s_fin.astype(h.dtype)


def _trunk(x, pos0, cache_swa_k, cache_swa_v, state_gdn, state_gdn_conv, state_hgrn, w):
    new_k, new_v, new_gdn, new_conv, new_hgrn = [], [], [], [], []
    p_lb = jax.nn.softmax(w['hgrn_lb_logits'].astype(jnp.float32), axis=0)
    lb_all = jnp.cumsum(p_lb, axis=0) - p_lb[:1]
    for i in range(DEPTH):
        kind, j = i % N_MIXERS, i // N_MIXERS
        x = x + 0.5 * _swiglu(_rmsnorm(x, w['norm_ffn'][i, 0]), w['ffn_w_in'][i, 0], w['ffn_w_out'][i, 0])
        h = _rmsnorm(x, w['norm_mix'][i])
        if kind == 0:
            y, kb, vb = _swa_mixer(h, pos0, w['swa_w_in'][j], w['swa_w_out'][j], w['swa_sinks'][j],
                                   None if cache_swa_k is None else cache_swa_k[j],
                                   None if cache_swa_v is None else cache_swa_v[j])
            new_k.append(kb)
            new_v.append(vb)
        elif kind == 1:
            y, s_new, c_new = _gdn_mixer(h, w['gdn_w_in'][j], w['gdn_conv_w'][j], w['gdn_a_log'][j],
                                         w['gdn_dt_bias'][j], w['gdn_norm'][j], w['gdn_w_out'][j],
                                         None if state_gdn is None else state_gdn[j],
                                         None if state_gdn_conv is None else state_gdn_conv[j])
            new_gdn.append(s_new)
            new_conv.append(c_new)
        else:
            y, s_new = _hgrn_mixer(h, lb_all[i], w['hgrn_w_in'][j], w['hgrn_norm'][j], w['hgrn_w_out'][j],
                                   None if state_hgrn is None else state_hgrn[j])
            new_hgrn.append(s_new)
        x = x + y
        x = x + 0.5 * _swiglu(_rmsnorm(x, w['norm_ffn'][i, 1]), w['ffn_w_in'][i, 1], w['ffn_w_out'][i, 1])
    y_out = _rmsnorm(x, w['final_norm'])
    return (y_out, jnp.stack(new_k), jnp.stack(new_v), jnp.stack(new_gdn),
            jnp.stack(new_conv), jnp.stack(new_hgrn))


def setup_inputs(seed: int = 0) -> dict:
    key = jax.random.key(seed)
    ks = jax.random.split(key, 32)

    def nrm(i, shape, scale):
        return scale * jax.random.normal(ks[i], shape, jnp.float32)

    dt = jnp.exp(jax.random.uniform(ks[20], (N_GDN, GDN_V_HEADS), jnp.float32,
                                    math.log(1e-3), math.log(1e-1)))
    return {
        'x_prompt': nrm(0, (BATCH, SEQ, D_MODEL), 1.0),
        'x_sample': nrm(1, (DEC_BATCH, DEC_SEQ, D_MODEL), 1.0),
        'cache_swa_k': nrm(2, (N_SWA, DEC_BATCH, WINDOW, SWA_KV_HEADS, SWA_HEAD_DIM), 1.0),
        'cache_swa_v': nrm(3, (N_SWA, DEC_BATCH, WINDOW, SWA_KV_HEADS, SWA_HEAD_DIM), 1.0),
        'state_gdn': nrm(4, (N_GDN, DEC_BATCH, GDN_V_HEADS, GDN_HEAD_DIM, GDN_HEAD_DIM), 0.1),
        'state_gdn_conv': nrm(5, (N_GDN, DEC_BATCH, GDN_CONV - 1, GDN_CONV_DIM), 1.0),
        'state_hgrn': nrm(6, (N_HGRN, DEC_BATCH, HG_HEADS, HG_HEAD_DIM, HG_HEAD_DIM), 0.3),
        'norm_ffn': 1.0 + nrm(7, (DEPTH, 2, D_MODEL), 0.02),
        'ffn_w_in': nrm(8, (DEPTH, 2, D_MODEL, 2 * D_FF), D_MODEL ** -0.5),
        'ffn_w_out': nrm(9, (DEPTH, 2, D_FF, D_MODEL), D_FF ** -0.5),
        'norm_mix': 1.0 + nrm(10, (DEPTH, D_MODEL), 0.02),
        'swa_w_in': nrm(11, (N_SWA, D_MODEL, SWA_IN), D_MODEL ** -0.5),
        'swa_w_out': nrm(12, (N_SWA, SWA_HEADS * SWA_HEAD_DIM, D_MODEL), (SWA_HEADS * SWA_HEAD_DIM) ** -0.5),
        'swa_sinks': nrm(13, (N_SWA, SWA_HEADS), 0.5),
        'gdn_w_in': nrm(14, (N_GDN, D_MODEL, GDN_IN), D_MODEL ** -0.5),
        'gdn_conv_w': nrm(15, (N_GDN, GDN_CONV, GDN_CONV_DIM), 0.5),
        'gdn_a_log': jnp.log(jax.random.uniform(ks[21], (N_GDN, GDN_V_HEADS), jnp.float32, 1.0, 16.0)),
        'gdn_dt_bias': dt + jnp.log(-jnp.expm1(-dt)),
        'gdn_norm': 1.0 + nrm(16, (N_GDN, GDN_HEAD_DIM), 0.02),
        'gdn_w_out': nrm(17, (N_GDN, GDN_V_DIM, D_MODEL), GDN_V_DIM ** -0.5),
        'hgrn_w_in': nrm(18, (N_HGRN, D_MODEL, 4 * HG_DIM), D_MODEL ** -0.5),
        'hgrn_lb_logits': nrm(19, (DEPTH, HG_DIM), 0.5),
        'hgrn_norm': 1.0 + nrm(22, (N_HGRN, HG_HEAD_DIM), 0.02),
        'hgrn_w_out': nrm(23, (N_HGRN, HG_DIM, D_MODEL), HG_DIM ** -0.5),
        'final_norm': 1.0 + nrm(24, (D_MODEL,), 0.02),
    }


def reference(x_prompt, x_sample, cache_swa_k, cache_swa_v, state_gdn, state_gdn_conv, state_hgrn,
              norm_ffn, ffn_w_in, ffn_w_out, norm_mix, swa_w_in, swa_w_out, swa_sinks,
              gdn_w_in, gdn_conv_w, gdn_a_log, gdn_dt_bias, gdn_norm, gdn_w_out,
              hgrn_w_in, hgrn_lb_logits, hgrn_norm, hgrn_w_out, final_norm):
    w = dict(norm_ffn=norm_ffn, ffn_w_in=ffn_w_in, ffn_w_out=ffn_w_out, norm_mix=norm_mix,
             swa_w_in=swa_w_in, swa_w_out=swa_w_out, swa_sinks=swa_sinks,
             gdn_w_in=gdn_w_in, gdn_conv_w=gdn_conv_w, gdn_a_log=gdn_a_log, gdn_dt_bias=gdn_dt_bias,
             gdn_norm=gdn_norm, gdn_w_out=gdn_w_out, hgrn_w_in=hgrn_w_in, hgrn_lb_logits=hgrn_lb_logits,
             hgrn_norm=hgrn_norm, hgrn_w_out=hgrn_w_out, final_norm=final_norm)
    y_prompt, pk, pv, pg, pc, ph = _trunk(x_prompt, 0, None, None, None, None, None, w)
    y_sample, sk, sv, sg, sc, sh = _trunk(x_sample, PAST_LEN, cache_swa_k, cache_swa_v,
                                          state_gdn, state_gdn_conv, state_hgrn, w)
    return (y_prompt, y_sample, pk, pv, pg, pc, ph, sk, sv, sg, sc, sh)
```

```python
import functools
import math

import jax
import jax.numpy as jnp
from jax import lax
from jax.experimental import pallas as pl
from jax.experimental.pallas import tpu as pltpu

D_MODEL = 1024
BATCH = 8
SEQ = 2048
DEPTH = 4
DEC_BATCH = 128
DEC_SEQ = 4
PAST_LEN = 8192
N_MIXERS = 3

SWA_HEAD_DIM = 64
SWA_HEADS = 16
SWA_KV_HEADS = 4
SWA_GROUP = 4
WINDOW = 128
ROT_DIM = 16
ROPE_THETA = 500000.0
SWA_Q = SWA_HEADS * SWA_HEAD_DIM
SWA_KV = SWA_KV_HEADS * SWA_HEAD_DIM

GDN_HEAD_DIM = 128
GDN_QK_HEADS = 8
GDN_V_HEADS = 16
GDN_KEY_DIM = 1024
GDN_V_DIM = 2048
GDN_CONV_DIM = 4096
GDN_CONV = 4

HG_HEAD_DIM = 128
HG_HEADS = 8
HG_DIM = 1024

D_FF = 2816
NORM_EPS = 1e-6
NEG_INF = -1e30

LANES = 128
SUBLANES = 8
TOKEN_TILE = 512
N_PROMPT = BATCH * SEQ
N_SAMPLE = DEC_BATCH * DEC_SEQ
N_TOKENS = N_PROMPT + N_SAMPLE
N_PROMPT_TILES = N_PROMPT // TOKEN_TILE
FFN_CHUNK = D_FF // 2
VMEM_LIMIT = 56 * 1024 * 1024
GDN_CHUNK = 64
HG_CHUNK = 32
PAD_CHUNK = 8
SWA_SAMPLE_BLOCK = 8

F32 = jnp.float32
BF16 = jnp.bfloat16


def _const_spec(shape):
    nd = len(shape)
    return pl.BlockSpec(shape, lambda *_: (0,) * nd, pipeline_mode=pl.Buffered(1))


def _rms(x, g):
    return x * lax.rsqrt(jnp.mean(x * x, axis=-1, keepdims=True) + NORM_EPS) * g


def _sigmoid(x):
    return 1.0 / (1.0 + jnp.exp(-x))


def _silu(x):
    return x * _sigmoid(x)


def _dot(a, b):
    return jnp.dot(a.astype(BF16), b.astype(BF16), preferred_element_type=F32)


def _dot_nt(a, b):
    return lax.dot_general(a.astype(BF16), b.astype(BF16), (((1,), (1,)), ((), ())),
                           preferred_element_type=F32)


def _dot_tn(a, b):
    return lax.dot_general(a.astype(BF16), b.astype(BF16), (((0,), (0,)), ((), ())),
                           preferred_element_type=F32)


def _dot_f32(a, b):
    return jnp.dot(a, b, preferred_element_type=F32, precision=lax.Precision.HIGHEST)


def _ffn_kernel(*refs, has_proj, final_norm):
    it = iter(refs)
    x_ref = next(it)
    if has_proj:
        op_ref, os_ref, wp_ref = next(it), next(it), next(it)
    g_ref, wg_ref, wu_ref, wo_ref = next(it), next(it), next(it), next(it)
    gf_ref = next(it) if final_norm else None
    out_ref = next(it)

    x = x_ref[...]
    if has_proj:
        o = jnp.where(pl.program_id(0) < N_PROMPT_TILES, op_ref[...], os_ref[...])
        x = x + jnp.dot(o, wp_ref[...], preferred_element_type=F32)
    h = _rms(x, g_ref[...]).astype(BF16)
    y = jnp.zeros_like(x)
    for c in range(D_FF // FFN_CHUNK):
        cols = slice(c * FFN_CHUNK, (c + 1) * FFN_CHUNK)
        gate = jnp.dot(h, wg_ref[:, cols], preferred_element_type=F32)
        up = jnp.dot(h, wu_ref[:, cols], preferred_element_type=F32)
        act = (_silu(gate) * up).astype(BF16)
        y = y + jnp.dot(act, wo_ref[cols, :], preferred_element_type=F32)
    x = x + 0.5 * y
    if final_norm:
        x = _rms(x, gf_ref[...])
    out_ref[...] = x


def _ffn(x, g, w_in, w_out, proj=None, final_g=None):
    tile = pl.BlockSpec((TOKEN_TILE, D_MODEL), lambda i: (i, 0))
    in_specs, args = [tile], [x]
    if proj is not None:
        o_p, o_s, w_p = proj
        k = w_p.shape[0]
        in_specs += [pl.BlockSpec((TOKEN_TILE, k), lambda i: (jnp.minimum(i, N_PROMPT_TILES - 1), 0)),
                     pl.BlockSpec((TOKEN_TILE, k), lambda i: (0, 0)),
                     _const_spec((k, D_MODEL))]
        args += [o_p, o_s, w_p]
    in_specs += [_const_spec((1, D_MODEL)),
                 pl.BlockSpec((D_MODEL, D_FF), lambda i: (0, 0), pipeline_mode=pl.Buffered(1)),
                 pl.BlockSpec((D_MODEL, D_FF), lambda i: (0, 1), pipeline_mode=pl.Buffered(1)),
                 _const_spec((D_FF, D_MODEL))]
    args += [g.reshape(1, D_MODEL), w_in, w_in, w_out]
    if final_g is not None:
        in_specs.append(_const_spec((1, D_MODEL)))
        args.append(final_g.reshape(1, D_MODEL))
    return pl.pallas_call(
        functools.partial(_ffn_kernel, has_proj=proj is not None, final_norm=final_g is not None),
        grid=(N_TOKENS // TOKEN_TILE,),
        in_specs=in_specs,
        out_specs=tile,
        out_shape=jax.ShapeDtypeStruct((N_TOKENS, D_MODEL), F32),
        compiler_params=pltpu.CompilerParams(dimension_semantics=("arbitrary",), vmem_limit_bytes=VMEM_LIMIT),
        name="ffn",
    )(*args)


def _norm_proj_kernel(*refs, n_out):
    x_ref, g_ref = refs[0], refs[1]
    w_refs = refs[2:2 + n_out]
    out_refs = refs[2 + n_out:]
    h = _rms(x_ref[...], g_ref[...]).astype(BF16)
    for w_ref, out_ref in zip(w_refs, out_refs):
        out_ref[...] = jnp.dot(h, w_ref[...], preferred_element_type=F32)


def _norm_proj(x, g, weights, tile_rows):
    in_specs = [pl.BlockSpec((tile_rows, D_MODEL), lambda i: (i, 0)), _const_spec((1, D_MODEL))]
    in_specs += [_const_spec(w.shape) for w in weights]
    return pl.pallas_call(
        functools.partial(_norm_proj_kernel, n_out=len(weights)),
        grid=(N_TOKENS // tile_rows,),
        in_specs=in_specs,
        out_specs=[pl.BlockSpec((tile_rows, w.shape[1]), lambda i: (i, 0)) for w in weights],
        out_shape=[jax.ShapeDtypeStruct((N_TOKENS, w.shape[1]), F32) for w in weights],
        compiler_params=pltpu.CompilerParams(dimension_semantics=("arbitrary",), vmem_limit_bytes=VMEM_LIMIT),
        name="norm_proj",
    )(x, g.reshape(1, D_MODEL), *weights)


def _rope_tables(pos, width):
    half = ROT_DIM // 2
    inv_freq = ROPE_THETA ** (-jnp.arange(0, ROT_DIM, 2, dtype=F32) / ROT_DIM)
    ang = pos.astype(F32)[:, None] * inv_freq[None, :]
    cos, sin = jnp.cos(ang), jnp.sin(ang)
    n = pos.shape[0]
    rest = SWA_HEAD_DIM - ROT_DIM
    c = jnp.concatenate([cos, cos, jnp.ones((n, rest), F32)], axis=1)
    s_lo = jnp.concatenate([-sin, jnp.zeros((n, half + rest), F32)], axis=1)
    s_hi = jnp.concatenate([jnp.zeros((n, half), F32), sin, jnp.zeros((n, rest), F32)], axis=1)
    reps = width // SWA_HEAD_DIM
    return tuple(jnp.tile(t, (1, reps)) for t in (c, s_lo, s_hi))


def _rope(x, c, s_lo, s_hi):
    half = ROT_DIM // 2
    outs = []
    for j in range(x.shape[1] // LANES):
        xs = x[:, j * LANES:(j + 1) * LANES]
        outs.append(xs * c + pltpu.roll(xs, LANES - half, 1) * s_lo + pltpu.roll(xs, half, 1) * s_hi)
    return outs[0] if len(outs) == 1 else jnp.concatenate(outs, axis=1)


def _swa_prompt_kernel(sink_ref, q_ref, k_ref, v_ref, c_ref, slo_ref, shi_ref,
                       o_ref, kc_ref, vc_ref, kprev, vprev):
    n = pl.program_id(1)

    @pl.when(n == 0)
    def _():
        kprev[...] = jnp.zeros_like(kprev)
        vprev[...] = jnp.zeros_like(vprev)

    c, s_lo, s_hi = c_ref[...], slo_ref[...], shi_ref[...]
    q = _rope(q_ref[...], c, s_lo, s_hi) * (SWA_HEAD_DIM ** -0.5)
    k_cur = _rope(k_ref[...], c, s_lo, s_hi)
    v_cur = v_ref[...]
    k_old, v_old = kprev[...], vprev[...]

    qi = lax.broadcasted_iota(jnp.int32, (WINDOW, WINDOW), 0)
    kj = lax.broadcasted_iota(jnp.int32, (WINDOW, WINDOW), 1)
    mask_cur = kj <= qi
    mask_old = kj > qi + jnp.where(n > 0, 0, WINDOW)

    for h in range(SWA_HEADS):
        kv = h // SWA_GROUP
        kv_cols = slice(kv * SWA_HEAD_DIM, (kv + 1) * SWA_HEAD_DIM)
        qh = q[:, h * SWA_HEAD_DIM:(h + 1) * SWA_HEAD_DIM]
        s_c = jnp.where(mask_cur, _dot_nt(qh, k_cur[:, kv_cols]), NEG_INF)
        s_o = jnp.where(mask_old, _dot_nt(qh, k_old[:, kv_cols]), NEG_INF)
        sink = sink_ref[h]
        m = jnp.maximum(jnp.maximum(jnp.max(s_c, axis=-1, keepdims=True),
                                    jnp.max(s_o, axis=-1, keepdims=True)), sink)
        p_c = jnp.exp(s_c - m)
        p_o = jnp.exp(s_o - m)
        denom = (jnp.sum(p_c, axis=-1, keepdims=True) + jnp.sum(p_o, axis=-1, keepdims=True)
                 + jnp.exp(sink - m))
        o = (_dot(p_c, v_cur[:, kv_cols]) + _dot(p_o, v_old[:, kv_cols])) / denom
        o_ref[:, h * SWA_HEAD_DIM:(h + 1) * SWA_HEAD_DIM] = o.astype(o_ref.dtype)

    kprev[...] = k_cur
    vprev[...] = v_cur

    @pl.when(n == pl.num_programs(1) - 1)
    def _():
        kc_ref[0] = k_cur
        vc_ref[0] = v_cur


def _swa_prompt(qkv, sinks, tables):
    nb = SEQ // WINDOW
    q_blocks = SWA_Q // SWA_KV
    tab = pl.BlockSpec((WINDOW, LANES), lambda b, n, *_: (n, 0))
    cache = pl.BlockSpec((1, WINDOW, SWA_KV), lambda b, n, *_: (b, 0, 0))
    return pl.pallas_call(
        _swa_prompt_kernel,
        grid_spec=pltpu.PrefetchScalarGridSpec(
            num_scalar_prefetch=1,
            grid=(BATCH, nb),
            in_specs=[pl.BlockSpec((WINDOW, SWA_Q), lambda b, n, *_: (b * nb + n, 0)),
                      pl.BlockSpec((WINDOW, SWA_KV), lambda b, n, *_: (b * nb + n, q_blocks)),
                      pl.BlockSpec((WINDOW, SWA_KV), lambda b, n, *_: (b * nb + n, q_blocks + 1)),
                      tab, tab, tab],
            out_specs=[pl.BlockSpec((WINDOW, SWA_Q), lambda b, n, *_: (b * nb + n, 0)), cache, cache],
            scratch_shapes=[pltpu.VMEM((WINDOW, SWA_KV), F32), pltpu.VMEM((WINDOW, SWA_KV), F32)]),
        out_shape=[jax.ShapeDtypeStruct((N_PROMPT, SWA_Q), BF16),
                   jax.ShapeDtypeStruct((BATCH, WINDOW, SWA_KV), F32),
                   jax.ShapeDtypeStruct((BATCH, WINDOW, SWA_KV), F32)],
        compiler_params=pltpu.CompilerParams(dimension_semantics=("arbitrary", "arbitrary")),
        name="swa_prompt",
    )(sinks, qkv, qkv, qkv, *tables)


def _swa_sample_kernel(q_ref, kn_ref, vn_ref, kc_ref, vc_ref, sink_ref,
                       qc_ref, qlo_ref, qhi_ref, kc_tab, klo_tab, khi_tab,
                       o_ref, ko_ref, vo_ref, kext, vext):
    rows = SWA_KV_HEADS * DEC_SEQ * SWA_GROUP
    ext = WINDOW + PAD_CHUNK
    r = lax.broadcasted_iota(jnp.int32, (rows, ext), 0)
    j = lax.broadcasted_iota(jnp.int32, (rows, ext), 1)
    t = jnp.bitwise_and(jnp.right_shift(r, 2), DEC_SEQ - 1)
    valid = jnp.logical_and(j > t, j <= t + WINDOW)
    sink = sink_ref[:, 0:1]
    lane_head = jnp.right_shift(lax.broadcasted_iota(jnp.int32, (DEC_SEQ * SWA_GROUP, SWA_KV), 1), 6)
    zero_tail = jnp.zeros((PAD_CHUNK - DEC_SEQ, SWA_KV), F32)

    for b in range(SWA_SAMPLE_BLOCK):
        kext[0:WINDOW, :] = kc_ref[b]
        vext[0:WINDOW, :] = vc_ref[b]
        kext[WINDOW:WINDOW + DEC_SEQ, :] = _rope(kn_ref[b], kc_tab[...], klo_tab[...], khi_tab[...])
        vext[WINDOW:WINDOW + DEC_SEQ, :] = vn_ref[b]
        kext[WINDOW + DEC_SEQ:ext, :] = zero_tail
        vext[WINDOW + DEC_SEQ:ext, :] = zero_tail

        q = _rope(q_ref[b], qc_ref[...], qlo_ref[...], qhi_ref[...]) * (SWA_HEAD_DIM ** -0.5)
        s = jnp.where(valid, _dot_nt(q, kext[...]), NEG_INF)
        m = jnp.maximum(jnp.max(s, axis=-1, keepdims=True), sink)
        p = jnp.exp(s - m)
        denom = jnp.sum(p, axis=-1, keepdims=True) + jnp.exp(sink - m)
        o_all = _dot(p, vext[...]) / denom
        per_head = DEC_SEQ * SWA_GROUP
        o = jnp.zeros((per_head, SWA_KV), F32)
        for h in range(SWA_KV_HEADS):
            o = o + jnp.where(lane_head == h, o_all[h * per_head:(h + 1) * per_head, :], 0.0)
        o_ref[b] = o.astype(o_ref.dtype)
        ko_ref[b] = kext[DEC_SEQ:DEC_SEQ + WINDOW, :]
        vo_ref[b] = vext[DEC_SEQ:DEC_SEQ + WINDOW, :]


def _swa_sample(qkv_s, cache_k, cache_v, sinks, pos):
    q = qkv_s[:, :SWA_Q].reshape(DEC_BATCH, DEC_SEQ, SWA_KV_HEADS, SWA_GROUP, SWA_HEAD_DIM)
    q = jnp.transpose(q, (0, 2, 1, 3, 4))
    eye = jnp.eye(SWA_KV_HEADS, dtype=F32)
    rows = SWA_KV_HEADS * DEC_SEQ * SWA_GROUP
    q_bd = (q[:, :, :, :, None, :] * eye[None, :, None, None, :, None]).reshape(DEC_BATCH, rows, SWA_KV)
    k_new = qkv_s[:, SWA_Q:SWA_Q + SWA_KV].reshape(DEC_BATCH, DEC_SEQ, SWA_KV)
    v_new = qkv_s[:, SWA_Q + SWA_KV:].reshape(DEC_BATCH, DEC_SEQ, SWA_KV)
    row_t = (jnp.arange(rows) // SWA_GROUP) % DEC_SEQ
    q_tabs = _rope_tables(pos[row_t], LANES)
    k_tabs = _rope_tables(pos, LANES)
    row_head = (jnp.arange(rows) // (DEC_SEQ * SWA_GROUP)) * SWA_GROUP + jnp.arange(rows) % SWA_GROUP
    sink_rows = jnp.broadcast_to(sinks[row_head][:, None], (rows, LANES))

    nblk = SWA_SAMPLE_BLOCK
    blk = lambda r, c: pl.BlockSpec((nblk, r, c), lambda i: (i, 0, 0))
    per_head = DEC_SEQ * SWA_GROUP
    o, ko, vo = pl.pallas_call(
        _swa_sample_kernel,
        grid=(DEC_BATCH // nblk,),
        in_specs=[blk(rows, SWA_KV), blk(DEC_SEQ, SWA_KV), blk(DEC_SEQ, SWA_KV),
                  blk(WINDOW, SWA_KV), blk(WINDOW, SWA_KV), _const_spec((rows, LANES)),
                  _const_spec((rows, LANES)), _const_spec((rows, LANES)), _const_spec((rows, LANES)),
                  _const_spec((DEC_SEQ, LANES)), _const_spec((DEC_SEQ, LANES)), _const_spec((DEC_SEQ, LANES))],
        out_specs=[blk(per_head, SWA_KV), blk(WINDOW, SWA_KV), blk(WINDOW, SWA_KV)],
        out_shape=[jax.ShapeDtypeStruct((DEC_BATCH, per_head, SWA_KV), BF16),
                   jax.ShapeDtypeStruct((DEC_BATCH, WINDOW, SWA_KV), F32),
                   jax.ShapeDtypeStruct((DEC_BATCH, WINDOW, SWA_KV), F32)],
        scratch_shapes=[pltpu.VMEM((WINDOW + PAD_CHUNK, SWA_KV), F32), pltpu.VMEM((WINDOW + PAD_CHUNK, SWA_KV), F32)],
        compiler_params=pltpu.CompilerParams(dimension_semantics=("arbitrary",)),
        name="swa_sample",
    )(q_bd, k_new, v_new, cache_k, cache_v, sink_rows, *q_tabs, *k_tabs)
    o = o.reshape(DEC_BATCH, DEC_SEQ, SWA_GROUP, SWA_KV_HEADS, SWA_HEAD_DIM)
    o = jnp.transpose(o, (0, 1, 3, 2, 4)).reshape(N_SAMPLE, SWA_Q)
    return o, ko, vo


def _mixer_out_dtype(chunk):
    return BF16 if chunk % (2 * SUBLANES) == 0 else F32


def _tri(n, strict=False):
    i = lax.broadcasted_iota(jnp.int32, (n, n), 0)
    j = lax.broadcasted_iota(jnp.int32, (n, n), 1)
    return (j < i) if strict else (j <= i)


def _gdn_kernel(*refs, chunk, n_valid, has_state):
    it = iter(refs)
    x_ref, z_ref, b_ref, a_ref = next(it), next(it), next(it), next(it)
    cw_ref, alog_ref, dtb_ref, ng_ref = next(it), next(it), next(it), next(it)
    if has_state:
        s0_ref, c0_ref = next(it), next(it)
    o_ref, s_ref = next(it), next(it)
    xbuf = next(it)
    c = pl.program_id(1)

    @pl.when(c == 0)
    def _():
        if has_state:
            s_ref[...] = s0_ref[...]
            xbuf[0:SUBLANES - GDN_CONV + 1, :] = jnp.zeros((SUBLANES - GDN_CONV + 1, GDN_CONV_DIM), F32)
            xbuf[SUBLANES - GDN_CONV + 1:SUBLANES, :] = c0_ref[0]
        else:
            s_ref[...] = jnp.zeros_like(s_ref)
            xbuf[0:SUBLANES, :] = jnp.zeros((SUBLANES, GDN_CONV_DIM), F32)

    x = x_ref[...]
    xbuf[SUBLANES:SUBLANES + chunk, :] = x
    acc = x * cw_ref[GDN_CONV - 1:GDN_CONV, :]
    for tap in range(GDN_CONV - 1):
        start = SUBLANES - (GDN_CONV - 1) + tap
        acc = acc + xbuf[start:start + chunk, :] * cw_ref[tap:tap + 1, :]
    xbuf[0:SUBLANES, :] = x[chunk - SUBLANES:chunk, :]
    xc = _silu(acc)

    row = lax.broadcasted_iota(jnp.int32, (chunk, LANES), 0)
    live = row < n_valid
    beta = jnp.where(live, _sigmoid(b_ref[...]), 0.0)
    a = a_ref[...] + dtb_ref[...]
    softplus = jnp.maximum(a, 0.0) + jnp.log(1.0 + jnp.exp(-jnp.abs(a)))
    g = jnp.where(live, -jnp.exp(alog_ref[...]) * softplus, 0.0)

    incl = _tri(chunk)
    strict = _tri(chunk, strict=True)
    incl_f = incl.astype(F32)
    gcum = _dot_f32(incl_f, g)
    upper_f = (lax.broadcasted_iota(jnp.int32, (chunk, chunk), 0)
               <= lax.broadcasted_iota(jnp.int32, (chunk, chunk), 1)).astype(F32)
    gcum_t = lax.dot_general(g, upper_f, (((0,), (0,)), ((), ())), preferred_element_type=F32,
                             precision=lax.Precision.HIGHEST)
    eye = (lax.broadcasted_iota(jnp.int32, (chunk, chunk), 0)
           == lax.broadcasted_iota(jnp.int32, (chunk, chunk), 1)).astype(F32)
    n_double = int(math.log2(chunk)) - 1
    ng = ng_ref[...]

    for hq in range(GDN_QK_HEADS):
        qh = xc[:, hq * GDN_HEAD_DIM:(hq + 1) * GDN_HEAD_DIM]
        kh = xc[:, GDN_KEY_DIM + hq * GDN_HEAD_DIM:GDN_KEY_DIM + (hq + 1) * GDN_HEAD_DIM]
        qn = qh * lax.rsqrt(jnp.sum(qh * qh, axis=-1, keepdims=True) + NORM_EPS) * (GDN_HEAD_DIM ** -0.5)
        kn = kh * lax.rsqrt(jnp.sum(kh * kh, axis=-1, keepdims=True) + NORM_EPS)
        kk = _dot_nt(kn, kn)
        qk = _dot_nt(qn, kn)
        for h in range(hq * (GDN_V_HEADS // GDN_QK_HEADS), (hq + 1) * (GDN_V_HEADS // GDN_QK_HEADS)):
            vh = xc[:, 2 * GDN_KEY_DIM + h * GDN_HEAD_DIM:2 * GDN_KEY_DIM + (h + 1) * GDN_HEAD_DIM]
            gc = gcum[:, h:h + 1]
            gr = gcum_t[h:h + 1, :]
            bc = beta[:, h:h + 1]
            decay = jnp.exp(jnp.where(incl, gc - gr, NEG_INF))
            a_mat = jnp.where(strict, kk * decay, 0.0) * bc
            t_mat = eye - a_mat
            p_mat = a_mat
            for _ in range(n_double):
                p_mat = _dot(p_mat, p_mat)
                t_mat = t_mat + _dot(t_mat, p_mat)
            s = s_ref[0, h]
            egc = jnp.exp(gc)
            rhs = bc * (vh - egc * _dot(kn, s))
            v_new = _dot(t_mat, rhs)
            o = _dot(qn * egc, s) + _dot(qk * decay, v_new)
            g_last = gcum[chunk - 1:chunk, h:h + 1]
            s_ref[0, h] = s * jnp.exp(g_last) + _dot_tn(kn * jnp.exp(g_last - gc), v_new)
            zh = z_ref[:, h * GDN_HEAD_DIM:(h + 1) * GDN_HEAD_DIM]
            o_ref[:, h * GDN_HEAD_DIM:(h + 1) * GDN_HEAD_DIM] = (_rms(o, ng) * _silu(zh)).astype(o_ref.dtype)


def _gdn(x, z, b, a, conv_w, a_log, dt_bias, norm_g, n_seq, n_chunks, chunk, n_valid, state=None, conv0=None):
    has_state = state is not None
    rows = n_seq * n_chunks * chunk
    blk = lambda w: pl.BlockSpec((chunk, w), lambda s, c: (s * n_chunks + c, 0))
    pad_heads = lambda v: jnp.pad(v.astype(F32), (0, LANES - GDN_V_HEADS)).reshape(1, LANES)
    state_spec = pl.BlockSpec((1, GDN_V_HEADS, GDN_HEAD_DIM, GDN_HEAD_DIM), lambda s, c: (s, 0, 0, 0))
    in_specs = [blk(GDN_CONV_DIM), blk(GDN_V_DIM), blk(LANES), blk(LANES),
                _const_spec((GDN_CONV, GDN_CONV_DIM)), _const_spec((1, LANES)), _const_spec((1, LANES)),
                _const_spec((1, GDN_HEAD_DIM))]
    args = [x, z, b, a, conv_w, pad_heads(a_log), pad_heads(dt_bias), norm_g.reshape(1, GDN_HEAD_DIM)]
    if has_state:
        in_specs += [state_spec, pl.BlockSpec((1, GDN_CONV - 1, GDN_CONV_DIM), lambda s, c: (s, 0, 0))]
        args += [state, conv0]
    return pl.pallas_call(
        functools.partial(_gdn_kernel, chunk=chunk, n_valid=n_valid, has_state=has_state),
        grid=(n_seq, n_chunks),
        in_specs=in_specs,
        out_specs=[blk(GDN_V_DIM), state_spec],
        out_shape=[jax.ShapeDtypeStruct((rows, GDN_V_DIM), _mixer_out_dtype(chunk)),
                   jax.ShapeDtypeStruct((n_seq, GDN_V_HEADS, GDN_HEAD_DIM, GDN_HEAD_DIM), F32)],
        scratch_shapes=[pltpu.VMEM((SUBLANES + chunk, GDN_CONV_DIM), F32)],
        compiler_params=pltpu.CompilerParams(dimension_semantics=("arbitrary", "arbitrary"),
                                             vmem_limit_bytes=VMEM_LIMIT),
        name="gdn",
    )(*args)


def _hgrn_kernel(*refs, chunk, n_valid, has_state, layer):
    it = iter(refs)
    x_ref, lbl_ref, ng_ref = next(it), next(it), next(it)
    if has_state:
        s0_ref = next(it)
    o_ref, s_ref = next(it), next(it)
    c = pl.program_id(1)

    @pl.when(c == 0)
    def _():
        if has_state:
            s_ref[...] = s0_ref[...]
        else:
            s_ref[...] = jnp.zeros_like(s_ref)

    logits = [lbl_ref[i:i + 1, :] for i in range(DEPTH)]
    mx = functools.reduce(jnp.maximum, logits)
    ex = [jnp.exp(l - mx) for l in logits]
    tot = functools.reduce(lambda u, v: u + v, ex)
    probs = [e / tot for e in ex]
    lb = functools.reduce(lambda u, v: u + v, probs[:layer + 1]) - probs[0]

    row = lax.broadcasted_iota(jnp.int32, (chunk, HG_DIM), 0)
    live = row < n_valid
    fz = x_ref[:, HG_DIM:2 * HG_DIM]
    log_f = jnp.where(live, jnp.log(lb + (1.0 - lb) * _sigmoid(fz)), 0.0)
    k_all = jnp.where(live, (1.0 - lb) * _sigmoid(-fz), 0.0)
    q_all = _silu(x_ref[:, 0:HG_DIM])
    incl_f = _tri(chunk).astype(F32)
    bcum = _dot_f32(incl_f, log_f)
    ones_c = jnp.ones((chunk, LANES), F32)
    ones_l = jnp.ones((LANES, LANES), BF16)
    ng = ng_ref[...]
    n_blk = chunk // SUBLANES
    sub = lax.broadcasted_iota(jnp.int32, (SUBLANES, LANES), 0)

    for h in range(HG_HEADS):
        cols = slice(h * HG_HEAD_DIM, (h + 1) * HG_HEAD_DIM)
        q, k, bc = q_all[:, cols], k_all[:, cols], bcum[:, cols]
        v = x_ref[:, 2 * HG_DIM + h * HG_HEAD_DIM:2 * HG_DIM + (h + 1) * HG_HEAD_DIM]
        s = s_ref[0, h]
        o_inter = _dot(q * jnp.exp(bc), s)
        tiles, meta = [], []
        for bi in range(n_blk):
            qi = q[bi * SUBLANES:(bi + 1) * SUBLANES, :]
            bi_rows = bc[bi * SUBLANES:(bi + 1) * SUBLANES, :]
            for j in range((bi + 1) * SUBLANES):
                e = bi_rows - bc[j:j + 1, :]
                if j >= bi * SUBLANES:
                    e = jnp.where(sub >= j - bi * SUBLANES, e, NEG_INF)
                tiles.append(qi * k[j:j + 1, :] * jnp.exp(e))
                meta.append((bi, j))
        w_sum = jnp.dot(jnp.concatenate(tiles, axis=0).astype(BF16), ones_l, preferred_element_type=F32)
        o_blocks = [None] * n_blk
        for idx, (bi, j) in enumerate(meta):
            term = w_sum[idx * SUBLANES:(idx + 1) * SUBLANES, :] * v[j:j + 1, :]
            o_blocks[bi] = term if o_blocks[bi] is None else o_blocks[bi] + term
        o_intra = o_blocks[0] if n_blk == 1 else jnp.concatenate(o_blocks, axis=0)
        o = o_inter + o_intra
        b_last = bc[chunk - 1:chunk, :]
        b_last_col = lax.dot_general(log_f[:, cols], ones_c, (((0,), (0,)), ((), ())),
                                     preferred_element_type=F32, precision=lax.Precision.HIGHEST)
        s_ref[0, h] = s * jnp.exp(b_last_col) + _dot_tn(k * jnp.exp(b_last - bc), v)
        gate = x_ref[:, 3 * HG_DIM + h * HG_HEAD_DIM:3 * HG_DIM + (h + 1) * HG_HEAD_DIM]
        o_ref[:, cols] = (_rms(o, ng) * _silu(gate)).astype(o_ref.dtype)


def _hgrn(x, lb_logits, norm_g, layer, n_seq, n_chunks, chunk, n_valid, state=None):
    has_state = state is not None
    rows = n_seq * n_chunks * chunk
    state_spec = pl.BlockSpec((1, HG_HEADS, HG_HEAD_DIM, HG_HEAD_DIM), lambda s, c: (s, 0, 0, 0))
    in_specs = [pl.BlockSpec((chunk, 4 * HG_DIM), lambda s, c: (s * n_chunks + c, 0)),
                _const_spec((DEPTH, HG_DIM)), _const_spec((1, HG_HEAD_DIM))]
    args = [x, lb_logits, norm_g.reshape(1, HG_HEAD_DIM)]
    if has_state:
        in_specs.append(state_spec)
        args.append(state)
    return pl.pallas_call(
        functools.partial(_hgrn_kernel, chunk=chunk, n_valid=n_valid, has_state=has_state, layer=layer),
        grid=(n_seq, n_chunks),
        in_specs=in_specs,
        out_specs=[pl.BlockSpec((chunk, HG_DIM), lambda s, c: (s * n_chunks + c, 0)), state_spec],
        out_shape=[jax.ShapeDtypeStruct((rows, HG_DIM), _mixer_out_dtype(chunk)),
                   jax.ShapeDtypeStruct((n_seq, HG_HEADS, HG_HEAD_DIM, HG_HEAD_DIM), F32)],
        compiler_params=pltpu.CompilerParams(dimension_semantics=("arbitrary", "arbitrary"),
                                             vmem_limit_bytes=VMEM_LIMIT),
        name="hgrn",
    )(*args)


def _pad_sample(t):
    w = t.shape[1]
    t = jnp.pad(t.reshape(DEC_BATCH, DEC_SEQ, w), ((0, 0), (0, PAD_CHUNK - DEC_SEQ), (0, 0)))
    return t.reshape(DEC_BATCH * PAD_CHUNK, w)


def _unpad_sample(t):
    w = t.shape[1]
    return t.reshape(DEC_BATCH, PAD_CHUNK, w)[:, :DEC_SEQ].reshape(N_SAMPLE, w).astype(BF16)


def _pad_cols(w, n):
    return jnp.pad(w, ((0, 0), (0, n - w.shape[1])))


def kernel(x_prompt, x_sample, cache_swa_k, cache_swa_v, state_gdn, state_gdn_conv, state_hgrn, norm_ffn, ffn_w_in, ffn_w_out, norm_mix, swa_w_in, swa_w_out, swa_sinks, gdn_w_in, gdn_conv_w, gdn_a_log, gdn_dt_bias, gdn_norm, gdn_w_out, hgrn_w_in, hgrn_lb_logits, hgrn_norm, hgrn_w_out, final_norm):
    x = jnp.concatenate([x_prompt.reshape(N_PROMPT, D_MODEL), x_sample.reshape(N_SAMPLE, D_MODEL)], axis=0)
    ffn_in = ffn_w_in.astype(BF16)
    ffn_out = ffn_w_out.astype(BF16)
    prompt_tables = _rope_tables(jnp.arange(SEQ), LANES)
    sample_pos = PAST_LEN + jnp.arange(DEC_SEQ)

    outs = {k: [] for k in ("pk", "pv", "pg", "pc", "ph", "sk", "sv", "sg", "sc", "sh")}
    proj = None
    for i in range(DEPTH):
        kind, j = i % N_MIXERS, i // N_MIXERS
        if proj is not None:
            x = _ffn(x, norm_ffn[i - 1, 1], ffn_in[i - 1, 1], ffn_out[i - 1, 1], proj=proj)
        x = _ffn(x, norm_ffn[i, 0], ffn_in[i, 0], ffn_out[i, 0])
        if kind == 0:
            (qkv,) = _norm_proj(x, norm_mix[i], [swa_w_in[j].astype(BF16)], TOKEN_TILE)
            o_p, pk, pv = _swa_prompt(qkv, swa_sinks[j], prompt_tables)
            o_s, sk, sv = _swa_sample(qkv[N_PROMPT:], cache_swa_k[j].reshape(DEC_BATCH, WINDOW, SWA_KV),
                                      cache_swa_v[j].reshape(DEC_BATCH, WINDOW, SWA_KV), swa_sinks[j], sample_pos)
            kv_shape = (WINDOW, SWA_KV_HEADS, SWA_HEAD_DIM)
            outs["pk"].append(pk.reshape((BATCH,) + kv_shape))
            outs["pv"].append(pv.reshape((BATCH,) + kv_shape))
            outs["sk"].append(sk.reshape((DEC_BATCH,) + kv_shape))
            outs["sv"].append(sv.reshape((DEC_BATCH,) + kv_shape))
            proj = (o_p, o_s, swa_w_out[j].astype(BF16))
        elif kind == 1:
            w = gdn_w_in[j].astype(BF16)
            z0 = GDN_CONV_DIM + GDN_V_DIM
            weights = [w[:, :GDN_CONV_DIM], w[:, GDN_CONV_DIM:z0],
                       _pad_cols(w[:, z0:z0 + GDN_V_HEADS], LANES), _pad_cols(w[:, z0 + GDN_V_HEADS:], LANES)]
            qkv, z, b, a = _norm_proj(x, norm_mix[i], weights, TOKEN_TILE // 2)
            o_p, pg = _gdn(qkv, z, b, a, gdn_conv_w[j], gdn_a_log[j], gdn_dt_bias[j], gdn_norm[j],
                           BATCH, SEQ // GDN_CHUNK, GDN_CHUNK, GDN_CHUNK)
            o_s, sg = _gdn(*[_pad_sample(t[N_PROMPT:]) for t in (qkv, z, b, a)],
                           gdn_conv_w[j], gdn_a_log[j], gdn_dt_bias[j], gdn_norm[j],
                           DEC_BATCH, 1, PAD_CHUNK, DEC_SEQ, state=state_gdn[j], conv0=state_gdn_conv[j])
            outs["pg"].append(pg)
            outs["sg"].append(sg)
            keep = GDN_CONV - 1
            outs["pc"].append(qkv[:N_PROMPT].reshape(BATCH, SEQ, GDN_CONV_DIM)[:, SEQ - keep:])
            outs["sc"].append(qkv[N_PROMPT:].reshape(DEC_BATCH, DEC_SEQ, GDN_CONV_DIM)[:, DEC_SEQ - keep:])
            proj = (o_p, _unpad_sample(o_s), gdn_w_out[j].astype(BF16))
        else:
            (xin,) = _norm_proj(x, norm_mix[i], [hgrn_w_in[j].astype(BF16)], TOKEN_TILE)
            o_p, ph = _hgrn(xin, hgrn_lb_logits, hgrn_norm[j], i, BATCH, SEQ // HG_CHUNK, HG_CHUNK, HG_CHUNK)
            o_s, sh = _hgrn(_pad_sample(xin[N_PROMPT:]), hgrn_lb_logits, hgrn_norm[j], i,
                            DEC_BATCH, 1, PAD_CHUNK, DEC_SEQ, state=state_hgrn[j])
            outs["ph"].append(ph)
            outs["sh"].append(sh)
            proj = (o_p, _unpad_sample(o_s), hgrn_w_out[j].astype(BF16))
    x = _ffn(x, norm_ffn[DEPTH - 1, 1], ffn_in[DEPTH - 1, 1], ffn_out[DEPTH - 1, 1], proj=proj, final_g=final_norm)
    y_prompt = x[:N_PROMPT].reshape(BATCH, SEQ, D_MODEL)
    y_sample = x[N_PROMPT:].reshape(DEC_BATCH, DEC_SEQ, D_MODEL)
    st = lambda k: jnp.stack(outs[k])
    return (y_prompt, y_sample, st("pk"), st("pv"), st("pg"), st("pc"), st("ph"),
            st("sk"), st("sv"), st("sg"), st("sc"), st("sh"))
```

```python
import functools
import math

import jax
import jax.numpy as jnp
from jax import lax
from jax.experimental import pallas as pl
from jax.experimental.pallas import tpu as pltpu

D_MODEL = 1024
BATCH = 8
SEQ = 2048
DEPTH = 4
DEC_BATCH = 128
DEC_SEQ = 4
PAST_LEN = 8192
N_MIXERS = 3

SWA_HEAD_DIM = 64
SWA_HEADS = 16
SWA_KV_HEADS = 4
SWA_GROUP = 4
WINDOW = 128
ROT_DIM = 16
ROPE_THETA = 500000.0
SWA_Q = SWA_HEADS * SWA_HEAD_DIM
SWA_KV = SWA_KV_HEADS * SWA_HEAD_DIM

GDN_HEAD_DIM = 128
GDN_QK_HEADS = 8
GDN_V_HEADS = 16
GDN_KEY_DIM = 1024
GDN_V_DIM = 2048
GDN_CONV_DIM = 4096
GDN_CONV = 4

HG_HEAD_DIM = 128
HG_HEADS = 8
HG_DIM = 1024

D_FF = 2816
NORM_EPS = 1e-6
NEG_INF = -1e30
LOG2_E = 1.4426950408889634

LANES = 128
SUBLANES = 8
TOKEN_TILE = 512
N_PROMPT = BATCH * SEQ
N_SAMPLE = DEC_BATCH * DEC_SEQ
N_TOKENS = N_PROMPT + N_SAMPLE
N_PROMPT_TILES = N_PROMPT // TOKEN_TILE
FFN_CHUNK = D_FF // 2
VMEM_LIMIT = 56 * 1024 * 1024
GDN_CHUNK = 64
HG_CHUNK = 32
PAD_CHUNK = 8
SWA_SAMPLE_BLOCK = 8

F32 = jnp.float32
BF16 = jnp.bfloat16


def _const_spec(shape):
    nd = len(shape)
    return pl.BlockSpec(shape, lambda *_: (0,) * nd, pipeline_mode=pl.Buffered(1))


def _rms(x, g):
    return x * lax.rsqrt(jnp.mean(x * x, axis=-1, keepdims=True) + NORM_EPS) * g


def _sigmoid(x):
    return 1.0 / (1.0 + jnp.exp(-x))


def _silu(x):
    return x * _sigmoid(x)


def _dot(a, b):
    return jnp.dot(a.astype(BF16), b.astype(BF16), preferred_element_type=F32)


def _dot_nt(a, b):
    return lax.dot_general(a.astype(BF16), b.astype(BF16), (((1,), (1,)), ((), ())),
                           preferred_element_type=F32)


def _dot_tn(a, b):
    return lax.dot_general(a.astype(BF16), b.astype(BF16), (((0,), (0,)), ((), ())),
                           preferred_element_type=F32)


def _dot_f32(a, b):
    return jnp.dot(a, b, preferred_element_type=F32, precision=lax.Precision.HIGHEST)


def _ffn_kernel(*refs, has_proj, final_norm):
    it = iter(refs)
    x_ref = next(it)
    if has_proj:
        op_ref, os_ref, wp_ref = next(it), next(it), next(it)
    g_ref, wg_ref, wu_ref, wo_ref = next(it), next(it), next(it), next(it)
    gf_ref = next(it) if final_norm else None
    out_ref = next(it)

    x = x_ref[...]
    if has_proj:
        o = jnp.where(pl.program_id(0) < N_PROMPT_TILES, op_ref[...], os_ref[...])
        x = x + jnp.dot(o, wp_ref[...], preferred_element_type=F32)
    h = _rms(x, g_ref[...]).astype(BF16)
    y = jnp.zeros_like(x)
    for c in range(D_FF // FFN_CHUNK):
        cols = slice(c * FFN_CHUNK, (c + 1) * FFN_CHUNK)
        gate = jnp.dot(h, wg_ref[:, cols], preferred_element_type=F32)
        up = jnp.dot(h, wu_ref[:, cols], preferred_element_type=F32)
        act = (_silu(gate) * up).astype(BF16)
        y = y + jnp.dot(act, wo_ref[cols, :], preferred_element_type=F32)
    x = x + 0.5 * y
    if final_norm:
        x = _rms(x, gf_ref[...])
    out_ref[...] = x


def _ffn(x, g, w_in, w_out, proj=None, final_g=None):
    tile = pl.BlockSpec((TOKEN_TILE, D_MODEL), lambda i: (i, 0))
    in_specs, args = [tile], [x]
    if proj is not None:
        o_p, o_s, w_p = proj
        k = w_p.shape[0]
        in_specs += [pl.BlockSpec((TOKEN_TILE, k), lambda i: (jnp.minimum(i, N_PROMPT_TILES - 1), 0)),
                     pl.BlockSpec((TOKEN_TILE, k), lambda i: (0, 0)),
                     _const_spec((k, D_MODEL))]
        args += [o_p, o_s, w_p]
    in_specs += [_const_spec((1, D_MODEL)),
                 pl.BlockSpec((D_MODEL, D_FF), lambda i: (0, 0), pipeline_mode=pl.Buffered(1)),
                 pl.BlockSpec((D_MODEL, D_FF), lambda i: (0, 1), pipeline_mode=pl.Buffered(1)),
                 _const_spec((D_FF, D_MODEL))]
    args += [g.reshape(1, D_MODEL), w_in, w_in, w_out]
    if final_g is not None:
        in_specs.append(_const_spec((1, D_MODEL)))
        args.append(final_g.reshape(1, D_MODEL))
    return pl.pallas_call(
        functools.partial(_ffn_kernel, has_proj=proj is not None, final_norm=final_g is not None),
        grid=(N_TOKENS // TOKEN_TILE,),
        in_specs=in_specs,
        out_specs=tile,
        out_shape=jax.ShapeDtypeStruct((N_TOKENS, D_MODEL), F32),
        compiler_params=pltpu.CompilerParams(dimension_semantics=("arbitrary",), vmem_limit_bytes=VMEM_LIMIT),
        name="ffn",
    )(*args)


def _norm_proj_kernel(*refs, n_out):
    x_ref, g_ref = refs[0], refs[1]
    w_refs = refs[2:2 + n_out]
    out_refs = refs[2 + n_out:]
    h = _rms(x_ref[...], g_ref[...]).astype(BF16)
    for w_ref, out_ref in zip(w_refs, out_refs):
        out_ref[...] = jnp.dot(h, w_ref[...], preferred_element_type=F32)


def _norm_proj(x, g, weights, tile_rows):
    in_specs = [pl.BlockSpec((tile_rows, D_MODEL), lambda i: (i, 0)), _const_spec((1, D_MODEL))]
    in_specs += [_const_spec(w.shape) for w in weights]
    return pl.pallas_call(
        functools.partial(_norm_proj_kernel, n_out=len(weights)),
        grid=(N_TOKENS // tile_rows,),
        in_specs=in_specs,
        out_specs=[pl.BlockSpec((tile_rows, w.shape[1]), lambda i: (i, 0)) for w in weights],
        out_shape=[jax.ShapeDtypeStruct((N_TOKENS, w.shape[1]), F32) for w in weights],
        compiler_params=pltpu.CompilerParams(dimension_semantics=("arbitrary",), vmem_limit_bytes=VMEM_LIMIT),
        name="norm_proj",
    )(x, g.reshape(1, D_MODEL), *weights)


def _rope_tables(pos, width):
    half = ROT_DIM // 2
    inv_freq = ROPE_THETA ** (-jnp.arange(0, ROT_DIM, 2, dtype=F32) / ROT_DIM)
    ang = pos.astype(F32)[:, None] * inv_freq[None, :]
    cos, sin = jnp.cos(ang), jnp.sin(ang)
    n = pos.shape[0]
    rest = SWA_HEAD_DIM - ROT_DIM
    c = jnp.concatenate([cos, cos, jnp.ones((n, rest), F32)], axis=1)
    s_lo = jnp.concatenate([-sin, jnp.zeros((n, half + rest), F32)], axis=1)
    s_hi = jnp.concatenate([jnp.zeros((n, half), F32), sin, jnp.zeros((n, rest), F32)], axis=1)
    reps = width // SWA_HEAD_DIM
    return tuple(jnp.tile(t, (1, reps)) for t in (c, s_lo, s_hi))


def _rope(x, c, s_lo, s_hi):
    half = ROT_DIM // 2
    outs = []
    for j in range(x.shape[1] // LANES):
        xs = x[:, j * LANES:(j + 1) * LANES]
        outs.append(xs * c + pltpu.roll(xs, LANES - half, 1) * s_lo + pltpu.roll(xs, half, 1) * s_hi)
    return outs[0] if len(outs) == 1 else jnp.concatenate(outs, axis=1)


def _swa_prompt_kernel(sink_ref, q_ref, k_ref, v_ref, c_ref, slo_ref, shi_ref,
                       o_ref, kc_ref, vc_ref, kprev, vprev):
    n = pl.program_id(1)

    @pl.when(n == 0)
    def _():
        kprev[...] = jnp.zeros_like(kprev)
        vprev[...] = jnp.zeros_like(vprev)

    c, s_lo, s_hi = c_ref[...], slo_ref[...], shi_ref[...]
    q = _rope(q_ref[...], c, s_lo, s_hi) * (SWA_HEAD_DIM ** -0.5)
    k_cur = _rope(k_ref[...], c, s_lo, s_hi)
    v_cur = v_ref[...]
    keys = jnp.concatenate([kprev[...], k_cur], axis=0)
    vals = jnp.concatenate([vprev[...], v_cur], axis=0)

    rows = SWA_GROUP * WINDOW
    qi = jnp.bitwise_and(lax.broadcasted_iota(jnp.int32, (rows, 2 * WINDOW), 0), WINDOW - 1)
    kj = lax.broadcasted_iota(jnp.int32, (rows, 2 * WINDOW), 1)
    low = jnp.maximum(qi, jnp.where(n > 0, 0, WINDOW - 1))
    mask = jnp.logical_and(kj > low, kj <= qi + WINDOW)
    ones = jnp.ones((2 * WINDOW, SWA_HEAD_DIM), F32)

    stacked, scores, sinks = [], [], []
    for kv in range(SWA_KV_HEADS):
        heads = range(kv * SWA_GROUP, (kv + 1) * SWA_GROUP)
        kv_cols = slice(kv * SWA_HEAD_DIM, (kv + 1) * SWA_HEAD_DIM)
        qs = jnp.concatenate([q[:, h * SWA_HEAD_DIM:(h + 1) * SWA_HEAD_DIM] for h in heads], axis=0)
        scores.append(jnp.where(mask, _dot_nt(qs, keys[:, kv_cols]), NEG_INF))
        sinks.append(jnp.concatenate([jnp.full((WINDOW, 1), sink_ref[h], F32) for h in heads], axis=0))
    for kv in range(SWA_KV_HEADS):
        kv_cols = slice(kv * SWA_HEAD_DIM, (kv + 1) * SWA_HEAD_DIM)
        s, sink = scores[kv], sinks[kv]
        m = jnp.maximum(jnp.max(s, axis=-1, keepdims=True), sink)
        p = jnp.exp(s - m)
        pv = _dot(p, jnp.concatenate([vals[:, kv_cols], ones], axis=1))
        denom = pv[:, SWA_HEAD_DIM:SWA_HEAD_DIM + 1] + jnp.exp(sink - m)
        stacked.append(pv[:, :SWA_HEAD_DIM] / denom)
    for kv in range(SWA_KV_HEADS):
        for pair in range(SWA_GROUP // 2):
            g0 = 2 * pair
            both = jnp.concatenate([stacked[kv][g * WINDOW:(g + 1) * WINDOW] for g in (g0, g0 + 1)], axis=1)
            col = (kv * SWA_GROUP + g0) * SWA_HEAD_DIM
            o_ref[:, col:col + 2 * SWA_HEAD_DIM] = both.astype(o_ref.dtype)

    kprev[...] = k_cur
    vprev[...] = v_cur

    @pl.when(n == pl.num_programs(1) - 1)
    def _():
        kc_ref[0] = k_cur
        vc_ref[0] = v_cur


def _swa_prompt(qkv, sinks, tables):
    nb = SEQ // WINDOW
    q_blocks = SWA_Q // SWA_KV
    tab = pl.BlockSpec((WINDOW, LANES), lambda b, n, *_: (n, 0))
    cache = pl.BlockSpec((1, WINDOW, SWA_KV), lambda b, n, *_: (b, 0, 0))
    return pl.pallas_call(
        _swa_prompt_kernel,
        grid_spec=pltpu.PrefetchScalarGridSpec(
            num_scalar_prefetch=1,
            grid=(BATCH, nb),
            in_specs=[pl.BlockSpec((WINDOW, SWA_Q), lambda b, n, *_: (b * nb + n, 0)),
                      pl.BlockSpec((WINDOW, SWA_KV), lambda b, n, *_: (b * nb + n, q_blocks)),
                      pl.BlockSpec((WINDOW, SWA_KV), lambda b, n, *_: (b * nb + n, q_blocks + 1)),
                      tab, tab, tab],
            out_specs=[pl.BlockSpec((WINDOW, SWA_Q), lambda b, n, *_: (b * nb + n, 0)), cache, cache],
            scratch_shapes=[pltpu.VMEM((WINDOW, SWA_KV), F32), pltpu.VMEM((WINDOW, SWA_KV), F32)]),
        out_shape=[jax.ShapeDtypeStruct((N_PROMPT, SWA_Q), BF16),
                   jax.ShapeDtypeStruct((BATCH, WINDOW, SWA_KV), F32),
                   jax.ShapeDtypeStruct((BATCH, WINDOW, SWA_KV), F32)],
        compiler_params=pltpu.CompilerParams(dimension_semantics=("arbitrary", "arbitrary")),
        name="swa_prompt",
    )(sinks, qkv, qkv, qkv, *tables)


def _swa_sample_kernel(q_ref, kn_ref, vn_ref, kc_ref, vc_ref, sink_ref,
                       qc_ref, qlo_ref, qhi_ref, kc_tab, klo_tab, khi_tab,
                       o_ref, ko_ref, vo_ref, kext, vext):
    rows = SWA_KV_HEADS * DEC_SEQ * SWA_GROUP
    ext = WINDOW + PAD_CHUNK
    r = lax.broadcasted_iota(jnp.int32, (rows, ext), 0)
    j = lax.broadcasted_iota(jnp.int32, (rows, ext), 1)
    t = jnp.bitwise_and(jnp.right_shift(r, 2), DEC_SEQ - 1)
    valid = jnp.logical_and(j > t, j <= t + WINDOW)
    sink = sink_ref[:, 0:1]
    lane_head = jnp.right_shift(lax.broadcasted_iota(jnp.int32, (DEC_SEQ * SWA_GROUP, SWA_KV), 1), 6)
    zero_tail = jnp.zeros((PAD_CHUNK - DEC_SEQ, SWA_KV), F32)

    for b in range(SWA_SAMPLE_BLOCK):
        kext[0:WINDOW, :] = kc_ref[b]
        vext[0:WINDOW, :] = vc_ref[b]
        kext[WINDOW:WINDOW + DEC_SEQ, :] = _rope(kn_ref[b], kc_tab[...], klo_tab[...], khi_tab[...])
        vext[WINDOW:WINDOW + DEC_SEQ, :] = vn_ref[b]
        kext[WINDOW + DEC_SEQ:ext, :] = zero_tail
        vext[WINDOW + DEC_SEQ:ext, :] = zero_tail

        q = _rope(q_ref[b], qc_ref[...], qlo_ref[...], qhi_ref[...]) * (SWA_HEAD_DIM ** -0.5)
        s = jnp.where(valid, _dot_nt(q, kext[...]), NEG_INF)
        m = jnp.maximum(jnp.max(s, axis=-1, keepdims=True), sink)
        p = jnp.exp(s - m)
        denom = jnp.sum(p, axis=-1, keepdims=True) + jnp.exp(sink - m)
        o_all = _dot(p, vext[...]) / denom
        per_head = DEC_SEQ * SWA_GROUP
        o = jnp.zeros((per_head, SWA_KV), F32)
        for h in range(SWA_KV_HEADS):
            o = o + jnp.where(lane_head == h, o_all[h * per_head:(h + 1) * per_head, :], 0.0)
        o_ref[b] = o.astype(o_ref.dtype)
        ko_ref[b] = kext[DEC_SEQ:DEC_SEQ + WINDOW, :]
        vo_ref[b] = vext[DEC_SEQ:DEC_SEQ + WINDOW, :]


def _swa_sample(qkv_s, cache_k, cache_v, sinks, pos):
    q = qkv_s[:, :SWA_Q].reshape(DEC_BATCH, DEC_SEQ, SWA_KV_HEADS, SWA_GROUP, SWA_HEAD_DIM)
    q = jnp.transpose(q, (0, 2, 1, 3, 4))
    eye = jnp.eye(SWA_KV_HEADS, dtype=F32)
    rows = SWA_KV_HEADS * DEC_SEQ * SWA_GROUP
    q_bd = (q[:, :, :, :, None, :] * eye[None, :, None, None, :, None]).reshape(DEC_BATCH, rows, SWA_KV)
    k_new = qkv_s[:, SWA_Q:SWA_Q + SWA_KV].reshape(DEC_BATCH, DEC_SEQ, SWA_KV)
    v_new = qkv_s[:, SWA_Q + SWA_KV:].reshape(DEC_BATCH, DEC_SEQ, SWA_KV)
    row_t = (jnp.arange(rows) // SWA_GROUP) % DEC_SEQ
    q_tabs = _rope_tables(pos[row_t], LANES)
    k_tabs = _rope_tables(pos, LANES)
    row_head = (jnp.arange(rows) // (DEC_SEQ * SWA_GROUP)) * SWA_GROUP + jnp.arange(rows) % SWA_GROUP
    sink_rows = jnp.broadcast_to(sinks[row_head][:, None], (rows, LANES))

    nblk = SWA_SAMPLE_BLOCK
    blk = lambda r, c: pl.BlockSpec((nblk, r, c), lambda i: (i, 0, 0))
    per_head = DEC_SEQ * SWA_GROUP
    o, ko, vo = pl.pallas_call(
        _swa_sample_kernel,
        grid=(DEC_BATCH // nblk,),
        in_specs=[blk(rows, SWA_KV), blk(DEC_SEQ, SWA_KV), blk(DEC_SEQ, SWA_KV),
                  blk(WINDOW, SWA_KV), blk(WINDOW, SWA_KV), _const_spec((rows, LANES)),
                  _const_spec((rows, LANES)), _const_spec((rows, LANES)), _const_spec((rows, LANES)),
                  _const_spec((DEC_SEQ, LANES)), _const_spec((DEC_SEQ, LANES)), _const_spec((DEC_SEQ, LANES))],
        out_specs=[blk(per_head, SWA_KV), blk(WINDOW, SWA_KV), blk(WINDOW, SWA_KV)],
        out_shape=[jax.ShapeDtypeStruct((DEC_BATCH, per_head, SWA_KV), BF16),
                   jax.ShapeDtypeStruct((DEC_BATCH, WINDOW, SWA_KV), F32),
                   jax.ShapeDtypeStruct((DEC_BATCH, WINDOW, SWA_KV), F32)],
        scratch_shapes=[pltpu.VMEM((WINDOW + PAD_CHUNK, SWA_KV), F32), pltpu.VMEM((WINDOW + PAD_CHUNK, SWA_KV), F32)],
        compiler_params=pltpu.CompilerParams(dimension_semantics=("arbitrary",)),
        name="swa_sample",
    )(q_bd, k_new, v_new, cache_k, cache_v, sink_rows, *q_tabs, *k_tabs)
    o = o.reshape(DEC_BATCH, DEC_SEQ, SWA_GROUP, SWA_KV_HEADS, SWA_HEAD_DIM)
    o = jnp.transpose(o, (0, 1, 3, 2, 4)).reshape(N_SAMPLE, SWA_Q)
    return o, ko, vo


def _mixer_out_dtype(chunk):
    return BF16 if chunk % (2 * SUBLANES) == 0 else F32


def _tri(n, strict=False):
    i = lax.broadcasted_iota(jnp.int32, (n, n), 0)
    j = lax.broadcasted_iota(jnp.int32, (n, n), 1)
    return (j < i) if strict else (j <= i)


def _gdn_kernel(*refs, chunk, n_valid, has_state):
    it = iter(refs)
    x_ref, z_ref, b_ref, a_ref = next(it), next(it), next(it), next(it)
    cw_ref, alog_ref, dtb_ref, ng_ref = next(it), next(it), next(it), next(it)
    if has_state:
        s0_ref, c0_ref = next(it), next(it)
    o_ref, s_ref, tail_ref = next(it), next(it), next(it)
    xbuf = next(it)
    c = pl.program_id(1)

    @pl.when(c == 0)
    def _():
        if has_state:
            s_ref[...] = s0_ref[...]
            xbuf[0:SUBLANES - GDN_CONV + 1, :] = jnp.zeros((SUBLANES - GDN_CONV + 1, GDN_CONV_DIM), F32)
            xbuf[SUBLANES - GDN_CONV + 1:SUBLANES, :] = c0_ref[0]
        else:
            s_ref[...] = jnp.zeros_like(s_ref)
            xbuf[0:SUBLANES, :] = jnp.zeros((SUBLANES, GDN_CONV_DIM), F32)

    x = x_ref[...]
    xbuf[SUBLANES:SUBLANES + chunk, :] = x
    acc = x * cw_ref[GDN_CONV - 1:GDN_CONV, :]
    for tap in range(GDN_CONV - 1):
        start = SUBLANES - (GDN_CONV - 1) + tap
        acc = acc + xbuf[start:start + chunk, :] * cw_ref[tap:tap + 1, :]
    xbuf[0:SUBLANES, :] = x[chunk - SUBLANES:chunk, :]
    tail_ref[0] = x[chunk - SUBLANES:chunk, :]
    xc = _silu(acc)

    row = lax.broadcasted_iota(jnp.int32, (chunk, LANES), 0)
    live = row < n_valid
    beta = jnp.where(live, _sigmoid(b_ref[...]), 0.0)
    a = a_ref[...] + dtb_ref[...]
    softplus = jnp.maximum(a, 0.0) + jnp.log(1.0 + jnp.exp(-jnp.abs(a)))
    g = jnp.where(live, -jnp.exp(alog_ref[...]) * softplus, 0.0)

    incl = _tri(chunk)
    strict = _tri(chunk, strict=True)
    incl_f = incl.astype(F32)
    gcum = _dot_f32(incl_f, g)
    upper_f = (lax.broadcasted_iota(jnp.int32, (chunk, chunk), 0)
               <= lax.broadcasted_iota(jnp.int32, (chunk, chunk), 1)).astype(F32)
    gcum_t = lax.dot_general(g, upper_f, (((0,), (0,)), ((), ())), preferred_element_type=F32,
                             precision=lax.Precision.HIGHEST)
    eye = (lax.broadcasted_iota(jnp.int32, (chunk, chunk), 0)
           == lax.broadcasted_iota(jnp.int32, (chunk, chunk), 1)).astype(F32)
    n_pow = int(math.log2(chunk))
    ng = ng_ref[...]
    rep = GDN_V_HEADS // GDN_QK_HEADS
    heads = range(GDN_V_HEADS)

    qn, kn, kk, qk, ks, qs = [], [], [], [], [], []
    for hq in range(GDN_QK_HEADS):
        qh = xc[:, hq * GDN_HEAD_DIM:(hq + 1) * GDN_HEAD_DIM]
        kh = xc[:, GDN_KEY_DIM + hq * GDN_HEAD_DIM:GDN_KEY_DIM + (hq + 1) * GDN_HEAD_DIM]
        qn.append(qh * lax.rsqrt(jnp.sum(qh * qh, axis=-1, keepdims=True) + NORM_EPS) * (GDN_HEAD_DIM ** -0.5))
        kn.append(kh * lax.rsqrt(jnp.sum(kh * kh, axis=-1, keepdims=True) + NORM_EPS))
        kq = jnp.concatenate([kn[hq], qn[hq]], axis=0)
        scores = _dot_nt(kq, kn[hq])
        kk.append(scores[:chunk])
        qk.append(scores[chunk:])
        s_pair = jnp.concatenate([s_ref[0, hq * rep + r] for r in range(rep)], axis=1)
        against_state = _dot(kq, s_pair)
        for r in range(rep):
            ks.append(against_state[:chunk, r * GDN_HEAD_DIM:(r + 1) * GDN_HEAD_DIM])
            qs.append(against_state[chunk:, r * GDN_HEAD_DIM:(r + 1) * GDN_HEAD_DIM])

    gc = [gcum[:, h:h + 1] for h in heads]
    bc = [beta[:, h:h + 1] for h in heads]
    decay = [jnp.exp(jnp.where(incl, gc[h] - gcum_t[h:h + 1, :], NEG_INF)) for h in heads]
    q_pow = [-(jnp.where(strict, kk[h // rep] * decay[h], 0.0) * bc[h]) for h in heads]
    t_mat = [eye + q_pow[h] for h in heads]
    q_pow = [_dot(q_pow[h], q_pow[h]) for h in heads]
    for _ in range(n_pow - 2):
        both = [_dot(jnp.concatenate([t_mat[h], q_pow[h]], axis=0), q_pow[h]) for h in heads]
        t_mat = [t_mat[h] + both[h][:chunk] for h in heads]
        q_pow = [both[h][chunk:] for h in heads]
    t_mat = [t_mat[h] + _dot(t_mat[h], q_pow[h]) for h in heads]

    egc = [jnp.exp(gc[h]) for h in heads]
    v_new = []
    for h in heads:
        vh = xc[:, 2 * GDN_KEY_DIM + h * GDN_HEAD_DIM:2 * GDN_KEY_DIM + (h + 1) * GDN_HEAD_DIM]
        v_new.append(_dot(t_mat[h], bc[h] * (vh - egc[h] * ks[h])))
    for h in heads:
        o = egc[h] * qs[h] + _dot(qk[h // rep] * decay[h], v_new[h])
        g_last = gcum[chunk - 1:chunk, h:h + 1]
        s_ref[0, h] = (s_ref[0, h] * jnp.exp(g_last)
                       + _dot_tn(kn[h // rep] * jnp.exp(g_last - gc[h]), v_new[h]))
        zh = z_ref[:, h * GDN_HEAD_DIM:(h + 1) * GDN_HEAD_DIM]
        o_ref[:, h * GDN_HEAD_DIM:(h + 1) * GDN_HEAD_DIM] = (_rms(o, ng) * _silu(zh)).astype(o_ref.dtype)


def _gdn(x, z, b, a, conv_w, a_log, dt_bias, norm_g, n_seq, n_chunks, chunk, n_valid, state=None, conv0=None):
    has_state = state is not None
    rows = n_seq * n_chunks * chunk
    blk = lambda w: pl.BlockSpec((chunk, w), lambda s, c: (s * n_chunks + c, 0))
    pad_heads = lambda v: jnp.pad(v.astype(F32), (0, LANES - GDN_V_HEADS)).reshape(1, LANES)
    state_spec = pl.BlockSpec((1, GDN_V_HEADS, GDN_HEAD_DIM, GDN_HEAD_DIM), lambda s, c: (s, 0, 0, 0))
    in_specs = [blk(GDN_CONV_DIM), blk(GDN_V_DIM), blk(LANES), blk(LANES),
                _const_spec((GDN_CONV, GDN_CONV_DIM)), _const_spec((1, LANES)), _const_spec((1, LANES)),
                _const_spec((1, GDN_HEAD_DIM))]
    args = [x, z, b, a, conv_w, pad_heads(a_log), pad_heads(dt_bias), norm_g.reshape(1, GDN_HEAD_DIM)]
    if has_state:
        in_specs += [state_spec, pl.BlockSpec((1, GDN_CONV - 1, GDN_CONV_DIM), lambda s, c: (s, 0, 0))]
        args += [state, conv0]
    return pl.pallas_call(
        functools.partial(_gdn_kernel, chunk=chunk, n_valid=n_valid, has_state=has_state),
        grid=(n_seq, n_chunks),
        in_specs=in_specs,
        out_specs=[blk(GDN_V_DIM), state_spec,
                   pl.BlockSpec((1, SUBLANES, GDN_CONV_DIM), lambda s, c: (s, 0, 0))],
        out_shape=[jax.ShapeDtypeStruct((rows, GDN_V_DIM), _mixer_out_dtype(chunk)),
                   jax.ShapeDtypeStruct((n_seq, GDN_V_HEADS, GDN_HEAD_DIM, GDN_HEAD_DIM), F32),
                   jax.ShapeDtypeStruct((n_seq, SUBLANES, GDN_CONV_DIM), F32)],
        scratch_shapes=[pltpu.VMEM((SUBLANES + chunk, GDN_CONV_DIM), F32)],
        compiler_params=pltpu.CompilerParams(dimension_semantics=("arbitrary", "arbitrary"),
                                             vmem_limit_bytes=VMEM_LIMIT),
        name="gdn",
    )(*args)


def _hgrn_kernel(*refs, chunk, n_valid, has_state, layer):
    it = iter(refs)
    x_ref, lbl_ref, ng_ref = next(it), next(it), next(it)
    if has_state:
        s0_ref = next(it)
    o_ref, s_ref = next(it), next(it)
    st = next(it)
    c = pl.program_id(1)

    @pl.when(c == 0)
    def _():
        for h in range(HG_HEADS):
            st[h] = s0_ref[0, h].T if has_state else jnp.zeros((HG_HEAD_DIM, HG_HEAD_DIM), F32)

    logits = [lbl_ref[i:i + 1, :] for i in range(DEPTH)]
    mx = functools.reduce(jnp.maximum, logits)
    ex = [jnp.exp(l - mx) for l in logits]
    tot = functools.reduce(lambda u, v: u + v, ex)
    probs = [e / tot for e in ex]
    lb = functools.reduce(lambda u, v: u + v, probs[:layer + 1]) - probs[0]

    row = lax.broadcasted_iota(jnp.int32, (chunk, HG_DIM), 0)
    live = row < n_valid
    fz = x_ref[:, HG_DIM:2 * HG_DIM]
    log_f = jnp.where(live, jnp.log(lb + (1.0 - lb) * _sigmoid(fz)), 0.0)
    k_all = jnp.where(live, (1.0 - lb) * _sigmoid(-fz), 0.0)
    q_all = _silu(x_ref[:, 0:HG_DIM])
    incl_f = _tri(chunk).astype(F32)
    b2_all = _dot_f32(incl_f, log_f) * LOG2_E
    ones_l = jnp.ones((LANES, LANES), BF16)
    ng = ng_ref[...]
    n_blk = chunk // SUBLANES
    sub = lax.broadcasted_iota(jnp.int32, (SUBLANES, LANES), 0)

    for h in range(HG_HEADS):
        cols = slice(h * HG_HEAD_DIM, (h + 1) * HG_HEAD_DIM)
        q, k, b2 = q_all[:, cols], k_all[:, cols], b2_all[:, cols]
        v = x_ref[:, 2 * HG_DIM + h * HG_HEAD_DIM:2 * HG_DIM + (h + 1) * HG_HEAD_DIM]
        s_t = st[h]
        o_inter = _dot_nt(q * jnp.exp2(b2), s_t)
        k_rows = [k[j:j + 1, :] for j in range(chunk)]
        b_rows = [b2[j:j + 1, :] for j in range(chunk)]
        tiles, meta = [], []
        for bi in range(n_blk):
            qi = q[bi * SUBLANES:(bi + 1) * SUBLANES, :]
            bi_rows = b2[bi * SUBLANES:(bi + 1) * SUBLANES, :]
            for j in range((bi + 1) * SUBLANES):
                e = bi_rows - b_rows[j]
                if j >= bi * SUBLANES:
                    e = jnp.where(sub >= j - bi * SUBLANES, e, NEG_INF)
                tiles.append(qi * k_rows[j] * jnp.exp2(e))
                meta.append((bi, j))
        w_sum = jnp.dot(jnp.concatenate(tiles, axis=0).astype(BF16), ones_l, preferred_element_type=F32)
        o_blocks = [None] * n_blk
        for idx, (bi, j) in enumerate(meta):
            term = w_sum[idx * SUBLANES:(idx + 1) * SUBLANES, :] * v[j:j + 1, :]
            o_blocks[bi] = term if o_blocks[bi] is None else o_blocks[bi] + term
        o_intra = o_blocks[0] if n_blk == 1 else jnp.concatenate(o_blocks, axis=0)
        o = o_inter + o_intra
        b_last = b_rows[chunk - 1]
        st[h] = s_t * jnp.exp2(b_last) + _dot_tn(v, k * jnp.exp2(b_last - b2))
        gate = x_ref[:, 3 * HG_DIM + h * HG_HEAD_DIM:3 * HG_DIM + (h + 1) * HG_HEAD_DIM]
        o_ref[:, cols] = (_rms(o, ng) * _silu(gate)).astype(o_ref.dtype)

    @pl.when(c == pl.num_programs(1) - 1)
    def _():
        for h in range(HG_HEADS):
            s_ref[0, h] = st[h].T


def _hgrn(x, lb_logits, norm_g, layer, n_seq, n_chunks, chunk, n_valid, state=None):
    has_state = state is not None
    rows = n_seq * n_chunks * chunk
    state_spec = pl.BlockSpec((1, HG_HEADS, HG_HEAD_DIM, HG_HEAD_DIM), lambda s, c: (s, 0, 0, 0))
    in_specs = [pl.BlockSpec((chunk, 4 * HG_DIM), lambda s, c: (s * n_chunks + c, 0)),
                _const_spec((DEPTH, HG_DIM)), _const_spec((1, HG_HEAD_DIM))]
    args = [x, lb_logits, norm_g.reshape(1, HG_HEAD_DIM)]
    if has_state:
        in_specs.append(state_spec)
        args.append(state)
    return pl.pallas_call(
        functools.partial(_hgrn_kernel, chunk=chunk, n_valid=n_valid, has_state=has_state, layer=layer),
        grid=(n_seq, n_chunks),
        in_specs=in_specs,
        out_specs=[pl.BlockSpec((chunk, HG_DIM), lambda s, c: (s * n_chunks + c, 0)), state_spec],
        out_shape=[jax.ShapeDtypeStruct((rows, HG_DIM), _mixer_out_dtype(chunk)),
                   jax.ShapeDtypeStruct((n_seq, HG_HEADS, HG_HEAD_DIM, HG_HEAD_DIM), F32)],
        scratch_shapes=[pltpu.VMEM((HG_HEADS, HG_HEAD_DIM, HG_HEAD_DIM), F32)],
        compiler_params=pltpu.CompilerParams(dimension_semantics=("arbitrary", "arbitrary"),
                                             vmem_limit_bytes=VMEM_LIMIT),
        name="hgrn",
    )(*args)


def _pad_sample(t):
    w = t.shape[1]
    t = jnp.pad(t.reshape(DEC_BATCH, DEC_SEQ, w), ((0, 0), (0, PAD_CHUNK - DEC_SEQ), (0, 0)))
    return t.reshape(DEC_BATCH * PAD_CHUNK, w)


def _unpad_sample(t):
    w = t.shape[1]
    return t.reshape(DEC_BATCH, PAD_CHUNK, w)[:, :DEC_SEQ].reshape(N_SAMPLE, w).astype(BF16)


def _pad_cols(w, n):
    return jnp.pad(w, ((0, 0), (0, n - w.shape[1])))


def kernel(x_prompt, x_sample, cache_swa_k, cache_swa_v, state_gdn, state_gdn_conv, state_hgrn, norm_ffn, ffn_w_in, ffn_w_out, norm_mix, swa_w_in, swa_w_out, swa_sinks, gdn_w_in, gdn_conv_w, gdn_a_log, gdn_dt_bias, gdn_norm, gdn_w_out, hgrn_w_in, hgrn_lb_logits, hgrn_norm, hgrn_w_out, final_norm):
    x = jnp.concatenate([x_prompt.reshape(N_PROMPT, D_MODEL), x_sample.reshape(N_SAMPLE, D_MODEL)], axis=0)
    ffn_in = ffn_w_in.astype(BF16)
    ffn_out = ffn_w_out.astype(BF16)
    prompt_tables = _rope_tables(jnp.arange(SEQ), LANES)
    sample_pos = PAST_LEN + jnp.arange(DEC_SEQ)

    outs = {k: [] for k in ("pk", "pv", "pg", "pc", "ph", "sk", "sv", "sg", "sc", "sh")}
    proj = None
    for i in range(DEPTH):
        kind, j = i % N_MIXERS, i // N_MIXERS
        if proj is not None:
            x = _ffn(x, norm_ffn[i - 1, 1], ffn_in[i - 1, 1], ffn_out[i - 1, 1], proj=proj)
        x = _ffn(x, norm_ffn[i, 0], ffn_in[i, 0], ffn_out[i, 0])
        if kind == 0:
            (qkv,) = _norm_proj(x, norm_mix[i], [swa_w_in[j].astype(BF16)], TOKEN_TILE)
            o_p, pk, pv = _swa_prompt(qkv, swa_sinks[j], prompt_tables)
            o_s, sk, sv = _swa_sample(qkv[N_PROMPT:], cache_swa_k[j].reshape(DEC_BATCH, WINDOW, SWA_KV),
                                      cache_swa_v[j].reshape(DEC_BATCH, WINDOW, SWA_KV), swa_sinks[j], sample_pos)
            kv_shape = (WINDOW, SWA_KV_HEADS, SWA_HEAD_DIM)
            outs["pk"].append(pk.reshape((BATCH,) + kv_shape))
            outs["pv"].append(pv.reshape((BATCH,) + kv_shape))
            outs["sk"].append(sk.reshape((DEC_BATCH,) + kv_shape))
            outs["sv"].append(sv.reshape((DEC_BATCH,) + kv_shape))
            proj = (o_p, o_s, swa_w_out[j].astype(BF16))
        elif kind == 1:
            w = gdn_w_in[j].astype(BF16)
            z0 = GDN_CONV_DIM + GDN_V_DIM
            weights = [w[:, :GDN_CONV_DIM], w[:, GDN_CONV_DIM:z0],
                       _pad_cols(w[:, z0:z0 + GDN_V_HEADS], LANES), _pad_cols(w[:, z0 + GDN_V_HEADS:], LANES)]
            qkv, z, b, a = _norm_proj(x, norm_mix[i], weights, TOKEN_TILE // 2)
            o_p, pg, p_tail = _gdn(qkv, z, b, a, gdn_conv_w[j], gdn_a_log[j], gdn_dt_bias[j], gdn_norm[j],
                                   BATCH, SEQ // GDN_CHUNK, GDN_CHUNK, GDN_CHUNK)
            o_s, sg, s_tail = _gdn(*[_pad_sample(t[N_PROMPT:]) for t in (qkv, z, b, a)],
                                   gdn_conv_w[j], gdn_a_log[j], gdn_dt_bias[j], gdn_norm[j],
                                   DEC_BATCH, 1, PAD_CHUNK, DEC_SEQ, state=state_gdn[j], conv0=state_gdn_conv[j])
            outs["pg"].append(pg)
            outs["sg"].append(sg)
            keep = GDN_CONV - 1
            outs["pc"].append(p_tail[:, SUBLANES - keep:])
            outs["sc"].append(s_tail[:, DEC_SEQ - keep:DEC_SEQ])
            proj = (o_p, _unpad_sample(o_s), gdn_w_out[j].astype(BF16))
        else:
            (xin,) = _norm_proj(x, norm_mix[i], [hgrn_w_in[j].astype(BF16)], TOKEN_TILE)
            o_p, ph = _hgrn(xin, hgrn_lb_logits, hgrn_norm[j], i, BATCH, SEQ // HG_CHUNK, HG_CHUNK, HG_CHUNK)
            o_s, sh = _hgrn(_pad_sample(xin[N_PROMPT:]), hgrn_lb_logits, hgrn_norm[j], i,
                            DEC_BATCH, 1, PAD_CHUNK, DEC_SEQ, state=state_hgrn[j])
            outs["ph"].append(ph)
            outs["sh"].append(sh)
            proj = (o_p, _unpad_sample(o_s), hgrn_w_out[j].astype(BF16))
    x = _ffn(x, norm_ffn[DEPTH - 1, 1], ffn_in[DEPTH - 1, 1], ffn_out[DEPTH - 1, 1], proj=proj, final_g=final_norm)
    y_prompt = x[:N_PROMPT].reshape(BATCH, SEQ, D_MODEL)
    y_sample = x[N_PROMPT:].reshape(DEC_BATCH, DEC_SEQ, D_MODEL)
    st = lambda k: jnp.stack(outs[k])
    return (y_prompt, y_sample, st("pk"), st("pv"), st("pg"), st("pc"), st("ph"),
            st("sk"), st("sv"), st("sg"), st("sc"), st("sh"))
```

```python
import functools
import math

import jax
import jax.numpy as jnp
from jax import lax
from jax.experimental import pallas as pl
from jax.experimental.pallas import tpu as pltpu

D_MODEL = 1024
BATCH = 8
SEQ = 2048
DEPTH = 4
DEC_BATCH = 128
DEC_SEQ = 4
PAST_LEN = 8192
N_MIXERS = 3

SWA_HEAD_DIM = 64
SWA_HEADS = 16
SWA_KV_HEADS = 4
SWA_GROUP = 4
WINDOW = 128
ROT_DIM = 16
ROPE_THETA = 500000.0
SWA_Q = SWA_HEADS * SWA_HEAD_DIM
SWA_KV = SWA_KV_HEADS * SWA_HEAD_DIM

GDN_HEAD_DIM = 128
GDN_QK_HEADS = 8
GDN_V_HEADS = 16
GDN_KEY_DIM = 1024
GDN_V_DIM = 2048
GDN_CONV_DIM = 4096
GDN_CONV = 4

HG_HEAD_DIM = 128
HG_HEADS = 8
HG_DIM = 1024

D_FF = 2816
NORM_EPS = 1e-6
NEG_INF = -1e30
LOG2_E = 1.4426950408889634

LANES = 128
SUBLANES = 8
TOKEN_TILE = 512
N_PROMPT = BATCH * SEQ
N_SAMPLE = DEC_BATCH * DEC_SEQ
N_TOKENS = N_PROMPT + N_SAMPLE
N_PROMPT_TILES = N_PROMPT // TOKEN_TILE
FFN_CHUNK = 256
VMEM_LIMIT = 56 * 1024 * 1024
GDN_CHUNK = 64
HG_CHUNK = 32
PAD_CHUNK = 8
SWA_SAMPLE_BLOCK = 8

F32 = jnp.float32
BF16 = jnp.bfloat16


def _const_spec(shape):
    nd = len(shape)
    return pl.BlockSpec(shape, lambda *_: (0,) * nd, pipeline_mode=pl.Buffered(1))


def _rms(x, g):
    return x * lax.rsqrt(jnp.mean(x * x, axis=-1, keepdims=True) + NORM_EPS) * g


def _sigmoid(x):
    return 1.0 / (1.0 + jnp.exp(-x))


def _silu(x):
    return x * _sigmoid(x)


def _dot(a, b):
    return jnp.dot(a.astype(BF16), b.astype(BF16), preferred_element_type=F32)


def _dot_nt(a, b):
    return lax.dot_general(a.astype(BF16), b.astype(BF16), (((1,), (1,)), ((), ())),
                           preferred_element_type=F32)


def _dot_tn(a, b):
    return lax.dot_general(a.astype(BF16), b.astype(BF16), (((0,), (0,)), ((), ())),
                           preferred_element_type=F32)


def _dot_f32(a, b):
    return jnp.dot(a, b, preferred_element_type=F32, precision=lax.Precision.HIGHEST)


def _ffn_kernel(*refs, split_in, has_proj, final_norm):
    it = iter(refs)
    is_prompt = pl.program_id(0) < N_PROMPT_TILES
    if split_in:
        xp_ref, xs_ref = next(it), next(it)
        x = jnp.where(is_prompt, xp_ref[...], xs_ref[...])
    else:
        x = next(it)[...]
    if has_proj:
        op_ref, os_ref, wp_ref = next(it), next(it), next(it)
    g_ref, wg_ref, wu_ref, wo_ref = next(it), next(it), next(it), next(it)
    gf_ref = next(it) if final_norm else None
    out_refs = list(it)

    if has_proj:
        o = jnp.where(is_prompt, op_ref[...], os_ref[...])
        x = x + jnp.dot(o, wp_ref[...], preferred_element_type=F32)
    h = _rms(x, g_ref[...]).astype(BF16)
    y = jnp.zeros_like(x)
    for c in range(D_FF // FFN_CHUNK):
        cols = slice(c * FFN_CHUNK, (c + 1) * FFN_CHUNK)
        gate = jnp.dot(h, wg_ref[:, cols], preferred_element_type=F32)
        up = jnp.dot(h, wu_ref[:, cols], preferred_element_type=F32)
        act = (_silu(gate) * up).astype(BF16)
        y = y + jnp.dot(act, wo_ref[cols, :], preferred_element_type=F32)
    x = x + 0.5 * y
    if final_norm:
        x = _rms(x, gf_ref[...])
    if len(out_refs) == 1:
        out_refs[0][...] = x
    else:
        @pl.when(is_prompt)
        def _():
            out_refs[0][...] = x

        @pl.when(jnp.logical_not(is_prompt))
        def _():
            out_refs[1][...] = x


def _ffn(x, layer, which, norm_ffn, w_in, w_out, proj=None, final_g=None):
    tile = pl.BlockSpec((TOKEN_TILE, D_MODEL), lambda i: (i, 0))
    prompt_tile = lambda w: pl.BlockSpec((TOKEN_TILE, w), lambda i: (jnp.minimum(i, N_PROMPT_TILES - 1), 0))
    sample_tile = lambda w: pl.BlockSpec((TOKEN_TILE, w), lambda i: (0, 0))
    split_in = isinstance(x, tuple)
    in_specs = [prompt_tile(D_MODEL), sample_tile(D_MODEL)] if split_in else [tile]
    args = list(x) if split_in else [x]
    if proj is not None:
        o_p, o_s, w_p = proj
        k = w_p.shape[0]
        in_specs += [prompt_tile(k), sample_tile(k), _const_spec((k, D_MODEL))]
        args += [o_p, o_s, w_p]
    once = pl.Buffered(1)
    in_specs += [pl.BlockSpec((None, None, 1, D_MODEL), lambda i: (layer, which, 0, 0), pipeline_mode=once),
                 pl.BlockSpec((None, None, D_MODEL, D_FF), lambda i: (layer, which, 0, 0), pipeline_mode=once),
                 pl.BlockSpec((None, None, D_MODEL, D_FF), lambda i: (layer, which, 0, 1), pipeline_mode=once),
                 pl.BlockSpec((None, None, D_FF, D_MODEL), lambda i: (layer, which, 0, 0), pipeline_mode=once)]
    args += [norm_ffn, w_in, w_in, w_out]
    if final_g is not None:
        in_specs.append(_const_spec((1, D_MODEL)))
        args.append(final_g.reshape(1, D_MODEL))
        out_specs = [prompt_tile(D_MODEL), sample_tile(D_MODEL)]
        out_shape = [jax.ShapeDtypeStruct((N_PROMPT, D_MODEL), F32), jax.ShapeDtypeStruct((N_SAMPLE, D_MODEL), F32)]
    else:
        out_specs = tile
        out_shape = jax.ShapeDtypeStruct((N_TOKENS, D_MODEL), F32)
    return pl.pallas_call(
        functools.partial(_ffn_kernel, split_in=split_in, has_proj=proj is not None, final_norm=final_g is not None),
        grid=(N_TOKENS // TOKEN_TILE,),
        in_specs=in_specs,
        out_specs=out_specs,
        out_shape=out_shape,
        compiler_params=pltpu.CompilerParams(dimension_semantics=("arbitrary",), vmem_limit_bytes=VMEM_LIMIT),
        name="ffn",
    )(*args)


def _norm_proj_kernel(*refs, n_out):
    x_ref, g_ref = refs[0], refs[1]
    w_refs = refs[2:2 + n_out]
    out_refs = refs[2 + n_out:]
    h = _rms(x_ref[...], g_ref[...]).astype(BF16)
    for w_ref, out_ref in zip(w_refs, out_refs):
        out_ref[...] = jnp.dot(h, w_ref[...], preferred_element_type=F32)


def _norm_proj(x, g, weights, tile_rows):
    in_specs = [pl.BlockSpec((tile_rows, D_MODEL), lambda i: (i, 0)), _const_spec((1, D_MODEL))]
    in_specs += [_const_spec(w.shape) for w in weights]
    return pl.pallas_call(
        functools.partial(_norm_proj_kernel, n_out=len(weights)),
        grid=(N_TOKENS // tile_rows,),
        in_specs=in_specs,
        out_specs=[pl.BlockSpec((tile_rows, w.shape[1]), lambda i: (i, 0)) for w in weights],
        out_shape=[jax.ShapeDtypeStruct((N_TOKENS, w.shape[1]), F32) for w in weights],
        compiler_params=pltpu.CompilerParams(dimension_semantics=("arbitrary",), vmem_limit_bytes=VMEM_LIMIT),
        name="norm_proj",
    )(x, g.reshape(1, D_MODEL), *weights)


def _rope_tables(pos, width):
    half = ROT_DIM // 2
    inv_freq = ROPE_THETA ** (-jnp.arange(0, ROT_DIM, 2, dtype=F32) / ROT_DIM)
    ang = pos.astype(F32)[:, None] * inv_freq[None, :]
    cos, sin = jnp.cos(ang), jnp.sin(ang)
    n = pos.shape[0]
    rest = SWA_HEAD_DIM - ROT_DIM
    c = jnp.concatenate([cos, cos, jnp.ones((n, rest), F32)], axis=1)
    s_lo = jnp.concatenate([-sin, jnp.zeros((n, half + rest), F32)], axis=1)
    s_hi = jnp.concatenate([jnp.zeros((n, half), F32), sin, jnp.zeros((n, rest), F32)], axis=1)
    reps = width // SWA_HEAD_DIM
    return tuple(jnp.tile(t, (1, reps)) for t in (c, s_lo, s_hi))


def _rope(x, c, s_lo, s_hi):
    half = ROT_DIM // 2
    outs = []
    for j in range(x.shape[1] // LANES):
        xs = x[:, j * LANES:(j + 1) * LANES]
        outs.append(xs * c + pltpu.roll(xs, LANES - half, 1) * s_lo + pltpu.roll(xs, half, 1) * s_hi)
    return outs[0] if len(outs) == 1 else jnp.concatenate(outs, axis=1)


def _swa_prompt_kernel(sink_ref, q_ref, k_ref, v_ref, c_ref, slo_ref, shi_ref,
                       o_ref, kc_ref, vc_ref, kprev, vprev):
    n = pl.program_id(1)

    @pl.when(n == 0)
    def _():
        kprev[...] = jnp.zeros_like(kprev)
        vprev[...] = jnp.zeros_like(vprev)

    c, s_lo, s_hi = c_ref[...], slo_ref[...], shi_ref[...]
    q = _rope(q_ref[...], c, s_lo, s_hi) * (SWA_HEAD_DIM ** -0.5 * LOG2_E)
    k_cur = _rope(k_ref[...], c, s_lo, s_hi)
    v_cur = v_ref[...]
    keys = jnp.concatenate([kprev[...], k_cur], axis=0)
    vals = jnp.concatenate([vprev[...], v_cur], axis=0)

    qi = lax.broadcasted_iota(jnp.int32, (WINDOW, 2 * WINDOW), 0)
    kj = lax.broadcasted_iota(jnp.int32, (WINDOW, 2 * WINDOW), 1)
    low = jnp.maximum(qi, jnp.where(n > 0, 0, WINDOW - 1))
    bias_one = jnp.where(jnp.logical_and(kj > low, kj <= qi + WINDOW), 0.0, NEG_INF)
    bias = jnp.concatenate([bias_one] * SWA_GROUP, axis=0)
    ones = jnp.ones((2 * WINDOW, SWA_HEAD_DIM), BF16)

    stacked, scores, sinks = [], [], []
    for kv in range(SWA_KV_HEADS):
        heads = range(kv * SWA_GROUP, (kv + 1) * SWA_GROUP)
        kv_cols = slice(kv * SWA_HEAD_DIM, (kv + 1) * SWA_HEAD_DIM)
        qs = jnp.concatenate([q[:, h * SWA_HEAD_DIM:(h + 1) * SWA_HEAD_DIM] for h in heads], axis=0)
        scores.append(_dot_nt(qs, keys[:, kv_cols]) + bias)
        sinks.append(jnp.concatenate([jnp.full((WINDOW, 1), sink_ref[h] * LOG2_E, F32) for h in heads], axis=0))
    for kv in range(SWA_KV_HEADS):
        kv_cols = slice(kv * SWA_HEAD_DIM, (kv + 1) * SWA_HEAD_DIM)
        s, sink = scores[kv], sinks[kv]
        m = jnp.maximum(jnp.max(s, axis=-1, keepdims=True), sink)
        p = jnp.exp2(s - m).astype(BF16)
        denom = jnp.dot(p, ones, preferred_element_type=F32) + jnp.exp2(sink - m)
        stacked.append(_dot(p, vals[:, kv_cols]) / denom)
    for kv in range(SWA_KV_HEADS):
        for pair in range(SWA_GROUP // 2):
            g0 = 2 * pair
            both = jnp.concatenate([stacked[kv][g * WINDOW:(g + 1) * WINDOW] for g in (g0, g0 + 1)], axis=1)
            col = (kv * SWA_GROUP + g0) * SWA_HEAD_DIM
            o_ref[:, col:col + 2 * SWA_HEAD_DIM] = both.astype(o_ref.dtype)

    kprev[...] = k_cur
    vprev[...] = v_cur

    @pl.when(n == pl.num_programs(1) - 1)
    def _():
        kc_ref[0] = k_cur
        vc_ref[0] = v_cur


def _swa_prompt(qkv, sinks, tables):
    nb = SEQ // WINDOW
    q_blocks = SWA_Q // SWA_KV
    tab = pl.BlockSpec((WINDOW, LANES), lambda b, n, *_: (n, 0))
    cache = pl.BlockSpec((1, WINDOW, SWA_KV), lambda b, n, *_: (b, 0, 0))
    return pl.pallas_call(
        _swa_prompt_kernel,
        grid_spec=pltpu.PrefetchScalarGridSpec(
            num_scalar_prefetch=1,
            grid=(BATCH, nb),
            in_specs=[pl.BlockSpec((WINDOW, SWA_Q), lambda b, n, *_: (b * nb + n, 0)),
                      pl.BlockSpec((WINDOW, SWA_KV), lambda b, n, *_: (b * nb + n, q_blocks)),
                      pl.BlockSpec((WINDOW, SWA_KV), lambda b, n, *_: (b * nb + n, q_blocks + 1)),
                      tab, tab, tab],
            out_specs=[pl.BlockSpec((WINDOW, SWA_Q), lambda b, n, *_: (b * nb + n, 0)), cache, cache],
            scratch_shapes=[pltpu.VMEM((WINDOW, SWA_KV), F32), pltpu.VMEM((WINDOW, SWA_KV), F32)]),
        out_shape=[jax.ShapeDtypeStruct((N_PROMPT, SWA_Q), BF16),
                   jax.ShapeDtypeStruct((BATCH, WINDOW, SWA_KV), F32),
                   jax.ShapeDtypeStruct((BATCH, WINDOW, SWA_KV), F32)],
        compiler_params=pltpu.CompilerParams(dimension_semantics=("arbitrary", "arbitrary")),
        name="swa_prompt",
    )(sinks, qkv, qkv, qkv, *tables)


def _swa_sample_kernel(q_ref, kn_ref, vn_ref, kc_ref, vc_ref, sink_ref,
                       qc_ref, qlo_ref, qhi_ref, kc_tab, klo_tab, khi_tab,
                       o_ref, ko_ref, vo_ref, kext, vext):
    rows = SWA_KV_HEADS * DEC_SEQ * SWA_GROUP
    ext = WINDOW + PAD_CHUNK
    r = lax.broadcasted_iota(jnp.int32, (rows, ext), 0)
    j = lax.broadcasted_iota(jnp.int32, (rows, ext), 1)
    t = jnp.bitwise_and(jnp.right_shift(r, 2), DEC_SEQ - 1)
    valid = jnp.logical_and(j > t, j <= t + WINDOW)
    sink = sink_ref[:, 0:1]
    lane_head = jnp.right_shift(lax.broadcasted_iota(jnp.int32, (DEC_SEQ * SWA_GROUP, SWA_KV), 1), 6)
    zero_tail = jnp.zeros((PAD_CHUNK - DEC_SEQ, SWA_KV), F32)

    for b in range(SWA_SAMPLE_BLOCK):
        kext[0:WINDOW, :] = kc_ref[b]
        vext[0:WINDOW, :] = vc_ref[b]
        kext[WINDOW:WINDOW + DEC_SEQ, :] = _rope(kn_ref[b], kc_tab[...], klo_tab[...], khi_tab[...])
        vext[WINDOW:WINDOW + DEC_SEQ, :] = vn_ref[b]
        kext[WINDOW + DEC_SEQ:ext, :] = zero_tail
        vext[WINDOW + DEC_SEQ:ext, :] = zero_tail

        q = _rope(q_ref[b], qc_ref[...], qlo_ref[...], qhi_ref[...]) * (SWA_HEAD_DIM ** -0.5)
        s = jnp.where(valid, _dot_nt(q, kext[...]), NEG_INF)
        m = jnp.maximum(jnp.max(s, axis=-1, keepdims=True), sink)
        p = jnp.exp(s - m)
        denom = jnp.sum(p, axis=-1, keepdims=True) + jnp.exp(sink - m)
        o_all = _dot(p, vext[...]) / denom
        per_head = DEC_SEQ * SWA_GROUP
        o = jnp.zeros((per_head, SWA_KV), F32)
        for h in range(SWA_KV_HEADS):
            o = o + jnp.where(lane_head == h, o_all[h * per_head:(h + 1) * per_head, :], 0.0)
        o_ref[b] = o.astype(o_ref.dtype)
        ko_ref[b] = kext[DEC_SEQ:DEC_SEQ + WINDOW, :]
        vo_ref[b] = vext[DEC_SEQ:DEC_SEQ + WINDOW, :]


def _swa_sample(qkv_s, cache_k, cache_v, sinks, pos):
    q = qkv_s[:, :SWA_Q].reshape(DEC_BATCH, DEC_SEQ, SWA_KV_HEADS, SWA_GROUP, SWA_HEAD_DIM)
    q = jnp.transpose(q, (0, 2, 1, 3, 4))
    eye = jnp.eye(SWA_KV_HEADS, dtype=F32)
    rows = SWA_KV_HEADS * DEC_SEQ * SWA_GROUP
    q_bd = (q[:, :, :, :, None, :] * eye[None, :, None, None, :, None]).reshape(DEC_BATCH, rows, SWA_KV)
    k_new = qkv_s[:, SWA_Q:SWA_Q + SWA_KV].reshape(DEC_BATCH, DEC_SEQ, SWA_KV)
    v_new = qkv_s[:, SWA_Q + SWA_KV:].reshape(DEC_BATCH, DEC_SEQ, SWA_KV)
    row_t = (jnp.arange(rows) // SWA_GROUP) % DEC_SEQ
    q_tabs = _rope_tables(pos[row_t], LANES)
    k_tabs = _rope_tables(pos, LANES)
    row_head = (jnp.arange(rows) // (DEC_SEQ * SWA_GROUP)) * SWA_GROUP + jnp.arange(rows) % SWA_GROUP
    sink_rows = jnp.broadcast_to(sinks[row_head][:, None], (rows, LANES))

    nblk = SWA_SAMPLE_BLOCK
    blk = lambda r, c: pl.BlockSpec((nblk, r, c), lambda i: (i, 0, 0))
    per_head = DEC_SEQ * SWA_GROUP
    o, ko, vo = pl.pallas_call(
        _swa_sample_kernel,
        grid=(DEC_BATCH // nblk,),
        in_specs=[blk(rows, SWA_KV), blk(DEC_SEQ, SWA_KV), blk(DEC_SEQ, SWA_KV),
                  blk(WINDOW, SWA_KV), blk(WINDOW, SWA_KV), _const_spec((rows, LANES)),
                  _const_spec((rows, LANES)), _const_spec((rows, LANES)), _const_spec((rows, LANES)),
                  _const_spec((DEC_SEQ, LANES)), _const_spec((DEC_SEQ, LANES)), _const_spec((DEC_SEQ, LANES))],
        out_specs=[blk(per_head, SWA_KV), blk(WINDOW, SWA_KV), blk(WINDOW, SWA_KV)],
        out_shape=[jax.ShapeDtypeStruct((DEC_BATCH, per_head, SWA_KV), BF16),
                   jax.ShapeDtypeStruct((DEC_BATCH, WINDOW, SWA_KV), F32),
                   jax.ShapeDtypeStruct((DEC_BATCH, WINDOW, SWA_KV), F32)],
        scratch_shapes=[pltpu.VMEM((WINDOW + PAD_CHUNK, SWA_KV), F32), pltpu.VMEM((WINDOW + PAD_CHUNK, SWA_KV), F32)],
        compiler_params=pltpu.CompilerParams(dimension_semantics=("arbitrary",)),
        name="swa_sample",
    )(q_bd, k_new, v_new, cache_k, cache_v, sink_rows, *q_tabs, *k_tabs)
    o = o.reshape(DEC_BATCH, DEC_SEQ, SWA_GROUP, SWA_KV_HEADS, SWA_HEAD_DIM)
    o = jnp.transpose(o, (0, 1, 3, 2, 4)).reshape(N_SAMPLE, SWA_Q)
    return o, ko, vo


def _mixer_out_dtype(chunk):
    return BF16 if chunk % (2 * SUBLANES) == 0 else F32


def _tri(n, strict=False):
    i = lax.broadcasted_iota(jnp.int32, (n, n), 0)
    j = lax.broadcasted_iota(jnp.int32, (n, n), 1)
    return (j < i) if strict else (j <= i)


def _gdn_kernel(*refs, chunk, n_valid, has_state):
    it = iter(refs)
    x_ref, z_ref, b_ref, a_ref = next(it), next(it), next(it), next(it)
    cw_ref, alog_ref, dtb_ref, ng_ref = next(it), next(it), next(it), next(it)
    if has_state:
        s0_ref, c0_ref = next(it), next(it)
    o_ref, s_ref, tail_ref = next(it), next(it), next(it)
    xbuf = next(it)
    c = pl.program_id(1)

    @pl.when(c == 0)
    def _():
        if has_state:
            s_ref[...] = s0_ref[...]
            xbuf[0:SUBLANES - GDN_CONV + 1, :] = jnp.zeros((SUBLANES - GDN_CONV + 1, GDN_CONV_DIM), F32)
            xbuf[SUBLANES - GDN_CONV + 1:SUBLANES, :] = c0_ref[0]
        else:
            s_ref[...] = jnp.zeros_like(s_ref)
            xbuf[0:SUBLANES, :] = jnp.zeros((SUBLANES, GDN_CONV_DIM), F32)

    x = x_ref[...]
    full = jnp.concatenate([xbuf[...], x], axis=0)
    acc = x * cw_ref[GDN_CONV - 1:GDN_CONV, :]
    for tap in range(GDN_CONV - 1):
        back = GDN_CONV - 1 - tap
        acc = acc + pltpu.roll(full, back, 0)[SUBLANES:, :] * cw_ref[tap:tap + 1, :]
    xbuf[...] = x[chunk - SUBLANES:chunk, :]
    tail_ref[0] = x[chunk - SUBLANES:chunk, :]
    xc = _silu(acc)

    row = lax.broadcasted_iota(jnp.int32, (chunk, LANES), 0)
    live = row < n_valid
    beta = jnp.where(live, _sigmoid(b_ref[...]), 0.0)
    a = a_ref[...] + dtb_ref[...]
    softplus = jnp.maximum(a, 0.0) + jnp.log(1.0 + jnp.exp(-jnp.abs(a)))
    g = jnp.where(live, -jnp.exp(alog_ref[...]) * softplus, 0.0)

    incl = _tri(chunk)
    strict = _tri(chunk, strict=True)
    incl_f = incl.astype(F32)
    gcum = _dot_f32(incl_f, g)
    upper_f = (lax.broadcasted_iota(jnp.int32, (chunk, chunk), 0)
               <= lax.broadcasted_iota(jnp.int32, (chunk, chunk), 1)).astype(F32)
    gcum_t = lax.dot_general(g, upper_f, (((0,), (0,)), ((), ())), preferred_element_type=F32,
                             precision=lax.Precision.HIGHEST)
    eye = (lax.broadcasted_iota(jnp.int32, (chunk, chunk), 0)
           == lax.broadcasted_iota(jnp.int32, (chunk, chunk), 1)).astype(F32)
    n_pow = int(math.log2(chunk))
    ng = ng_ref[...]
    rep = GDN_V_HEADS // GDN_QK_HEADS
    heads = range(GDN_V_HEADS)

    qn, kn, kk, qk, ks, qs = [], [], [], [], [], []
    for hq in range(GDN_QK_HEADS):
        qh = xc[:, hq * GDN_HEAD_DIM:(hq + 1) * GDN_HEAD_DIM]
        kh = xc[:, GDN_KEY_DIM + hq * GDN_HEAD_DIM:GDN_KEY_DIM + (hq + 1) * GDN_HEAD_DIM]
        qn.append(qh * lax.rsqrt(jnp.sum(qh * qh, axis=-1, keepdims=True) + NORM_EPS) * (GDN_HEAD_DIM ** -0.5))
        kn.append(kh * lax.rsqrt(jnp.sum(kh * kh, axis=-1, keepdims=True) + NORM_EPS))
        kq = jnp.concatenate([kn[hq], qn[hq]], axis=0)
        scores = _dot_nt(kq, kn[hq])
        kk.append(scores[:chunk])
        qk.append(scores[chunk:])
        s_pair = jnp.concatenate([s_ref[0, hq * rep + r] for r in range(rep)], axis=1)
        against_state = _dot(kq, s_pair)
        for r in range(rep):
            ks.append(against_state[:chunk, r * GDN_HEAD_DIM:(r + 1) * GDN_HEAD_DIM])
            qs.append(against_state[chunk:, r * GDN_HEAD_DIM:(r + 1) * GDN_HEAD_DIM])

    gc = [gcum[:, h:h + 1] for h in heads]
    bc = [beta[:, h:h + 1] for h in heads]
    decay = [jnp.exp(jnp.where(incl, gc[h] - gcum_t[h:h + 1, :], NEG_INF)) for h in heads]
    q_pow = [-(jnp.where(strict, kk[h // rep] * decay[h], 0.0) * bc[h]) for h in heads]
    t_mat = [eye + q_pow[h] for h in heads]
    q_pow = [_dot(q_pow[h], q_pow[h]) for h in heads]
    for _ in range(n_pow - 2):
        both = [_dot(jnp.concatenate([t_mat[h], q_pow[h]], axis=0), q_pow[h]) for h in heads]
        t_mat = [t_mat[h] + both[h][:chunk] for h in heads]
        q_pow = [both[h][chunk:] for h in heads]
    t_mat = [t_mat[h] + _dot(t_mat[h], q_pow[h]) for h in heads]

    egc = [jnp.exp(gc[h]) for h in heads]
    v_new = []
    for h in heads:
        vh = xc[:, 2 * GDN_KEY_DIM + h * GDN_HEAD_DIM:2 * GDN_KEY_DIM + (h + 1) * GDN_HEAD_DIM]
        v_new.append(_dot(t_mat[h], bc[h] * (vh - egc[h] * ks[h])))
    for h in heads:
        o = egc[h] * qs[h] + _dot(qk[h // rep] * decay[h], v_new[h])
        g_last = gcum[chunk - 1:chunk, h:h + 1]
        s_ref[0, h] = (s_ref[0, h] * jnp.exp(g_last)
                       + _dot_tn(kn[h // rep] * jnp.exp(g_last - gc[h]), v_new[h]))
        zh = z_ref[:, h * GDN_HEAD_DIM:(h + 1) * GDN_HEAD_DIM]
        o_ref[:, h * GDN_HEAD_DIM:(h + 1) * GDN_HEAD_DIM] = (_rms(o, ng) * _silu(zh)).astype(o_ref.dtype)


def _gdn(x, z, b, a, conv_w, a_log, dt_bias, norm_g, n_seq, n_chunks, chunk, n_valid, state=None, conv0=None):
    has_state = state is not None
    rows = n_seq * n_chunks * chunk
    blk = lambda w: pl.BlockSpec((chunk, w), lambda s, c: (s * n_chunks + c, 0))
    pad_heads = lambda v: jnp.pad(v.astype(F32), (0, LANES - GDN_V_HEADS)).reshape(1, LANES)
    state_spec = pl.BlockSpec((1, GDN_V_HEADS, GDN_HEAD_DIM, GDN_HEAD_DIM), lambda s, c: (s, 0, 0, 0))
    in_specs = [blk(GDN_CONV_DIM), blk(GDN_V_DIM), blk(LANES), blk(LANES),
                _const_spec((GDN_CONV, GDN_CONV_DIM)), _const_spec((1, LANES)), _const_spec((1, LANES)),
                _const_spec((1, GDN_HEAD_DIM))]
    args = [x, z, b, a, conv_w, pad_heads(a_log), pad_heads(dt_bias), norm_g.reshape(1, GDN_HEAD_DIM)]
    if has_state:
        in_specs += [state_spec, pl.BlockSpec((1, GDN_CONV - 1, GDN_CONV_DIM), lambda s, c: (s, 0, 0))]
        args += [state, conv0]
    return pl.pallas_call(
        functools.partial(_gdn_kernel, chunk=chunk, n_valid=n_valid, has_state=has_state),
        grid=(n_seq, n_chunks),
        in_specs=in_specs,
        out_specs=[blk(GDN_V_DIM), state_spec,
                   pl.BlockSpec((1, SUBLANES, GDN_CONV_DIM), lambda s, c: (s, 0, 0))],
        out_shape=[jax.ShapeDtypeStruct((rows, GDN_V_DIM), _mixer_out_dtype(chunk)),
                   jax.ShapeDtypeStruct((n_seq, GDN_V_HEADS, GDN_HEAD_DIM, GDN_HEAD_DIM), F32),
                   jax.ShapeDtypeStruct((n_seq, SUBLANES, GDN_CONV_DIM), F32)],
        scratch_shapes=[pltpu.VMEM((SUBLANES, GDN_CONV_DIM), F32)],
        compiler_params=pltpu.CompilerParams(dimension_semantics=("arbitrary", "arbitrary"),
                                             vmem_limit_bytes=VMEM_LIMIT),
        name="gdn",
    )(*args)


def _hgrn_kernel(*refs, chunk, n_valid, has_state, layer):
    it = iter(refs)
    x_ref, lbl_ref, ng_ref = next(it), next(it), next(it)
    if has_state:
        s0_ref = next(it)
    o_ref, s_ref = next(it), next(it)
    st = next(it)
    c = pl.program_id(1)

    @pl.when(c == 0)
    def _():
        for h in range(HG_HEADS):
            st[h] = s0_ref[0, h].T if has_state else jnp.zeros((HG_HEAD_DIM, HG_HEAD_DIM), F32)

    logits = [lbl_ref[i:i + 1, :] for i in range(DEPTH)]
    mx = functools.reduce(jnp.maximum, logits)
    ex = [jnp.exp(l - mx) for l in logits]
    tot = functools.reduce(lambda u, v: u + v, ex)
    probs = [e / tot for e in ex]
    lb = functools.reduce(lambda u, v: u + v, probs[:layer + 1]) - probs[0]

    row = lax.broadcasted_iota(jnp.int32, (chunk, HG_DIM), 0)
    live = row < n_valid
    fz = x_ref[:, HG_DIM:2 * HG_DIM]
    log_f = jnp.where(live, jnp.log(lb + (1.0 - lb) * _sigmoid(fz)), 0.0)
    k_all = jnp.where(live, (1.0 - lb) * _sigmoid(-fz), 0.0)
    q_all = _silu(x_ref[:, 0:HG_DIM])
    incl_f = _tri(chunk).astype(F32)
    b2_all = _dot_f32(incl_f, log_f) * LOG2_E
    ones_l = jnp.ones((LANES, LANES), BF16)
    ng = ng_ref[...]
    n_blk = chunk // SUBLANES
    sub = lax.broadcasted_iota(jnp.int32, (SUBLANES, LANES), 0)

    for h in range(HG_HEADS):
        cols = slice(h * HG_HEAD_DIM, (h + 1) * HG_HEAD_DIM)
        q, k, b2 = q_all[:, cols], k_all[:, cols], b2_all[:, cols]
        v = x_ref[:, 2 * HG_DIM + h * HG_HEAD_DIM:2 * HG_DIM + (h + 1) * HG_HEAD_DIM]
        s_t = st[h]
        o_inter = _dot_nt(q * jnp.exp2(b2), s_t)
        k_rows = [k[j:j + 1, :] for j in range(chunk)]
        b_rows = [b2[j:j + 1, :] for j in range(chunk)]
        tiles, meta = [], []
        for bi in range(n_blk):
            qi = q[bi * SUBLANES:(bi + 1) * SUBLANES, :]
            bi_rows = b2[bi * SUBLANES:(bi + 1) * SUBLANES, :]
            for j in range((bi + 1) * SUBLANES):
                e = bi_rows - b_rows[j]
                if j >= bi * SUBLANES:
                    e = jnp.where(sub >= j - bi * SUBLANES, e, NEG_INF)
                tiles.append(qi * k_rows[j] * jnp.exp2(e))
                meta.append((bi, j))
        w_sum = jnp.dot(jnp.concatenate(tiles, axis=0).astype(BF16), ones_l, preferred_element_type=F32)
        o_blocks = [None] * n_blk
        for idx, (bi, j) in enumerate(meta):
            term = w_sum[idx * SUBLANES:(idx + 1) * SUBLANES, :] * v[j:j + 1, :]
            o_blocks[bi] = term if o_blocks[bi] is None else o_blocks[bi] + term
        o_intra = o_blocks[0] if n_blk == 1 else jnp.concatenate(o_blocks, axis=0)
        o = o_inter + o_intra
        b_last = b_rows[chunk - 1]
        st[h] = s_t * jnp.exp2(b_last) + _dot_tn(v, k * jnp.exp2(b_last - b2))
        gate = x_ref[:, 3 * HG_DIM + h * HG_HEAD_DIM:3 * HG_DIM + (h + 1) * HG_HEAD_DIM]
        o_ref[:, cols] = (_rms(o, ng) * _silu(gate)).astype(o_ref.dtype)

    @pl.when(c == pl.num_programs(1) - 1)
    def _():
        for h in range(HG_HEADS):
            s_ref[0, h] = st[h].T


def _hgrn(x, lb_logits, norm_g, layer, n_seq, n_chunks, chunk, n_valid, state=None):
    has_state = state is not None
    rows = n_seq * n_chunks * chunk
    state_spec = pl.BlockSpec((1, HG_HEADS, HG_HEAD_DIM, HG_HEAD_DIM), lambda s, c: (s, 0, 0, 0))
    in_specs = [pl.BlockSpec((chunk, 4 * HG_DIM), lambda s, c: (s * n_chunks + c, 0)),
                _const_spec((DEPTH, HG_DIM)), _const_spec((1, HG_HEAD_DIM))]
    args = [x, lb_logits, norm_g.reshape(1, HG_HEAD_DIM)]
    if has_state:
        in_specs.append(state_spec)
        args.append(state)
    return pl.pallas_call(
        functools.partial(_hgrn_kernel, chunk=chunk, n_valid=n_valid, has_state=has_state, layer=layer),
        grid=(n_seq, n_chunks),
        in_specs=in_specs,
        out_specs=[pl.BlockSpec((chunk, HG_DIM), lambda s, c: (s * n_chunks + c, 0)), state_spec],
        out_shape=[jax.ShapeDtypeStruct((rows, HG_DIM), _mixer_out_dtype(chunk)),
                   jax.ShapeDtypeStruct((n_seq, HG_HEADS, HG_HEAD_DIM, HG_HEAD_DIM), F32)],
        scratch_shapes=[pltpu.VMEM((HG_HEADS, HG_HEAD_DIM, HG_HEAD_DIM), F32)],
        compiler_params=pltpu.CompilerParams(dimension_semantics=("arbitrary", "arbitrary"),
                                             vmem_limit_bytes=VMEM_LIMIT),
        name="hgrn",
    )(*args)


def _pad_sample(t):
    w = t.shape[1]
    t = jnp.pad(t.reshape(DEC_BATCH, DEC_SEQ, w), ((0, 0), (0, PAD_CHUNK - DEC_SEQ), (0, 0)))
    return t.reshape(DEC_BATCH * PAD_CHUNK, w)


def _unpad_sample(t):
    w = t.shape[1]
    return t.reshape(DEC_BATCH, PAD_CHUNK, w)[:, :DEC_SEQ].reshape(N_SAMPLE, w).astype(BF16)


def _pad_cols(w, n):
    return jnp.pad(w, ((0, 0), (0, n - w.shape[1])))


def kernel(x_prompt, x_sample, cache_swa_k, cache_swa_v, state_gdn, state_gdn_conv, state_hgrn, norm_ffn, ffn_w_in, ffn_w_out, norm_mix, swa_w_in, swa_w_out, swa_sinks, gdn_w_in, gdn_conv_w, gdn_a_log, gdn_dt_bias, gdn_norm, gdn_w_out, hgrn_w_in, hgrn_lb_logits, hgrn_norm, hgrn_w_out, final_norm):
    x = (x_prompt.reshape(N_PROMPT, D_MODEL), x_sample.reshape(N_SAMPLE, D_MODEL))
    ffn = functools.partial(_ffn, norm_ffn=norm_ffn.reshape(DEPTH, 2, 1, D_MODEL),
                            w_in=ffn_w_in.astype(BF16), w_out=ffn_w_out.astype(BF16))
    prompt_tables = _rope_tables(jnp.arange(SEQ), LANES)
    sample_pos = PAST_LEN + jnp.arange(DEC_SEQ)

    outs = {k: [] for k in ("pk", "pv", "pg", "pc", "ph", "sk", "sv", "sg", "sc", "sh")}
    proj = None
    for i in range(DEPTH):
        kind, j = i % N_MIXERS, i // N_MIXERS
        if proj is not None:
            x = ffn(x, i - 1, 1, proj=proj)
        x = ffn(x, i, 0)
        if kind == 0:
            (qkv,) = _norm_proj(x, norm_mix[i], [swa_w_in[j].astype(BF16)], TOKEN_TILE)
            o_p, pk, pv = _swa_prompt(qkv, swa_sinks[j], prompt_tables)
            o_s, sk, sv = _swa_sample(qkv[N_PROMPT:], cache_swa_k[j].reshape(DEC_BATCH, WINDOW, SWA_KV),
                                      cache_swa_v[j].reshape(DEC_BATCH, WINDOW, SWA_KV), swa_sinks[j], sample_pos)
            kv_shape = (WINDOW, SWA_KV_HEADS, SWA_HEAD_DIM)
            outs["pk"].append(pk.reshape((BATCH,) + kv_shape))
            outs["pv"].append(pv.reshape((BATCH,) + kv_shape))
            outs["sk"].append(sk.reshape((DEC_BATCH,) + kv_shape))
            outs["sv"].append(sv.reshape((DEC_BATCH,) + kv_shape))
            proj = (o_p, o_s, swa_w_out[j].astype(BF16))
        elif kind == 1:
            w = gdn_w_in[j].astype(BF16)
            z0 = GDN_CONV_DIM + GDN_V_DIM
            weights = [w[:, :GDN_CONV_DIM], w[:, GDN_CONV_DIM:z0],
                       _pad_cols(w[:, z0:z0 + GDN_V_HEADS], LANES), _pad_cols(w[:, z0 + GDN_V_HEADS:], LANES)]
            qkv, z, b, a = _norm_proj(x, norm_mix[i], weights, TOKEN_TILE // 2)
            o_p, pg, p_tail = _gdn(qkv, z, b, a, gdn_conv_w[j], gdn_a_log[j], gdn_dt_bias[j], gdn_norm[j],
                                   BATCH, SEQ // GDN_CHUNK, GDN_CHUNK, GDN_CHUNK)
            o_s, sg, s_tail = _gdn(*[_pad_sample(t[N_PROMPT:]) for t in (qkv, z, b, a)],
                                   gdn_conv_w[j], gdn_a_log[j], gdn_dt_bias[j], gdn_norm[j],
                                   DEC_BATCH, 1, PAD_CHUNK, DEC_SEQ, state=state_gdn[j], conv0=state_gdn_conv[j])
            outs["pg"].append(pg)
            outs["sg"].append(sg)
            keep = GDN_CONV - 1
            outs["pc"].append(p_tail[:, SUBLANES - keep:])
            outs["sc"].append(s_tail[:, DEC_SEQ - keep:DEC_SEQ])
            proj = (o_p, _unpad_sample(o_s), gdn_w_out[j].astype(BF16))
        else:
            (xin,) = _norm_proj(x, norm_mix[i], [hgrn_w_in[j].astype(BF16)], TOKEN_TILE)
            o_p, ph = _hgrn(xin, hgrn_lb_logits, hgrn_norm[j], i, BATCH, SEQ // HG_CHUNK, HG_CHUNK, HG_CHUNK)
            o_s, sh = _hgrn(_pad_sample(xin[N_PROMPT:]), hgrn_lb_logits, hgrn_norm[j], i,
                            DEC_BATCH, 1, PAD_CHUNK, DEC_SEQ, state=state_hgrn[j])
            outs["ph"].append(ph)
            outs["sh"].append(sh)
            proj = (o_p, _unpad_sample(o_s), hgrn_w_out[j].astype(BF16))
    y_prompt, y_sample = ffn(x, DEPTH - 1, 1, proj=proj, final_g=final_norm)
    y_prompt = y_prompt.reshape(BATCH, SEQ, D_MODEL)
    y_sample = y_sample.reshape(DEC_BATCH, DEC_SEQ, D_MODEL)
    st = lambda k: jnp.stack(outs[k])
    return (y_prompt, y_sample, st("pk"), st("pv"), st("pg"), st("pc"), st("ph"),
            st("sk"), st("sv"), st("sg"), st("sc"), st("sh"))
```

```python
import functools
import math

import jax
import jax.numpy as jnp
from jax import lax
from jax.experimental import pallas as pl
from jax.experimental.pallas import tpu as pltpu

D_MODEL = 1024
BATCH = 8
SEQ = 2048
DEPTH = 4
DEC_BATCH = 128
DEC_SEQ = 4
PAST_LEN = 8192
N_MIXERS = 3

SWA_HEAD_DIM = 64
SWA_HEADS = 16
SWA_KV_HEADS = 4
SWA_GROUP = 4
WINDOW = 128
ROT_DIM = 16
ROPE_THETA = 500000.0
SWA_Q = SWA_HEADS * SWA_HEAD_DIM
SWA_KV = SWA_KV_HEADS * SWA_HEAD_DIM

GDN_HEAD_DIM = 128
GDN_QK_HEADS = 8
GDN_V_HEADS = 16
GDN_KEY_DIM = 1024
GDN_V_DIM = 2048
GDN_CONV_DIM = 4096
GDN_CONV = 4

HG_HEAD_DIM = 128
HG_HEADS = 8
HG_DIM = 1024

D_FF = 2816
NORM_EPS = 1e-6
NEG_INF = -1e30
LOG2_E = 1.4426950408889634

LANES = 128
SUBLANES = 8
TOKEN_TILE = 512
N_PROMPT = BATCH * SEQ
N_SAMPLE = DEC_BATCH * DEC_SEQ
N_TOKENS = N_PROMPT + N_SAMPLE
N_PROMPT_TILES = N_PROMPT // TOKEN_TILE
FFN_CHUNK = 256
VMEM_LIMIT = 56 * 1024 * 1024
GDN_CHUNK = 64
HG_CHUNK = 32
PAD_CHUNK = 8
SWA_SAMPLE_BLOCK = 8

F32 = jnp.float32
BF16 = jnp.bfloat16


def _const_spec(shape):
    nd = len(shape)
    return pl.BlockSpec(shape, lambda *_: (0,) * nd, pipeline_mode=pl.Buffered(1))


def _rms(x, g):
    return x * lax.rsqrt(jnp.mean(x * x, axis=-1, keepdims=True) + NORM_EPS) * g


def _sigmoid(x):
    return 1.0 / (1.0 + jnp.exp(-x))


def _silu(x):
    return x * _sigmoid(x)


def _dot(a, b):
    return jnp.dot(a.astype(BF16), b.astype(BF16), preferred_element_type=F32)


def _dot_nt(a, b):
    return lax.dot_general(a.astype(BF16), b.astype(BF16), (((1,), (1,)), ((), ())),
                           preferred_element_type=F32)


def _dot_tn(a, b):
    return lax.dot_general(a.astype(BF16), b.astype(BF16), (((0,), (0,)), ((), ())),
                           preferred_element_type=F32)


def _dot_f32(a, b):
    return jnp.dot(a, b, preferred_element_type=F32, precision=lax.Precision.HIGHEST)


def _ffn_kernel(*refs, split_in, has_proj, final_norm):
    it = iter(refs)
    is_prompt = pl.program_id(0) < N_PROMPT_TILES
    if split_in:
        xp_ref, xs_ref = next(it), next(it)
        x = jnp.where(is_prompt, xp_ref[...], xs_ref[...])
    else:
        x = next(it)[...]
    if has_proj:
        op_ref, os_ref, wp_ref = next(it), next(it), next(it)
    g_ref, wg_ref, wu_ref, wo_ref = next(it), next(it), next(it), next(it)
    gf_ref = next(it) if final_norm else None
    out_refs = list(it)

    if has_proj:
        o = jnp.where(is_prompt, op_ref[...], os_ref[...])
        x = x + jnp.dot(o, wp_ref[...], preferred_element_type=F32)
    h = _rms(x, g_ref[...]).astype(BF16)
    y = jnp.zeros_like(x)
    for c in range(D_FF // FFN_CHUNK):
        cols = slice(c * FFN_CHUNK, (c + 1) * FFN_CHUNK)
        gate = jnp.dot(h, wg_ref[:, cols], preferred_element_type=F32)
        up = jnp.dot(h, wu_ref[:, cols], preferred_element_type=F32)
        act = (_silu(gate) * up).astype(BF16)
        y = y + jnp.dot(act, wo_ref[cols, :], preferred_element_type=F32)
    x = x + 0.5 * y
    if final_norm:
        x = _rms(x, gf_ref[...])
    if len(out_refs) == 1:
        out_refs[0][...] = x
    else:
        @pl.when(is_prompt)
        def _():
            out_refs[0][...] = x

        @pl.when(jnp.logical_not(is_prompt))
        def _():
            out_refs[1][...] = x


def _ffn(x, layer, which, norm_ffn, w_in, w_out, proj=None, final_g=None):
    tile = pl.BlockSpec((TOKEN_TILE, D_MODEL), lambda i: (i, 0))
    prompt_tile = lambda w: pl.BlockSpec((TOKEN_TILE, w), lambda i: (jnp.minimum(i, N_PROMPT_TILES - 1), 0))
    sample_tile = lambda w: pl.BlockSpec((TOKEN_TILE, w), lambda i: (0, 0))
    split_in = isinstance(x, tuple)
    in_specs = [prompt_tile(D_MODEL), sample_tile(D_MODEL)] if split_in else [tile]
    args = list(x) if split_in else [x]
    if proj is not None:
        o_p, o_s, w_p = proj
        k = w_p.shape[0]
        in_specs += [prompt_tile(k), sample_tile(k), _const_spec((k, D_MODEL))]
        args += [o_p, o_s, w_p]
    once = pl.Buffered(1)
    in_specs += [pl.BlockSpec((None, None, 1, D_MODEL), lambda i: (layer, which, 0, 0), pipeline_mode=once),
                 pl.BlockSpec((None, None, D_MODEL, D_FF), lambda i: (layer, which, 0, 0), pipeline_mode=once),
                 pl.BlockSpec((None, None, D_MODEL, D_FF), lambda i: (layer, which, 0, 1), pipeline_mode=once),
                 pl.BlockSpec((None, None, D_FF, D_MODEL), lambda i: (layer, which, 0, 0), pipeline_mode=once)]
    args += [norm_ffn, w_in, w_in, w_out]
    if final_g is not None:
        in_specs.append(_const_spec((1, D_MODEL)))
        args.append(final_g.reshape(1, D_MODEL))
        out_specs = [prompt_tile(D_MODEL), sample_tile(D_MODEL)]
        out_shape = [jax.ShapeDtypeStruct((N_PROMPT, D_MODEL), F32), jax.ShapeDtypeStruct((N_SAMPLE, D_MODEL), F32)]
    else:
        out_specs = tile
        out_shape = jax.ShapeDtypeStruct((N_TOKENS, D_MODEL), F32)
    return pl.pallas_call(
        functools.partial(_ffn_kernel, split_in=split_in, has_proj=proj is not None, final_norm=final_g is not None),
        grid=(N_TOKENS // TOKEN_TILE,),
        in_specs=in_specs,
        out_specs=out_specs,
        out_shape=out_shape,
        compiler_params=pltpu.CompilerParams(dimension_semantics=("arbitrary",), vmem_limit_bytes=VMEM_LIMIT),
        name="ffn",
    )(*args)


def _norm_proj_kernel(*refs, n_out):
    x_ref, g_ref = refs[0], refs[1]
    w_refs = refs[2:2 + n_out]
    out_refs = refs[2 + n_out:]
    h = _rms(x_ref[...], g_ref[...]).astype(BF16)
    for w_ref, out_ref in zip(w_refs, out_refs):
        out_ref[...] = jnp.dot(h, w_ref[...], preferred_element_type=F32)


def _norm_proj(x, g, weights, tile_rows):
    in_specs = [pl.BlockSpec((tile_rows, D_MODEL), lambda i: (i, 0)), _const_spec((1, D_MODEL))]
    in_specs += [_const_spec(w.shape) for w in weights]
    return pl.pallas_call(
        functools.partial(_norm_proj_kernel, n_out=len(weights)),
        grid=(N_TOKENS // tile_rows,),
        in_specs=in_specs,
        out_specs=[pl.BlockSpec((tile_rows, w.shape[1]), lambda i: (i, 0)) for w in weights],
        out_shape=[jax.ShapeDtypeStruct((N_TOKENS, w.shape[1]), F32) for w in weights],
        compiler_params=pltpu.CompilerParams(dimension_semantics=("arbitrary",), vmem_limit_bytes=VMEM_LIMIT),
        name="norm_proj",
    )(x, g.reshape(1, D_MODEL), *weights)


def _rope_tables(pos, width):
    half = ROT_DIM // 2
    inv_freq = ROPE_THETA ** (-jnp.arange(0, ROT_DIM, 2, dtype=F32) / ROT_DIM)
    ang = pos.astype(F32)[:, None] * inv_freq[None, :]
    cos, sin = jnp.cos(ang), jnp.sin(ang)
    n = pos.shape[0]
    rest = SWA_HEAD_DIM - ROT_DIM
    c = jnp.concatenate([cos, cos, jnp.ones((n, rest), F32)], axis=1)
    s_lo = jnp.concatenate([-sin, jnp.zeros((n, half + rest), F32)], axis=1)
    s_hi = jnp.concatenate([jnp.zeros((n, half), F32), sin, jnp.zeros((n, rest), F32)], axis=1)
    reps = width // SWA_HEAD_DIM
    return tuple(jnp.tile(t, (1, reps)) for t in (c, s_lo, s_hi))


def _rope(x, c, s_lo, s_hi):
    half = ROT_DIM // 2
    outs = []
    for j in range(x.shape[1] // LANES):
        xs = x[:, j * LANES:(j + 1) * LANES]
        outs.append(xs * c + pltpu.roll(xs, LANES - half, 1) * s_lo + pltpu.roll(xs, half, 1) * s_hi)
    return outs[0] if len(outs) == 1 else jnp.concatenate(outs, axis=1)


def _swa_prompt_kernel(sink_ref, q_ref, k_ref, v_ref, c_ref, slo_ref, shi_ref,
                       o_ref, kc_ref, vc_ref, kprev, vprev):
    n = pl.program_id(1)

    @pl.when(n == 0)
    def _():
        kprev[...] = jnp.zeros_like(kprev)
        vprev[...] = jnp.zeros_like(vprev)

    c, s_lo, s_hi = c_ref[...], slo_ref[...], shi_ref[...]
    q = _rope(q_ref[...], c, s_lo, s_hi) * (SWA_HEAD_DIM ** -0.5 * LOG2_E)
    k_cur = _rope(k_ref[...], c, s_lo, s_hi)
    v_cur = v_ref[...]
    keys = jnp.concatenate([kprev[...], k_cur], axis=0)
    vals = jnp.concatenate([vprev[...], v_cur], axis=0)

    qi = lax.broadcasted_iota(jnp.int32, (WINDOW, 2 * WINDOW), 0)
    kj = lax.broadcasted_iota(jnp.int32, (WINDOW, 2 * WINDOW), 1)
    low = jnp.maximum(qi, jnp.where(n > 0, 0, WINDOW - 1))
    bias_one = jnp.where(jnp.logical_and(kj > low, kj <= qi + WINDOW), 0.0, NEG_INF)
    bias = jnp.concatenate([bias_one] * SWA_GROUP, axis=0)
    ones = jnp.ones((2 * WINDOW, SWA_HEAD_DIM), BF16)

    stacked, scores, sinks = [], [], []
    for kv in range(SWA_KV_HEADS):
        heads = range(kv * SWA_GROUP, (kv + 1) * SWA_GROUP)
        kv_cols = slice(kv * SWA_HEAD_DIM, (kv + 1) * SWA_HEAD_DIM)
        qs = jnp.concatenate([q[:, h * SWA_HEAD_DIM:(h + 1) * SWA_HEAD_DIM] for h in heads], axis=0)
        scores.append(_dot_nt(qs, keys[:, kv_cols]) + bias)
        sinks.append(jnp.concatenate([jnp.full((WINDOW, 1), sink_ref[h] * LOG2_E, F32) for h in heads], axis=0))
    for kv in range(SWA_KV_HEADS):
        kv_cols = slice(kv * SWA_HEAD_DIM, (kv + 1) * SWA_HEAD_DIM)
        s, sink = scores[kv], sinks[kv]
        m = jnp.maximum(jnp.max(s, axis=-1, keepdims=True), sink)
        p = jnp.exp2(s - m).astype(BF16)
        denom = jnp.dot(p, ones, preferred_element_type=F32) + jnp.exp2(sink - m)
        stacked.append(_dot(p, vals[:, kv_cols]) / denom)
    for kv in range(SWA_KV_HEADS):
        for pair in range(SWA_GROUP // 2):
            g0 = 2 * pair
            both = jnp.concatenate([stacked[kv][g * WINDOW:(g + 1) * WINDOW] for g in (g0, g0 + 1)], axis=1)
            col = (kv * SWA_GROUP + g0) * SWA_HEAD_DIM
            o_ref[:, col:col + 2 * SWA_HEAD_DIM] = both.astype(o_ref.dtype)

    kprev[...] = k_cur
    vprev[...] = v_cur

    @pl.when(n == pl.num_programs(1) - 1)
    def _():
        kc_ref[0] = k_cur
        vc_ref[0] = v_cur


def _swa_prompt(qkv, sinks, tables):
    nb = SEQ // WINDOW
    q_blocks = SWA_Q // SWA_KV
    tab = pl.BlockSpec((WINDOW, LANES), lambda b, n, *_: (n, 0))
    cache = pl.BlockSpec((1, WINDOW, SWA_KV), lambda b, n, *_: (b, 0, 0))
    return pl.pallas_call(
        _swa_prompt_kernel,
        grid_spec=pltpu.PrefetchScalarGridSpec(
            num_scalar_prefetch=1,
            grid=(BATCH, nb),
            in_specs=[pl.BlockSpec((WINDOW, SWA_Q), lambda b, n, *_: (b * nb + n, 0)),
                      pl.BlockSpec((WINDOW, SWA_KV), lambda b, n, *_: (b * nb + n, q_blocks)),
                      pl.BlockSpec((WINDOW, SWA_KV), lambda b, n, *_: (b * nb + n, q_blocks + 1)),
                      tab, tab, tab],
            out_specs=[pl.BlockSpec((WINDOW, SWA_Q), lambda b, n, *_: (b * nb + n, 0)), cache, cache],
            scratch_shapes=[pltpu.VMEM((WINDOW, SWA_KV), F32), pltpu.VMEM((WINDOW, SWA_KV), F32)]),
        out_shape=[jax.ShapeDtypeStruct((N_PROMPT, SWA_Q), BF16),
                   jax.ShapeDtypeStruct((BATCH, WINDOW, SWA_KV), F32),
                   jax.ShapeDtypeStruct((BATCH, WINDOW, SWA_KV), F32)],
        compiler_params=pltpu.CompilerParams(dimension_semantics=("arbitrary", "arbitrary")),
        name="swa_prompt",
    )(sinks, qkv, qkv, qkv, *tables)


def _swa_sample_kernel(q_ref, kn_ref, vn_ref, kt_ref, vt_ref, sink_ref,
                       qc_ref, qlo_ref, qhi_ref, kc_tab, klo_tab, khi_tab,
                       o_ref, kto_ref, vto_ref, knew, vnew):
    rows = SWA_KV_HEADS * DEC_SEQ * SWA_GROUP
    token = lambda w: jnp.bitwise_and(jnp.right_shift(lax.broadcasted_iota(jnp.int32, (rows, w), 0), 2), DEC_SEQ - 1)
    valid_old = lax.broadcasted_iota(jnp.int32, (rows, WINDOW), 1) > token(WINDOW)
    valid_new = lax.broadcasted_iota(jnp.int32, (rows, PAD_CHUNK), 1) <= token(PAD_CHUNK)
    sink = sink_ref[:, 0:1]
    lane_head = jnp.right_shift(lax.broadcasted_iota(jnp.int32, (DEC_SEQ * SWA_GROUP, SWA_KV), 1), 6)
    zero_tail = jnp.zeros((PAD_CHUNK - DEC_SEQ, SWA_KV), F32)
    slot = lax.broadcasted_iota(jnp.int32, (SWA_KV, WINDOW), 1)
    place = (lax.broadcasted_iota(jnp.int32, (PAD_CHUNK, WINDOW), 1)
             == lax.broadcasted_iota(jnp.int32, (PAD_CHUNK, WINDOW), 0) + (WINDOW - DEC_SEQ)).astype(F32)

    def shifted(old_t, new_rows):
        appended = lax.dot_general(new_rows, place, (((0,), (0,)), ((), ())), preferred_element_type=F32,
                                   precision=lax.Precision.HIGHEST)
        return jnp.where(slot >= WINDOW - DEC_SEQ, appended, pltpu.roll(old_t, WINDOW - DEC_SEQ, 1))

    for b in range(SWA_SAMPLE_BLOCK):
        knew[0:DEC_SEQ, :] = _rope(kn_ref[b], kc_tab[...], klo_tab[...], khi_tab[...])
        vnew[0:DEC_SEQ, :] = vn_ref[b]
        knew[DEC_SEQ:PAD_CHUNK, :] = zero_tail
        vnew[DEC_SEQ:PAD_CHUNK, :] = zero_tail
        k_new, v_new = knew[...], vnew[...]
        k_old, v_old = kt_ref[b], vt_ref[b]

        q = _rope(q_ref[b], qc_ref[...], qlo_ref[...], qhi_ref[...]) * (SWA_HEAD_DIM ** -0.5)
        s_old = jnp.where(valid_old, _dot(q, k_old), NEG_INF)
        s_new = jnp.where(valid_new, _dot_nt(q, k_new), NEG_INF)
        m = jnp.maximum(jnp.maximum(jnp.max(s_old, axis=-1, keepdims=True),
                                    jnp.max(s_new, axis=-1, keepdims=True)), sink)
        p_old = jnp.exp(s_old - m)
        p_new = jnp.exp(s_new - m)
        denom = (jnp.sum(p_old, axis=-1, keepdims=True) + jnp.sum(p_new, axis=-1, keepdims=True)
                 + jnp.exp(sink - m))
        o_all = (_dot_nt(p_old, v_old) + _dot(p_new, v_new)) / denom
        per_head = DEC_SEQ * SWA_GROUP
        o = jnp.zeros((per_head, SWA_KV), F32)
        for h in range(SWA_KV_HEADS):
            o = o + jnp.where(lane_head == h, o_all[h * per_head:(h + 1) * per_head, :], 0.0)
        o_ref[b] = o.astype(o_ref.dtype)
        kto_ref[b] = shifted(k_old, k_new)
        vto_ref[b] = shifted(v_old, v_new)


def _swa_sample(qkv_s, layer, cache_kt, cache_vt, sinks, pos):
    q = qkv_s[:, :SWA_Q].reshape(DEC_BATCH, DEC_SEQ, SWA_KV_HEADS, SWA_GROUP, SWA_HEAD_DIM)
    q = jnp.transpose(q, (0, 2, 1, 3, 4))
    eye = jnp.eye(SWA_KV_HEADS, dtype=F32)
    rows = SWA_KV_HEADS * DEC_SEQ * SWA_GROUP
    q_bd = (q[:, :, :, :, None, :] * eye[None, :, None, None, :, None]).reshape(DEC_BATCH, rows, SWA_KV)
    k_new = qkv_s[:, SWA_Q:SWA_Q + SWA_KV].reshape(DEC_BATCH, DEC_SEQ, SWA_KV)
    v_new = qkv_s[:, SWA_Q + SWA_KV:].reshape(DEC_BATCH, DEC_SEQ, SWA_KV)
    row_t = (jnp.arange(rows) // SWA_GROUP) % DEC_SEQ
    q_tabs = _rope_tables(pos[row_t], LANES)
    k_tabs = _rope_tables(pos, LANES)
    row_head = (jnp.arange(rows) // (DEC_SEQ * SWA_GROUP)) * SWA_GROUP + jnp.arange(rows) % SWA_GROUP
    sink_rows = jnp.broadcast_to(sinks[row_head][:, None], (rows, LANES))

    nblk = SWA_SAMPLE_BLOCK
    blk = lambda r, c: pl.BlockSpec((nblk, r, c), lambda i: (i, 0, 0))
    window_in = pl.BlockSpec((None, nblk, SWA_KV, WINDOW), lambda i: (layer, i, 0, 0))
    per_head = DEC_SEQ * SWA_GROUP
    o, ko, vo = pl.pallas_call(
        _swa_sample_kernel,
        grid=(DEC_BATCH // nblk,),
        in_specs=[blk(rows, SWA_KV), blk(DEC_SEQ, SWA_KV), blk(DEC_SEQ, SWA_KV),
                  window_in, window_in, _const_spec((rows, LANES)),
                  _const_spec((rows, LANES)), _const_spec((rows, LANES)), _const_spec((rows, LANES)),
                  _const_spec((DEC_SEQ, LANES)), _const_spec((DEC_SEQ, LANES)), _const_spec((DEC_SEQ, LANES))],
        out_specs=[blk(per_head, SWA_KV), blk(SWA_KV, WINDOW), blk(SWA_KV, WINDOW)],
        out_shape=[jax.ShapeDtypeStruct((DEC_BATCH, per_head, SWA_KV), BF16),
                   jax.ShapeDtypeStruct((DEC_BATCH, SWA_KV, WINDOW), F32),
                   jax.ShapeDtypeStruct((DEC_BATCH, SWA_KV, WINDOW), F32)],
        scratch_shapes=[pltpu.VMEM((PAD_CHUNK, SWA_KV), F32), pltpu.VMEM((PAD_CHUNK, SWA_KV), F32)],
        compiler_params=pltpu.CompilerParams(dimension_semantics=("arbitrary",)),
        name="swa_sample",
    )(q_bd, k_new, v_new, cache_kt, cache_vt, sink_rows, *q_tabs, *k_tabs)
    o = o.reshape(DEC_BATCH, DEC_SEQ, SWA_GROUP, SWA_KV_HEADS, SWA_HEAD_DIM)
    o = jnp.transpose(o, (0, 1, 3, 2, 4)).reshape(N_SAMPLE, SWA_Q)
    return o, ko, vo


def _mixer_out_dtype(chunk):
    return BF16 if chunk % (2 * SUBLANES) == 0 else F32


def _tri(n, strict=False):
    i = lax.broadcasted_iota(jnp.int32, (n, n), 0)
    j = lax.broadcasted_iota(jnp.int32, (n, n), 1)
    return (j < i) if strict else (j <= i)


def _gdn_kernel(*refs, chunk, n_valid, has_state):
    it = iter(refs)
    x_ref, z_ref, b_ref, a_ref = next(it), next(it), next(it), next(it)
    cw_ref, alog_ref, dtb_ref, ng_ref = next(it), next(it), next(it), next(it)
    if has_state:
        s0_ref, c0_ref = next(it), next(it)
    o_ref, s_ref, tail_ref = next(it), next(it), next(it)
    prev = next(it)
    c = pl.program_id(1)

    @pl.when(c == 0)
    def _():
        if has_state:
            s_ref[...] = s0_ref[...]
            prev[0:SUBLANES - GDN_CONV + 1, :] = jnp.zeros((SUBLANES - GDN_CONV + 1, GDN_CONV_DIM), F32)
            prev[SUBLANES - GDN_CONV + 1:SUBLANES, :] = c0_ref[0]
        else:
            s_ref[...] = jnp.zeros_like(s_ref)
            prev[...] = jnp.zeros_like(prev)

    x = x_ref[...]
    full = jnp.concatenate([prev[...], x], axis=0)
    acc = x * cw_ref[GDN_CONV - 1:GDN_CONV, :]
    for tap in range(GDN_CONV - 1):
        back = GDN_CONV - 1 - tap
        acc = acc + pltpu.roll(full, back, 0)[SUBLANES:, :] * cw_ref[tap:tap + 1, :]
    prev[...] = x[chunk - SUBLANES:chunk, :]
    tail_ref[0] = x[chunk - SUBLANES:chunk, :]
    xc = _silu(acc)

    row = lax.broadcasted_iota(jnp.int32, (chunk, LANES), 0)
    live = row < n_valid
    beta = jnp.where(live, _sigmoid(b_ref[...]), 0.0)
    a = a_ref[...] + dtb_ref[...]
    softplus = jnp.maximum(a, 0.0) + jnp.log(1.0 + jnp.exp(-jnp.abs(a)))
    g = jnp.where(live, -jnp.exp(alog_ref[...]) * softplus, 0.0)

    incl = _tri(chunk)
    strict = _tri(chunk, strict=True)
    incl_f = incl.astype(F32)
    gcum = _dot_f32(incl_f, g)
    upper_f = (lax.broadcasted_iota(jnp.int32, (chunk, chunk), 0)
               <= lax.broadcasted_iota(jnp.int32, (chunk, chunk), 1)).astype(F32)
    gcum_t = lax.dot_general(g, upper_f, (((0,), (0,)), ((), ())), preferred_element_type=F32,
                             precision=lax.Precision.HIGHEST)
    eye = (lax.broadcasted_iota(jnp.int32, (chunk, chunk), 0)
           == lax.broadcasted_iota(jnp.int32, (chunk, chunk), 1)).astype(F32)
    n_pow = int(math.log2(chunk))
    ng = ng_ref[...]
    rep = GDN_V_HEADS // GDN_QK_HEADS
    heads = range(GDN_V_HEADS)

    qn, kn, kk, qk, ks, qs = [], [], [], [], [], []
    for hq in range(GDN_QK_HEADS):
        qh = xc[:, hq * GDN_HEAD_DIM:(hq + 1) * GDN_HEAD_DIM]
        kh = xc[:, GDN_KEY_DIM + hq * GDN_HEAD_DIM:GDN_KEY_DIM + (hq + 1) * GDN_HEAD_DIM]
        qn.append(qh * lax.rsqrt(jnp.sum(qh * qh, axis=-1, keepdims=True) + NORM_EPS) * (GDN_HEAD_DIM ** -0.5))
        kn.append(kh * lax.rsqrt(jnp.sum(kh * kh, axis=-1, keepdims=True) + NORM_EPS))
        kq = jnp.concatenate([kn[hq], qn[hq]], axis=0)
        scores = _dot_nt(kq, kn[hq])
        kk.append(scores[:chunk])
        qk.append(scores[chunk:])
        s_pair = jnp.concatenate([s_ref[0, hq * rep + r] for r in range(rep)], axis=1)
        against_state = _dot(kq, s_pair)
        for r in range(rep):
            ks.append(against_state[:chunk, r * GDN_HEAD_DIM:(r + 1) * GDN_HEAD_DIM])
            qs.append(against_state[chunk:, r * GDN_HEAD_DIM:(r + 1) * GDN_HEAD_DIM])

    gc = [gcum[:, h:h + 1] for h in heads]
    bc = [beta[:, h:h + 1] for h in heads]
    decay = [jnp.exp(jnp.where(incl, gc[h] - gcum_t[h:h + 1, :], NEG_INF)) for h in heads]
    q_pow = [-(jnp.where(strict, kk[h // rep] * decay[h], 0.0) * bc[h]) for h in heads]
    t_mat = [eye + q_pow[h] for h in heads]
    q_pow = [_dot(q_pow[h], q_pow[h]) for h in heads]
    for _ in range(n_pow - 2):
        both = [_dot(jnp.concatenate([t_mat[h], q_pow[h]], axis=0), q_pow[h]) for h in heads]
        t_mat = [t_mat[h] + both[h][:chunk] for h in heads]
        q_pow = [both[h][chunk:] for h in heads]
    t_mat = [t_mat[h] + _dot(t_mat[h], q_pow[h]) for h in heads]

    egc = [jnp.exp(gc[h]) for h in heads]
    v_new = []
    for h in heads:
        vh = xc[:, 2 * GDN_KEY_DIM + h * GDN_HEAD_DIM:2 * GDN_KEY_DIM + (h + 1) * GDN_HEAD_DIM]
        v_new.append(_dot(t_mat[h], bc[h] * (vh - egc[h] * ks[h])))
    for h in heads:
        o = egc[h] * qs[h] + _dot(qk[h // rep] * decay[h], v_new[h])
        g_last = gcum[chunk - 1:chunk, h:h + 1]
        s_ref[0, h] = (s_ref[0, h] * jnp.exp(g_last)
                       + _dot_tn(kn[h // rep] * jnp.exp(g_last - gc[h]), v_new[h]))
        zh = z_ref[:, h * GDN_HEAD_DIM:(h + 1) * GDN_HEAD_DIM]
        o_ref[:, h * GDN_HEAD_DIM:(h + 1) * GDN_HEAD_DIM] = (_rms(o, ng) * _silu(zh)).astype(o_ref.dtype)


def _gdn(x, z, b, a, conv_w, a_log, dt_bias, norm_g, n_seq, n_chunks, chunk, n_valid, state=None, conv0=None):
    has_state = state is not None
    rows = n_seq * n_chunks * chunk
    blk = lambda w: pl.BlockSpec((chunk, w), lambda s, c: (s * n_chunks + c, 0))
    pad_heads = lambda v: jnp.pad(v.astype(F32), (0, LANES - GDN_V_HEADS)).reshape(1, LANES)
    state_spec = pl.BlockSpec((1, GDN_V_HEADS, GDN_HEAD_DIM, GDN_HEAD_DIM), lambda s, c: (s, 0, 0, 0))
    in_specs = [blk(GDN_CONV_DIM), blk(GDN_V_DIM), blk(LANES), blk(LANES),
                _const_spec((GDN_CONV, GDN_CONV_DIM)), _const_spec((1, LANES)), _const_spec((1, LANES)),
                _const_spec((1, GDN_HEAD_DIM))]
    args = [x, z, b, a, conv_w, pad_heads(a_log), pad_heads(dt_bias), norm_g.reshape(1, GDN_HEAD_DIM)]
    if has_state:
        in_specs += [state_spec, pl.BlockSpec((1, GDN_CONV - 1, GDN_CONV_DIM), lambda s, c: (s, 0, 0))]
        args += [state, conv0]
    return pl.pallas_call(
        functools.partial(_gdn_kernel, chunk=chunk, n_valid=n_valid, has_state=has_state),
        grid=(n_seq, n_chunks),
        in_specs=in_specs,
        out_specs=[blk(GDN_V_DIM), state_spec,
                   pl.BlockSpec((1, SUBLANES, GDN_CONV_DIM), lambda s, c: (s, 0, 0))],
        out_shape=[jax.ShapeDtypeStruct((rows, GDN_V_DIM), _mixer_out_dtype(chunk)),
                   jax.ShapeDtypeStruct((n_seq, GDN_V_HEADS, GDN_HEAD_DIM, GDN_HEAD_DIM), F32),
                   jax.ShapeDtypeStruct((n_seq, SUBLANES, GDN_CONV_DIM), F32)],
        scratch_shapes=[pltpu.VMEM((SUBLANES, GDN_CONV_DIM), F32)],
        compiler_params=pltpu.CompilerParams(dimension_semantics=("arbitrary", "arbitrary"),
                                             vmem_limit_bytes=VMEM_LIMIT),
        name="gdn",
    )(*args)


def _hgrn_kernel(*refs, chunk, n_valid, has_state, layer):
    it = iter(refs)
    x_ref, lbl_ref, ng_ref = next(it), next(it), next(it)
    if has_state:
        s0_ref = next(it)
    o_ref, s_ref = next(it), next(it)
    st = next(it)
    c = pl.program_id(1)

    @pl.when(c == 0)
    def _():
        for h in range(HG_HEADS):
            st[h] = s0_ref[0, h].T if has_state else jnp.zeros((HG_HEAD_DIM, HG_HEAD_DIM), F32)

    logits = [lbl_ref[i:i + 1, :] for i in range(DEPTH)]
    mx = functools.reduce(jnp.maximum, logits)
    ex = [jnp.exp(l - mx) for l in logits]
    tot = functools.reduce(lambda u, v: u + v, ex)
    probs = [e / tot for e in ex]
    lb = functools.reduce(lambda u, v: u + v, probs[:layer + 1]) - probs[0]

    row = lax.broadcasted_iota(jnp.int32, (chunk, HG_DIM), 0)
    live = row < n_valid
    fz = x_ref[:, HG_DIM:2 * HG_DIM]
    log_f = jnp.where(live, jnp.log(lb + (1.0 - lb) * _sigmoid(fz)), 0.0)
    k_all = jnp.where(live, (1.0 - lb) * _sigmoid(-fz), 0.0)
    q_all = _silu(x_ref[:, 0:HG_DIM])
    incl_f = _tri(chunk).astype(F32)
    b2_all = _dot_f32(incl_f, log_f) * LOG2_E
    ones_l = jnp.ones((LANES, LANES), BF16)
    ng = ng_ref[...]
    n_blk = chunk // SUBLANES
    sub = lax.broadcasted_iota(jnp.int32, (SUBLANES, LANES), 0)

    rows8 = lambda t, bi: t[bi * SUBLANES:(bi + 1) * SUBLANES, :]
    heads = range(HG_HEADS)
    head_cols = [slice(h * HG_HEAD_DIM, (h + 1) * HG_HEAD_DIM) for h in heads]
    q_h = [q_all[:, cs] for cs in head_cols]
    k_h = [k_all[:, cs] for cs in head_cols]
    b_h = [b2_all[:, cs] for cs in head_cols]
    v_h = [x_ref[:, 2 * HG_DIM + h * HG_HEAD_DIM:2 * HG_DIM + (h + 1) * HG_HEAD_DIM] for h in heads]
    seg = [0]
    for bi in range(1, n_blk):
        seg.append(seg[-1] + bi * SUBLANES)

    o_inter, w_sum, p_all = [], [], []
    for h in heads:
        q, k, b2 = q_h[h], k_h[h], b_h[h]
        o_inter.append(_dot_nt(q * jnp.exp2(b2), st[h]))
        tiles = []
        for bi in range(n_blk):
            for jj in range(SUBLANES):
                j = bi * SUBLANES + jj
                e = jnp.where(sub >= jj, rows8(b2, bi) - b2[j:j + 1, :], NEG_INF)
                tiles.append(rows8(q, bi) * k[j:j + 1, :] * jnp.exp2(e))
        w_sum.append(jnp.dot(jnp.concatenate(tiles, axis=0).astype(BF16), ones_l, preferred_element_type=F32))
        if n_blk > 1:
            q_t, k_t = [], []
            for bi in range(1, n_blk):
                r = b2[bi * SUBLANES - 1:bi * SUBLANES, :]
                q_t.append(rows8(q, bi) * jnp.exp2(rows8(b2, bi) - r))
                k_t.append(k[0:bi * SUBLANES, :] * jnp.exp2(r - b2[0:bi * SUBLANES, :]))
            p_all.append(_dot_nt(jnp.concatenate(q_t, axis=0), jnp.concatenate(k_t, axis=0)))

    if n_blk > 1:
        prow = jnp.right_shift(lax.broadcasted_iota(jnp.int32, p_all[0].shape, 0), 3)
        pcol = lax.broadcasted_iota(jnp.int32, p_all[0].shape, 1)
        own = functools.reduce(jnp.logical_or, [
            jnp.logical_and(prow == bi - 1, jnp.logical_and(pcol >= seg[bi - 1], pcol < seg[bi]))
            for bi in range(1, n_blk)])
    o_intra = []
    for h in heads:
        v = v_h[h]
        o_blocks = []
        for bi in range(n_blk):
            terms = [rows8(w_sum[h], bi * SUBLANES + jj) * v[bi * SUBLANES + jj:bi * SUBLANES + jj + 1, :]
                     for jj in range(SUBLANES)]
            o_blocks.append(functools.reduce(lambda u, w: u + w, terms))
        if n_blk > 1:
            v_t = jnp.concatenate([v[0:bi * SUBLANES, :] for bi in range(1, n_blk)], axis=0)
            o_off = _dot(jnp.where(own, p_all[h], 0.0), v_t)
            for bi in range(1, n_blk):
                o_blocks[bi] = o_blocks[bi] + rows8(o_off, bi - 1)
        o_intra.append(o_blocks[0] if n_blk == 1 else jnp.concatenate(o_blocks, axis=0))

    for h in heads:
        k, b2 = k_h[h], b_h[h]
        b_last = b2[chunk - 1:chunk, :]
        st[h] = st[h] * jnp.exp2(b_last) + _dot_tn(v_h[h], k * jnp.exp2(b_last - b2))
        gate = x_ref[:, 3 * HG_DIM + h * HG_HEAD_DIM:3 * HG_DIM + (h + 1) * HG_HEAD_DIM]
        o = o_inter[h] + o_intra[h]
        o_ref[:, head_cols[h]] = (_rms(o, ng) * _silu(gate)).astype(o_ref.dtype)

    @pl.when(c == pl.num_programs(1) - 1)
    def _():
        for h in range(HG_HEADS):
            s_ref[0, h] = st[h].T


def _hgrn(x, lb_logits, norm_g, layer, n_seq, n_chunks, chunk, n_valid, state=None):
    has_state = state is not None
    rows = n_seq * n_chunks * chunk
    state_spec = pl.BlockSpec((1, HG_HEADS, HG_HEAD_DIM, HG_HEAD_DIM), lambda s, c: (s, 0, 0, 0))
    in_specs = [pl.BlockSpec((chunk, 4 * HG_DIM), lambda s, c: (s * n_chunks + c, 0)),
                _const_spec((DEPTH, HG_DIM)), _const_spec((1, HG_HEAD_DIM))]
    args = [x, lb_logits, norm_g.reshape(1, HG_HEAD_DIM)]
    if has_state:
        in_specs.append(state_spec)
        args.append(state)
    return pl.pallas_call(
        functools.partial(_hgrn_kernel, chunk=chunk, n_valid=n_valid, has_state=has_state, layer=layer),
        grid=(n_seq, n_chunks),
        in_specs=in_specs,
        out_specs=[pl.BlockSpec((chunk, HG_DIM), lambda s, c: (s * n_chunks + c, 0)), state_spec],
        out_shape=[jax.ShapeDtypeStruct((rows, HG_DIM), _mixer_out_dtype(chunk)),
                   jax.ShapeDtypeStruct((n_seq, HG_HEADS, HG_HEAD_DIM, HG_HEAD_DIM), F32)],
        scratch_shapes=[pltpu.VMEM((HG_HEADS, HG_HEAD_DIM, HG_HEAD_DIM), F32)],
        compiler_params=pltpu.CompilerParams(dimension_semantics=("arbitrary", "arbitrary"),
                                             vmem_limit_bytes=VMEM_LIMIT),
        name="hgrn",
    )(*args)


def _pad_sample(t):
    w = t.shape[1]
    t = jnp.pad(t.reshape(DEC_BATCH, DEC_SEQ, w), ((0, 0), (0, PAD_CHUNK - DEC_SEQ), (0, 0)))
    return t.reshape(DEC_BATCH * PAD_CHUNK, w)


def _unpad_sample(t):
    w = t.shape[1]
    return t.reshape(DEC_BATCH, PAD_CHUNK, w)[:, :DEC_SEQ].reshape(N_SAMPLE, w).astype(BF16)


def _pad_cols(w, n):
    return jnp.pad(w, ((0, 0), (0, n - w.shape[1])))


def kernel(x_prompt, x_sample, cache_swa_k, cache_swa_v, state_gdn, state_gdn_conv, state_hgrn, norm_ffn, ffn_w_in, ffn_w_out, norm_mix, swa_w_in, swa_w_out, swa_sinks, gdn_w_in, gdn_conv_w, gdn_a_log, gdn_dt_bias, gdn_norm, gdn_w_out, hgrn_w_in, hgrn_lb_logits, hgrn_norm, hgrn_w_out, final_norm):
    x = (x_prompt.reshape(N_PROMPT, D_MODEL), x_sample.reshape(N_SAMPLE, D_MODEL))
    ffn = functools.partial(_ffn, norm_ffn=norm_ffn.reshape(DEPTH, 2, 1, D_MODEL),
                            w_in=ffn_w_in.astype(BF16), w_out=ffn_w_out.astype(BF16))
    prompt_tables = _rope_tables(jnp.arange(SEQ), LANES)
    sample_pos = PAST_LEN + jnp.arange(DEC_SEQ)
    window_cols = lambda c: jnp.transpose(c, (0, 1, 3, 4, 2)).reshape(c.shape[0], DEC_BATCH, SWA_KV, WINDOW)
    window_rows = lambda c: jnp.transpose(c.reshape(DEC_BATCH, SWA_KV_HEADS, SWA_HEAD_DIM, WINDOW), (0, 3, 1, 2))
    cache_kt, cache_vt = window_cols(cache_swa_k), window_cols(cache_swa_v)

    outs = {k: [] for k in ("pk", "pv", "pg", "pc", "ph", "sk", "sv", "sg", "sc", "sh")}
    proj = None
    for i in range(DEPTH):
        kind, j = i % N_MIXERS, i // N_MIXERS
        if proj is not None:
            x = ffn(x, i - 1, 1, proj=proj)
        x = ffn(x, i, 0)
        if kind == 0:
            (qkv,) = _norm_proj(x, norm_mix[i], [swa_w_in[j].astype(BF16)], TOKEN_TILE)
            o_p, pk, pv = _swa_prompt(qkv, swa_sinks[j], prompt_tables)
            o_s, sk, sv = _swa_sample(qkv[N_PROMPT:], j, cache_kt, cache_vt, swa_sinks[j], sample_pos)
            kv_shape = (WINDOW, SWA_KV_HEADS, SWA_HEAD_DIM)
            outs["pk"].append(pk.reshape((BATCH,) + kv_shape))
            outs["pv"].append(pv.reshape((BATCH,) + kv_shape))
            outs["sk"].append(window_rows(sk))
            outs["sv"].append(window_rows(sv))
            proj = (o_p, o_s, swa_w_out[j].astype(BF16))
        elif kind == 1:
            w = gdn_w_in[j].astype(BF16)
            z0 = GDN_CONV_DIM + GDN_V_DIM
            weights = [w[:, :GDN_CONV_DIM], w[:, GDN_CONV_DIM:z0],
                       _pad_cols(w[:, z0:z0 + GDN_V_HEADS], LANES), _pad_cols(w[:, z0 + GDN_V_HEADS:], LANES)]
            qkv, z, b, a = _norm_proj(x, norm_mix[i], weights, TOKEN_TILE // 2)
            o_p, pg, p_tail = _gdn(qkv, z, b, a, gdn_conv_w[j], gdn_a_log[j], gdn_dt_bias[j], gdn_norm[j],
                                   BATCH, SEQ // GDN_CHUNK, GDN_CHUNK, GDN_CHUNK)
            o_s, sg, s_tail = _gdn(*[_pad_sample(t[N_PROMPT:]) for t in (qkv, z, b, a)],
                                   gdn_conv_w[j], gdn_a_log[j], gdn_dt_bias[j], gdn_norm[j],
                                   DEC_BATCH, 1, PAD_CHUNK, DEC_SEQ, state=state_gdn[j], conv0=state_gdn_conv[j])
            outs["pg"].append(pg)
            outs["sg"].append(sg)
            keep = GDN_CONV - 1
            outs["pc"].append(p_tail[:, SUBLANES - keep:])
            outs["sc"].append(s_tail[:, DEC_SEQ - keep:DEC_SEQ])
            proj = (o_p, _unpad_sample(o_s), gdn_w_out[j].astype(BF16))
        else:
            (xin,) = _norm_proj(x, norm_mix[i], [hgrn_w_in[j].astype(BF16)], TOKEN_TILE)
            o_p, ph = _hgrn(xin, hgrn_lb_logits, hgrn_norm[j], i, BATCH, SEQ // HG_CHUNK, HG_CHUNK, HG_CHUNK)
            o_s, sh = _hgrn(_pad_sample(xin[N_PROMPT:]), hgrn_lb_logits, hgrn_norm[j], i,
                            DEC_BATCH, 1, PAD_CHUNK, DEC_SEQ, state=state_hgrn[j])
            outs["ph"].append(ph)
            outs["sh"].append(sh)
            proj = (o_p, _unpad_sample(o_s), hgrn_w_out[j].astype(BF16))
    y_prompt, y_sample = ffn(x, DEPTH - 1, 1, proj=proj, final_g=final_norm)
    y_prompt = y_prompt.reshape(BATCH, SEQ, D_MODEL)
    y_sample = y_sample.reshape(DEC_BATCH, DEC_SEQ, D_MODEL)
    st = lambda k: jnp.stack(outs[k])
    return (y_prompt, y_sample, st("pk"), st("pv"), st("pg"), st("pc"), st("ph"),
            st("sk"), st("sv"), st("sg"), st("sc"), st("sh"))
```

```python
import functools
import math

import jax
import jax.numpy as jnp
from jax import lax
from jax.experimental import pallas as pl
from jax.experimental.pallas import tpu as pltpu

D_MODEL = 1024
BATCH = 8
SEQ = 2048
DEPTH = 4
DEC_BATCH = 128
DEC_SEQ = 4
PAST_LEN = 8192
N_MIXERS = 3

SWA_HEAD_DIM = 64
SWA_HEADS = 16
SWA_KV_HEADS = 4
SWA_GROUP = 4
WINDOW = 128
ROT_DIM = 16
ROPE_THETA = 500000.0
SWA_Q = SWA_HEADS * SWA_HEAD_DIM
SWA_KV = SWA_KV_HEADS * SWA_HEAD_DIM

GDN_HEAD_DIM = 128
GDN_QK_HEADS = 8
GDN_V_HEADS = 16
GDN_KEY_DIM = 1024
GDN_V_DIM = 2048
GDN_CONV_DIM = 4096
GDN_CONV = 4

HG_HEAD_DIM = 128
HG_HEADS = 8
HG_DIM = 1024

D_FF = 2816
NORM_EPS = 1e-6
NEG_INF = -1e30
LOG2_E = 1.4426950408889634

LANES = 128
SUBLANES = 8
TOKEN_TILE = 512
N_PROMPT = BATCH * SEQ
N_SAMPLE = DEC_BATCH * DEC_SEQ
N_TOKENS = N_PROMPT + N_SAMPLE
N_PROMPT_TILES = N_PROMPT // TOKEN_TILE
FFN_CHUNK = 256
VMEM_LIMIT = 56 * 1024 * 1024
GDN_CHUNK = 64
HG_CHUNK = 64
PAD_CHUNK = 8
SWA_SAMPLE_BLOCK = 8

F32 = jnp.float32
BF16 = jnp.bfloat16


def _const_spec(shape):
    nd = len(shape)
    return pl.BlockSpec(shape, lambda *_: (0,) * nd, pipeline_mode=pl.Buffered(1))


def _rms(x, g):
    return x * lax.rsqrt(jnp.mean(x * x, axis=-1, keepdims=True) + NORM_EPS) * g


def _sigmoid(x):
    return 1.0 / (1.0 + jnp.exp(-x))


def _silu(x):
    return x * _sigmoid(x)


def _dot(a, b):
    return jnp.dot(a.astype(BF16), b.astype(BF16), preferred_element_type=F32)


def _dot_nt(a, b):
    return lax.dot_general(a.astype(BF16), b.astype(BF16), (((1,), (1,)), ((), ())),
                           preferred_element_type=F32)


def _dot_tn(a, b):
    return lax.dot_general(a.astype(BF16), b.astype(BF16), (((0,), (0,)), ((), ())),
                           preferred_element_type=F32)


def _dot_f32(a, b):
    return jnp.dot(a, b, preferred_element_type=F32, precision=lax.Precision.HIGHEST)


def _ffn_kernel(*refs, split_in, has_proj, final_norm):
    it = iter(refs)
    is_prompt = pl.program_id(0) < N_PROMPT_TILES
    if split_in:
        xp_ref, xs_ref = next(it), next(it)
        x = jnp.where(is_prompt, xp_ref[...], xs_ref[...])
    else:
        x = next(it)[...]
    if has_proj:
        op_ref, os_ref, wp_ref = next(it), next(it), next(it)
    g_ref, wg_ref, wu_ref, wo_ref = next(it), next(it), next(it), next(it)
    gf_ref = next(it) if final_norm else None
    out_refs = list(it)

    if has_proj:
        o = jnp.where(is_prompt, op_ref[...], os_ref[...])
        x = x + jnp.dot(o, wp_ref[...], preferred_element_type=F32)
    h = _rms(x, g_ref[...]).astype(BF16)
    y = jnp.zeros_like(x)
    for c in range(D_FF // FFN_CHUNK):
        cols = slice(c * FFN_CHUNK, (c + 1) * FFN_CHUNK)
        gate = jnp.dot(h, wg_ref[:, cols], preferred_element_type=F32)
        up = jnp.dot(h, wu_ref[:, cols], preferred_element_type=F32)
        act = (_silu(gate) * up).astype(BF16)
        y = y + jnp.dot(act, wo_ref[cols, :], preferred_element_type=F32)
    x = x + 0.5 * y
    if final_norm:
        x = _rms(x, gf_ref[...])
    if len(out_refs) == 1:
        out_refs[0][...] = x
    else:
        @pl.when(is_prompt)
        def _():
            out_refs[0][...] = x

        @pl.when(jnp.logical_not(is_prompt))
        def _():
            out_refs[1][...] = x


def _ffn(x, layer, which, norm_ffn, w_in, w_out, proj=None, final_g=None):
    tile = pl.BlockSpec((TOKEN_TILE, D_MODEL), lambda i: (i, 0))
    prompt_tile = lambda w: pl.BlockSpec((TOKEN_TILE, w), lambda i: (jnp.minimum(i, N_PROMPT_TILES - 1), 0))
    sample_tile = lambda w: pl.BlockSpec((TOKEN_TILE, w), lambda i: (0, 0))
    split_in = isinstance(x, tuple)
    in_specs = [prompt_tile(D_MODEL), sample_tile(D_MODEL)] if split_in else [tile]
    args = list(x) if split_in else [x]
    if proj is not None:
        o_p, o_s, w_p = proj
        k = w_p.shape[0]
        in_specs += [prompt_tile(k), sample_tile(k), _const_spec((k, D_MODEL))]
        args += [o_p, o_s, w_p]
    once = pl.Buffered(1)
    in_specs += [pl.BlockSpec((None, None, 1, D_MODEL), lambda i: (layer, which, 0, 0), pipeline_mode=once),
                 pl.BlockSpec((None, None, D_MODEL, D_FF), lambda i: (layer, which, 0, 0), pipeline_mode=once),
                 pl.BlockSpec((None, None, D_MODEL, D_FF), lambda i: (layer, which, 0, 1), pipeline_mode=once),
                 pl.BlockSpec((None, None, D_FF, D_MODEL), lambda i: (layer, which, 0, 0), pipeline_mode=once)]
    args += [norm_ffn, w_in, w_in, w_out]
    if final_g is not None:
        in_specs.append(_const_spec((1, D_MODEL)))
        args.append(final_g.reshape(1, D_MODEL))
        out_specs = [prompt_tile(D_MODEL), sample_tile(D_MODEL)]
        out_shape = [jax.ShapeDtypeStruct((N_PROMPT, D_MODEL), F32), jax.ShapeDtypeStruct((N_SAMPLE, D_MODEL), F32)]
    else:
        out_specs = tile
        out_shape = jax.ShapeDtypeStruct((N_TOKENS, D_MODEL), F32)
    return pl.pallas_call(
        functools.partial(_ffn_kernel, split_in=split_in, has_proj=proj is not None, final_norm=final_g is not None),
        grid=(N_TOKENS // TOKEN_TILE,),
        in_specs=in_specs,
        out_specs=out_specs,
        out_shape=out_shape,
        compiler_params=pltpu.CompilerParams(dimension_semantics=("arbitrary",), vmem_limit_bytes=VMEM_LIMIT),
        name="ffn",
    )(*args)


def _norm_proj_kernel(*refs, n_out):
    x_ref, g_ref = refs[0], refs[1]
    w_refs = refs[2:2 + n_out]
    out_refs = refs[2 + n_out:]
    h = _rms(x_ref[...], g_ref[...]).astype(BF16)
    for w_ref, out_ref in zip(w_refs, out_refs):
        out_ref[...] = jnp.dot(h, w_ref[...], preferred_element_type=F32)


def _norm_proj(x, g, weights, tile_rows):
    in_specs = [pl.BlockSpec((tile_rows, D_MODEL), lambda i: (i, 0)), _const_spec((1, D_MODEL))]
    in_specs += [_const_spec(w.shape) for w in weights]
    return pl.pallas_call(
        functools.partial(_norm_proj_kernel, n_out=len(weights)),
        grid=(N_TOKENS // tile_rows,),
        in_specs=in_specs,
        out_specs=[pl.BlockSpec((tile_rows, w.shape[1]), lambda i: (i, 0)) for w in weights],
        out_shape=[jax.ShapeDtypeStruct((N_TOKENS, w.shape[1]), F32) for w in weights],
        compiler_params=pltpu.CompilerParams(dimension_semantics=("arbitrary",), vmem_limit_bytes=VMEM_LIMIT),
        name="norm_proj",
    )(x, g.reshape(1, D_MODEL), *weights)


def _rope_tables(pos, width):
    half = ROT_DIM // 2
    inv_freq = ROPE_THETA ** (-jnp.arange(0, ROT_DIM, 2, dtype=F32) / ROT_DIM)
    ang = pos.astype(F32)[:, None] * inv_freq[None, :]
    cos, sin = jnp.cos(ang), jnp.sin(ang)
    n = pos.shape[0]
    rest = SWA_HEAD_DIM - ROT_DIM
    c = jnp.concatenate([cos, cos, jnp.ones((n, rest), F32)], axis=1)
    s_lo = jnp.concatenate([-sin, jnp.zeros((n, half + rest), F32)], axis=1)
    s_hi = jnp.concatenate([jnp.zeros((n, half), F32), sin, jnp.zeros((n, rest), F32)], axis=1)
    reps = width // SWA_HEAD_DIM
    return tuple(jnp.tile(t, (1, reps)) for t in (c, s_lo, s_hi))


def _rope(x, c, s_lo, s_hi):
    half = ROT_DIM // 2
    outs = []
    for j in range(x.shape[1] // LANES):
        xs = x[:, j * LANES:(j + 1) * LANES]
        outs.append(xs * c + pltpu.roll(xs, LANES - half, 1) * s_lo + pltpu.roll(xs, half, 1) * s_hi)
    return outs[0] if len(outs) == 1 else jnp.concatenate(outs, axis=1)


def _swa_prompt_kernel(sink_ref, q_ref, k_ref, v_ref, c_ref, slo_ref, shi_ref,
                       o_ref, kc_ref, vc_ref, kprev, vprev):
    n = pl.program_id(1)

    @pl.when(n == 0)
    def _():
        kprev[...] = jnp.zeros_like(kprev)
        vprev[...] = jnp.zeros_like(vprev)

    c, s_lo, s_hi = c_ref[...], slo_ref[...], shi_ref[...]
    q = _rope(q_ref[...], c, s_lo, s_hi) * (SWA_HEAD_DIM ** -0.5 * LOG2_E)
    k_cur = _rope(k_ref[...], c, s_lo, s_hi)
    v_cur = v_ref[...]
    slot0 = lax.broadcasted_iota(jnp.int32, (2 * WINDOW, 1), 0) == 0
    keys = jnp.where(slot0, 0.0, jnp.concatenate([kprev[...], k_cur], axis=0))
    vals = jnp.where(slot0, 0.0, jnp.concatenate([vprev[...], v_cur], axis=0))
    qi = lax.broadcasted_iota(jnp.int32, (WINDOW, 2 * WINDOW), 0)
    kj = lax.broadcasted_iota(jnp.int32, (WINDOW, 2 * WINDOW), 1)
    low = jnp.maximum(qi, jnp.where(n > 0, 0, WINDOW - 1))
    bias_one = jnp.where(jnp.logical_and(kj > low, kj <= qi + WINDOW), 0.0, NEG_INF)
    ones = jnp.ones((2 * WINDOW, LANES), BF16)

    low_half = lax.broadcasted_iota(jnp.int32, (1, LANES), 1) < SWA_HEAD_DIM

    def both_halves(t):
        out = []
        for grp in range(SWA_KV // LANES):
            g = t[:, grp * LANES:(grp + 1) * LANES]
            swapped = pltpu.roll(g, SWA_HEAD_DIM, 1)
            out += [jnp.where(low_half, g, swapped), jnp.where(low_half, swapped, g)]
        return out

    keys2, vals2 = both_halves(keys), both_halves(vals)
    scores = []
    for kv in range(SWA_KV_HEADS):
        heads = range(kv * SWA_GROUP, (kv + 1) * SWA_GROUP)
        parts, bias = [], []
        for h in heads:
            g = q[:, (h // 2) * LANES:(h // 2 + 1) * LANES]
            parts.append(jnp.where(low_half, g, 0.0) if h % 2 == 0 else jnp.where(low_half, 0.0, g))
            bias.append(jnp.where(kj == 0, sink_ref[h] * LOG2_E, bias_one))
        scores.append(_dot_nt(jnp.concatenate(parts, axis=0), keys2[kv])
                      + jnp.concatenate(bias, axis=0))
    outs = []
    for kv in range(SWA_KV_HEADS):
        s = scores[kv]
        p = jnp.exp2(s - jnp.max(s, axis=-1, keepdims=True)).astype(BF16)
        denom = jnp.dot(p, ones, preferred_element_type=F32)
        outs.append(_dot(p, vals2[kv]) / denom)
    for kv in range(SWA_KV_HEADS):
        for pair in range(SWA_GROUP // 2):
            even, odd = (outs[kv][g * WINDOW:(g + 1) * WINDOW] for g in (2 * pair, 2 * pair + 1))
            grp = (kv * SWA_GROUP) // 2 + pair
            o_ref[:, grp * LANES:(grp + 1) * LANES] = jnp.where(low_half, even, odd).astype(o_ref.dtype)

    kprev[...] = k_cur
    vprev[...] = v_cur

    @pl.when(n == pl.num_programs(1) - 1)
    def _():
        kc_ref[0] = k_cur
        vc_ref[0] = v_cur


def _swa_prompt(qkv, sinks, tables):
    nb = SEQ // WINDOW
    q_blocks = SWA_Q // SWA_KV
    tab = pl.BlockSpec((WINDOW, LANES), lambda b, n, *_: (n, 0))
    cache = pl.BlockSpec((1, WINDOW, SWA_KV), lambda b, n, *_: (b, 0, 0))
    return pl.pallas_call(
        _swa_prompt_kernel,
        grid_spec=pltpu.PrefetchScalarGridSpec(
            num_scalar_prefetch=1,
            grid=(BATCH, nb),
            in_specs=[pl.BlockSpec((WINDOW, SWA_Q), lambda b, n, *_: (b * nb + n, 0)),
                      pl.BlockSpec((WINDOW, SWA_KV), lambda b, n, *_: (b * nb + n, q_blocks)),
                      pl.BlockSpec((WINDOW, SWA_KV), lambda b, n, *_: (b * nb + n, q_blocks + 1)),
                      tab, tab, tab],
            out_specs=[pl.BlockSpec((WINDOW, SWA_Q), lambda b, n, *_: (b * nb + n, 0)), cache, cache],
            scratch_shapes=[pltpu.VMEM((WINDOW, SWA_KV), F32), pltpu.VMEM((WINDOW, SWA_KV), F32)]),
        out_shape=[jax.ShapeDtypeStruct((N_PROMPT, SWA_Q), BF16),
                   jax.ShapeDtypeStruct((BATCH, WINDOW, SWA_KV), F32),
                   jax.ShapeDtypeStruct((BATCH, WINDOW, SWA_KV), F32)],
        compiler_params=pltpu.CompilerParams(dimension_semantics=("arbitrary", "arbitrary")),
        name="swa_prompt",
    )(sinks, qkv, qkv, qkv, *tables)


def _swa_sample_kernel(q_ref, kn_ref, vn_ref, kt_ref, vt_ref, sink_ref,
                       qc_ref, qlo_ref, qhi_ref, kc_tab, klo_tab, khi_tab,
                       o_ref, kto_ref, vto_ref, knew, vnew):
    rows = SWA_KV_HEADS * DEC_SEQ * SWA_GROUP
    token = lambda w: jnp.bitwise_and(jnp.right_shift(lax.broadcasted_iota(jnp.int32, (rows, w), 0), 2), DEC_SEQ - 1)
    valid_old = lax.broadcasted_iota(jnp.int32, (rows, WINDOW), 1) > token(WINDOW)
    valid_new = lax.broadcasted_iota(jnp.int32, (rows, PAD_CHUNK), 1) <= token(PAD_CHUNK)
    sink = sink_ref[:, 0:1]
    lane_head = jnp.right_shift(lax.broadcasted_iota(jnp.int32, (DEC_SEQ * SWA_GROUP, SWA_KV), 1), 6)
    zero_tail = jnp.zeros((PAD_CHUNK - DEC_SEQ, SWA_KV), F32)
    slot = lax.broadcasted_iota(jnp.int32, (SWA_KV, WINDOW), 1)
    place = (lax.broadcasted_iota(jnp.int32, (PAD_CHUNK, WINDOW), 1)
             == lax.broadcasted_iota(jnp.int32, (PAD_CHUNK, WINDOW), 0) + (WINDOW - DEC_SEQ)).astype(F32)

    def shifted(old_t, new_rows):
        appended = lax.dot_general(new_rows, place, (((0,), (0,)), ((), ())), preferred_element_type=F32,
                                   precision=lax.Precision.HIGHEST)
        return jnp.where(slot >= WINDOW - DEC_SEQ, appended, pltpu.roll(old_t, WINDOW - DEC_SEQ, 1))

    for b in range(SWA_SAMPLE_BLOCK):
        knew[0:DEC_SEQ, :] = _rope(kn_ref[b], kc_tab[...], klo_tab[...], khi_tab[...])
        vnew[0:DEC_SEQ, :] = vn_ref[b]
        knew[DEC_SEQ:PAD_CHUNK, :] = zero_tail
        vnew[DEC_SEQ:PAD_CHUNK, :] = zero_tail
        k_new, v_new = knew[...], vnew[...]
        k_old, v_old = kt_ref[b], vt_ref[b]

        q = _rope(q_ref[b], qc_ref[...], qlo_ref[...], qhi_ref[...]) * (SWA_HEAD_DIM ** -0.5)
        s_old = jnp.where(valid_old, _dot(q, k_old), NEG_INF)
        s_new = jnp.where(valid_new, _dot_nt(q, k_new), NEG_INF)
        m = jnp.maximum(jnp.maximum(jnp.max(s_old, axis=-1, keepdims=True),
                                    jnp.max(s_new, axis=-1, keepdims=True)), sink)
        p_old = jnp.exp(s_old - m)
        p_new = jnp.exp(s_new - m)
        denom = (jnp.sum(p_old, axis=-1, keepdims=True) + jnp.sum(p_new, axis=-1, keepdims=True)
                 + jnp.exp(sink - m))
        o_all = (_dot_nt(p_old, v_old) + _dot(p_new, v_new)) / denom
        per_head = DEC_SEQ * SWA_GROUP
        o = jnp.zeros((per_head, SWA_KV), F32)
        for h in range(SWA_KV_HEADS):
            o = o + jnp.where(lane_head == h, o_all[h * per_head:(h + 1) * per_head, :], 0.0)
        o_ref[b] = o.astype(o_ref.dtype)
        kto_ref[b] = shifted(k_old, k_new)
        vto_ref[b] = shifted(v_old, v_new)


def _swa_sample(qkv_s, layer, cache_kt, cache_vt, sinks, pos):
    q = qkv_s[:, :SWA_Q].reshape(DEC_BATCH, DEC_SEQ, SWA_KV_HEADS, SWA_GROUP, SWA_HEAD_DIM)
    q = jnp.transpose(q, (0, 2, 1, 3, 4))
    eye = jnp.eye(SWA_KV_HEADS, dtype=F32)
    rows = SWA_KV_HEADS * DEC_SEQ * SWA_GROUP
    q_bd = (q[:, :, :, :, None, :] * eye[None, :, None, None, :, None]).reshape(DEC_BATCH, rows, SWA_KV)
    k_new = qkv_s[:, SWA_Q:SWA_Q + SWA_KV].reshape(DEC_BATCH, DEC_SEQ, SWA_KV)
    v_new = qkv_s[:, SWA_Q + SWA_KV:].reshape(DEC_BATCH, DEC_SEQ, SWA_KV)
    row_t = (jnp.arange(rows) // SWA_GROUP) % DEC_SEQ
    q_tabs = _rope_tables(pos[row_t], LANES)
    k_tabs = _rope_tables(pos, LANES)
    row_head = (jnp.arange(rows) // (DEC_SEQ * SWA_GROUP)) * SWA_GROUP + jnp.arange(rows) % SWA_GROUP
    sink_rows = jnp.broadcast_to(sinks[row_head][:, None], (rows, LANES))

    nblk = SWA_SAMPLE_BLOCK
    blk = lambda r, c: pl.BlockSpec((nblk, r, c), lambda i: (i, 0, 0))
    window_in = pl.BlockSpec((None, nblk, SWA_KV, WINDOW), lambda i: (layer, i, 0, 0))
    per_head = DEC_SEQ * SWA_GROUP
    o, ko, vo = pl.pallas_call(
        _swa_sample_kernel,
        grid=(DEC_BATCH // nblk,),
        in_specs=[blk(rows, SWA_KV), blk(DEC_SEQ, SWA_KV), blk(DEC_SEQ, SWA_KV),
                  window_in, window_in, _const_spec((rows, LANES)),
                  _const_spec((rows, LANES)), _const_spec((rows, LANES)), _const_spec((rows, LANES)),
                  _const_spec((DEC_SEQ, LANES)), _const_spec((DEC_SEQ, LANES)), _const_spec((DEC_SEQ, LANES))],
        out_specs=[blk(per_head, SWA_KV), blk(SWA_KV, WINDOW), blk(SWA_KV, WINDOW)],
        out_shape=[jax.ShapeDtypeStruct((DEC_BATCH, per_head, SWA_KV), BF16),
                   jax.ShapeDtypeStruct((DEC_BATCH, SWA_KV, WINDOW), F32),
                   jax.ShapeDtypeStruct((DEC_BATCH, SWA_KV, WINDOW), F32)],
        scratch_shapes=[pltpu.VMEM((PAD_CHUNK, SWA_KV), F32), pltpu.VMEM((PAD_CHUNK, SWA_KV), F32)],
        compiler_params=pltpu.CompilerParams(dimension_semantics=("arbitrary",)),
        name="swa_sample",
    )(q_bd, k_new, v_new, cache_kt, cache_vt, sink_rows, *q_tabs, *k_tabs)
    o = o.reshape(DEC_BATCH, DEC_SEQ, SWA_GROUP, SWA_KV_HEADS, SWA_HEAD_DIM)
    o = jnp.transpose(o, (0, 1, 3, 2, 4)).reshape(N_SAMPLE, SWA_Q)
    return o, ko, vo


def _mixer_out_dtype(chunk):
    return BF16 if chunk % (2 * SUBLANES) == 0 else F32


def _tri(n, strict=False):
    i = lax.broadcasted_iota(jnp.int32, (n, n), 0)
    j = lax.broadcasted_iota(jnp.int32, (n, n), 1)
    return (j < i) if strict else (j <= i)


def _gdn_kernel(*refs, chunk, n_valid, has_state):
    it = iter(refs)
    x_ref, z_ref, b_ref, a_ref = next(it), next(it), next(it), next(it)
    cw_ref, alog_ref, dtb_ref, ng_ref = next(it), next(it), next(it), next(it)
    if has_state:
        s0_ref, c0_ref = next(it), next(it)
    o_ref, s_ref, tail_ref = next(it), next(it), next(it)
    prev = next(it)
    c = pl.program_id(1)

    @pl.when(c == 0)
    def _():
        if has_state:
            s_ref[...] = s0_ref[...]
            prev[0:SUBLANES - GDN_CONV + 1, :] = jnp.zeros((SUBLANES - GDN_CONV + 1, GDN_CONV_DIM), F32)
            prev[SUBLANES - GDN_CONV + 1:SUBLANES, :] = c0_ref[0]
        else:
            s_ref[...] = jnp.zeros_like(s_ref)
            prev[...] = jnp.zeros_like(prev)

    x = x_ref[...]
    full = jnp.concatenate([prev[...], x], axis=0)
    acc = x * cw_ref[GDN_CONV - 1:GDN_CONV, :]
    for tap in range(GDN_CONV - 1):
        back = GDN_CONV - 1 - tap
        acc = acc + pltpu.roll(full, back, 0)[SUBLANES:, :] * cw_ref[tap:tap + 1, :]
    prev[...] = x[chunk - SUBLANES:chunk, :]
    tail_ref[0] = x[chunk - SUBLANES:chunk, :]
    xc = _silu(acc)

    row = lax.broadcasted_iota(jnp.int32, (chunk, LANES), 0)
    live = row < n_valid
    beta = jnp.where(live, _sigmoid(b_ref[...]), 0.0)
    a = a_ref[...] + dtb_ref[...]
    softplus = jnp.maximum(a, 0.0) + jnp.log(1.0 + jnp.exp(-jnp.abs(a)))
    g = jnp.where(live, -jnp.exp(alog_ref[...]) * softplus, 0.0) * LOG2_E

    incl = _tri(chunk)
    strict = _tri(chunk, strict=True)
    incl_f = incl.astype(F32)
    gcum = _dot_f32(incl_f, g)
    upper_f = (lax.broadcasted_iota(jnp.int32, (chunk, chunk), 0)
               <= lax.broadcasted_iota(jnp.int32, (chunk, chunk), 1)).astype(F32)
    gcum_t = lax.dot_general(g, upper_f, (((0,), (0,)), ((), ())), preferred_element_type=F32,
                             precision=lax.Precision.HIGHEST)
    eye = (lax.broadcasted_iota(jnp.int32, (chunk, chunk), 0)
           == lax.broadcasted_iota(jnp.int32, (chunk, chunk), 1)).astype(F32)
    n_pow = max(2, math.ceil(math.log2(min(chunk, n_valid))))
    ng = ng_ref[...]
    rep = GDN_V_HEADS // GDN_QK_HEADS
    heads = range(GDN_V_HEADS)

    qn, kn, kk, qk, ks, qs = [], [], [], [], [], []
    for hq in range(GDN_QK_HEADS):
        qh = xc[:, hq * GDN_HEAD_DIM:(hq + 1) * GDN_HEAD_DIM]
        kh = xc[:, GDN_KEY_DIM + hq * GDN_HEAD_DIM:GDN_KEY_DIM + (hq + 1) * GDN_HEAD_DIM]
        qn.append(qh * lax.rsqrt(jnp.sum(qh * qh, axis=-1, keepdims=True) + NORM_EPS) * (GDN_HEAD_DIM ** -0.5))
        kn.append(kh * lax.rsqrt(jnp.sum(kh * kh, axis=-1, keepdims=True) + NORM_EPS))
        kq = jnp.concatenate([kn[hq], qn[hq]], axis=0)
        scores = _dot_nt(kq, kn[hq])
        kk.append(jnp.where(strict, scores[:chunk], 0.0))
        qk.append(scores[chunk:])
        s_pair = jnp.concatenate([s_ref[0, hq * rep + r] for r in range(rep)], axis=1)
        against_state = _dot(kq, s_pair)
        for r in range(rep):
            ks.append(against_state[:chunk, r * GDN_HEAD_DIM:(r + 1) * GDN_HEAD_DIM])
            qs.append(against_state[chunk:, r * GDN_HEAD_DIM:(r + 1) * GDN_HEAD_DIM])

    gc = [gcum[:, h:h + 1] for h in heads]
    bc = [beta[:, h:h + 1] for h in heads]
    decay = [jnp.exp2(jnp.where(incl, gc[h] - gcum_t[h:h + 1, :], NEG_INF)) for h in heads]
    q_pow = [kk[h // rep] * decay[h] * (-bc[h]) for h in heads]
    t_mat = [eye + q_pow[h] for h in heads]
    q_pow = [_dot(q_pow[h], q_pow[h]) for h in heads]
    for _ in range(n_pow - 2):
        both = [_dot(jnp.concatenate([t_mat[h], q_pow[h]], axis=0), q_pow[h]) for h in heads]
        t_mat = [t_mat[h] + both[h][:chunk] for h in heads]
        q_pow = [both[h][chunk:] for h in heads]
    t_mat = [t_mat[h] + _dot(t_mat[h], q_pow[h]) for h in heads]

    egc = [jnp.exp2(gc[h]) for h in heads]
    v_new = []
    for h in heads:
        vh = xc[:, 2 * GDN_KEY_DIM + h * GDN_HEAD_DIM:2 * GDN_KEY_DIM + (h + 1) * GDN_HEAD_DIM]
        v_new.append(_dot(t_mat[h], bc[h] * (vh - egc[h] * ks[h])))
    for h in heads:
        o = egc[h] * qs[h] + _dot(qk[h // rep] * decay[h], v_new[h])
        g_last = gcum[chunk - 1:chunk, h:h + 1]
        s_ref[0, h] = (s_ref[0, h] * jnp.exp2(g_last)
                       + _dot_tn(kn[h // rep] * jnp.exp2(g_last - gc[h]), v_new[h]))
        zh = z_ref[:, h * GDN_HEAD_DIM:(h + 1) * GDN_HEAD_DIM]
        o_ref[:, h * GDN_HEAD_DIM:(h + 1) * GDN_HEAD_DIM] = (_rms(o, ng) * _silu(zh)).astype(o_ref.dtype)


def _gdn(x, z, b, a, conv_w, a_log, dt_bias, norm_g, n_seq, n_chunks, chunk, n_valid, state=None, conv0=None):
    has_state = state is not None
    rows = n_seq * n_chunks * chunk
    blk = lambda w: pl.BlockSpec((chunk, w), lambda s, c: (s * n_chunks + c, 0))
    pad_heads = lambda v: jnp.pad(v.astype(F32), (0, LANES - GDN_V_HEADS)).reshape(1, LANES)
    state_spec = pl.BlockSpec((1, GDN_V_HEADS, GDN_HEAD_DIM, GDN_HEAD_DIM), lambda s, c: (s, 0, 0, 0))
    in_specs = [blk(GDN_CONV_DIM), blk(GDN_V_DIM), blk(LANES), blk(LANES),
                _const_spec((GDN_CONV, GDN_CONV_DIM)), _const_spec((1, LANES)), _const_spec((1, LANES)),
                _const_spec((1, GDN_HEAD_DIM))]
    args = [x, z, b, a, conv_w, pad_heads(a_log), pad_heads(dt_bias), norm_g.reshape(1, GDN_HEAD_DIM)]
    if has_state:
        in_specs += [state_spec, pl.BlockSpec((1, GDN_CONV - 1, GDN_CONV_DIM), lambda s, c: (s, 0, 0))]
        args += [state, conv0]
    return pl.pallas_call(
        functools.partial(_gdn_kernel, chunk=chunk, n_valid=n_valid, has_state=has_state),
        grid=(n_seq, n_chunks),
        in_specs=in_specs,
        out_specs=[blk(GDN_V_DIM), state_spec,
                   pl.BlockSpec((1, SUBLANES, GDN_CONV_DIM), lambda s, c: (s, 0, 0))],
        out_shape=[jax.ShapeDtypeStruct((rows, GDN_V_DIM), _mixer_out_dtype(chunk)),
                   jax.ShapeDtypeStruct((n_seq, GDN_V_HEADS, GDN_HEAD_DIM, GDN_HEAD_DIM), F32),
                   jax.ShapeDtypeStruct((n_seq, SUBLANES, GDN_CONV_DIM), F32)],
        scratch_shapes=[pltpu.VMEM((SUBLANES, GDN_CONV_DIM), F32)],
        compiler_params=pltpu.CompilerParams(dimension_semantics=("arbitrary", "arbitrary"),
                                             vmem_limit_bytes=VMEM_LIMIT),
        name="gdn",
    )(*args)


def _hgrn_kernel(*refs, chunk, n_valid, has_state, layer):
    it = iter(refs)
    x_ref, lbl_ref, ng_ref = next(it), next(it), next(it)
    if has_state:
        s0_ref = next(it)
    o_ref, s_ref = next(it), next(it)
    st = next(it)
    c = pl.program_id(1)

    @pl.when(c == 0)
    def _():
        for h in range(HG_HEADS):
            st[h] = s0_ref[0, h].T if has_state else jnp.zeros((HG_HEAD_DIM, HG_HEAD_DIM), F32)

    logits = [lbl_ref[i:i + 1, :] for i in range(DEPTH)]
    mx = functools.reduce(jnp.maximum, logits)
    ex = [jnp.exp(l - mx) for l in logits]
    tot = functools.reduce(lambda u, v: u + v, ex)
    probs = [e / tot for e in ex]
    lb = functools.reduce(lambda u, v: u + v, probs[:layer + 1]) - probs[0]

    row = lax.broadcasted_iota(jnp.int32, (chunk, HG_DIM), 0)
    live = row < n_valid
    fz = x_ref[:, HG_DIM:2 * HG_DIM]
    log_f = jnp.where(live, jnp.log(lb + (1.0 - lb) * _sigmoid(fz)), 0.0)
    k_all = jnp.where(live, (1.0 - lb) * _sigmoid(-fz), 0.0)
    q_all = _silu(x_ref[:, 0:HG_DIM])
    incl_f = _tri(chunk).astype(F32)
    b2_all = _dot_f32(incl_f, log_f) * LOG2_E
    ones_l = jnp.ones((LANES, LANES), BF16)
    ng = ng_ref[...]
    n_blk = chunk // SUBLANES
    sub = lax.broadcasted_iota(jnp.int32, (SUBLANES, LANES), 0)

    rows8 = lambda t, bi: t[bi * SUBLANES:(bi + 1) * SUBLANES, :]
    heads = range(HG_HEADS)
    head_cols = [slice(h * HG_HEAD_DIM, (h + 1) * HG_HEAD_DIM) for h in heads]
    q_h = [q_all[:, cs] for cs in head_cols]
    k_h = [k_all[:, cs] for cs in head_cols]
    b_h = [b2_all[:, cs] for cs in head_cols]
    v_h = [x_ref[:, 2 * HG_DIM + h * HG_HEAD_DIM:2 * HG_DIM + (h + 1) * HG_HEAD_DIM] for h in heads]
    seg = [0]
    for bi in range(1, n_blk):
        seg.append(seg[-1] + bi * SUBLANES)

    o_inter, w_sum, p_all = [], [], []
    for h in heads:
        q, k, b2 = q_h[h], k_h[h], b_h[h]
        o_inter.append(_dot_nt(q * jnp.exp2(b2), st[h]))
        tiles = []
        for bi in range(n_blk):
            for jj in range(SUBLANES):
                j = bi * SUBLANES + jj
                e = jnp.where(sub >= jj, rows8(b2, bi) - b2[j:j + 1, :], NEG_INF)
                tiles.append(rows8(q, bi) * k[j:j + 1, :] * jnp.exp2(e))
        w_sum.append(jnp.dot(jnp.concatenate(tiles, axis=0).astype(BF16), ones_l, preferred_element_type=F32))
        if n_blk > 1:
            q_t, k_t = [], []
            for bi in range(1, n_blk):
                r = b2[bi * SUBLANES - 1:bi * SUBLANES, :]
                q_t.append(rows8(q, bi) * jnp.exp2(rows8(b2, bi) - r))
                k_t.append(k[0:bi * SUBLANES, :] * jnp.exp2(r - b2[0:bi * SUBLANES, :]))
            p_all.append(_dot_nt(jnp.concatenate(q_t, axis=0), jnp.concatenate(k_t, axis=0)))

    if n_blk > 1:
        prow = jnp.right_shift(lax.broadcasted_iota(jnp.int32, p_all[0].shape, 0), 3)
        pcol = lax.broadcasted_iota(jnp.int32, p_all[0].shape, 1)
        own = functools.reduce(jnp.logical_or, [
            jnp.logical_and(prow == bi - 1, jnp.logical_and(pcol >= seg[bi - 1], pcol < seg[bi]))
            for bi in range(1, n_blk)])
    o_intra = []
    for h in heads:
        v = v_h[h]
        o_blocks = []
        for bi in range(n_blk):
            terms = [rows8(w_sum[h], bi * SUBLANES + jj) * v[bi * SUBLANES + jj:bi * SUBLANES + jj + 1, :]
                     for jj in range(SUBLANES)]
            o_blocks.append(functools.reduce(lambda u, w: u + w, terms))
        if n_blk > 1:
            v_t = jnp.concatenate([v[0:bi * SUBLANES, :] for bi in range(1, n_blk)], axis=0)
            o_off = _dot(jnp.where(own, p_all[h], 0.0), v_t)
            for bi in range(1, n_blk):
                o_blocks[bi] = o_blocks[bi] + rows8(o_off, bi - 1)
        o_intra.append(o_blocks[0] if n_blk == 1 else jnp.concatenate(o_blocks, axis=0))

    for h in heads:
        k, b2 = k_h[h], b_h[h]
        b_last = b2[chunk - 1:chunk, :]
        st[h] = st[h] * jnp.exp2(b_last) + _dot_tn(v_h[h], k * jnp.exp2(b_last - b2))
        gate = x_ref[:, 3 * HG_DIM + h * HG_HEAD_DIM:3 * HG_DIM + (h + 1) * HG_HEAD_DIM]
        o = o_inter[h] + o_intra[h]
        o_ref[:, head_cols[h]] = (_rms(o, ng) * _silu(gate)).astype(o_ref.dtype)

    @pl.when(c == pl.num_programs(1) - 1)
    def _():
        for h in range(HG_HEADS):
            s_ref[0, h] = st[h].T


def _hgrn(x, lb_logits, norm_g, layer, n_seq, n_chunks, chunk, n_valid, state=None):
    has_state = state is not None
    rows = n_seq * n_chunks * chunk
    state_spec = pl.BlockSpec((1, HG_HEADS, HG_HEAD_DIM, HG_HEAD_DIM), lambda s, c: (s, 0, 0, 0))
    in_specs = [pl.BlockSpec((chunk, 4 * HG_DIM), lambda s, c: (s * n_chunks + c, 0)),
                _const_spec((DEPTH, HG_DIM)), _const_spec((1, HG_HEAD_DIM))]
    args = [x, lb_logits, norm_g.reshape(1, HG_HEAD_DIM)]
    if has_state:
        in_specs.append(state_spec)
        args.append(state)
    return pl.pallas_call(
        functools.partial(_hgrn_kernel, chunk=chunk, n_valid=n_valid, has_state=has_state, layer=layer),
        grid=(n_seq, n_chunks),
        in_specs=in_specs,
        out_specs=[pl.BlockSpec((chunk, HG_DIM), lambda s, c: (s * n_chunks + c, 0)), state_spec],
        out_shape=[jax.ShapeDtypeStruct((rows, HG_DIM), _mixer_out_dtype(chunk)),
                   jax.ShapeDtypeStruct((n_seq, HG_HEADS, HG_HEAD_DIM, HG_HEAD_DIM), F32)],
        scratch_shapes=[pltpu.VMEM((HG_HEADS, HG_HEAD_DIM, HG_HEAD_DIM), F32)],
        compiler_params=pltpu.CompilerParams(dimension_semantics=("arbitrary", "arbitrary"),
                                             vmem_limit_bytes=VMEM_LIMIT),
        name="hgrn",
    )(*args)


def _pad_sample(t):
    w = t.shape[1]
    t = jnp.pad(t.reshape(DEC_BATCH, DEC_SEQ, w), ((0, 0), (0, PAD_CHUNK - DEC_SEQ), (0, 0)))
    return t.reshape(DEC_BATCH * PAD_CHUNK, w)


def _unpad_sample(t):
    w = t.shape[1]
    return t.reshape(DEC_BATCH, PAD_CHUNK, w)[:, :DEC_SEQ].reshape(N_SAMPLE, w).astype(BF16)


def _pad_cols(w, n):
    return jnp.pad(w, ((0, 0), (0, n - w.shape[1])))


def kernel(x_prompt, x_sample, cache_swa_k, cache_swa_v, state_gdn, state_gdn_conv, state_hgrn, norm_ffn, ffn_w_in, ffn_w_out, norm_mix, swa_w_in, swa_w_out, swa_sinks, gdn_w_in, gdn_conv_w, gdn_a_log, gdn_dt_bias, gdn_norm, gdn_w_out, hgrn_w_in, hgrn_lb_logits, hgrn_norm, hgrn_w_out, final_norm):
    x = (x_prompt.reshape(N_PROMPT, D_MODEL), x_sample.reshape(N_SAMPLE, D_MODEL))
    ffn = functools.partial(_ffn, norm_ffn=norm_ffn.reshape(DEPTH, 2, 1, D_MODEL),
                            w_in=ffn_w_in.astype(BF16), w_out=ffn_w_out.astype(BF16))
    prompt_tables = _rope_tables(jnp.arange(SEQ), LANES)
    sample_pos = PAST_LEN + jnp.arange(DEC_SEQ)
    window_cols = lambda c: jnp.transpose(c, (0, 1, 3, 4, 2)).reshape(c.shape[0], DEC_BATCH, SWA_KV, WINDOW)
    window_rows = lambda c: jnp.transpose(c.reshape(DEC_BATCH, SWA_KV_HEADS, SWA_HEAD_DIM, WINDOW), (0, 3, 1, 2))
    cache_kt, cache_vt = window_cols(cache_swa_k), window_cols(cache_swa_v)

    outs = {k: [] for k in ("pk", "pv", "pg", "pc", "ph", "sk", "sv", "sg", "sc", "sh")}
    proj = None
    for i in range(DEPTH):
        kind, j = i % N_MIXERS, i // N_MIXERS
        if proj is not None:
            x = ffn(x, i - 1, 1, proj=proj)
        x = ffn(x, i, 0)
        if kind == 0:
            (qkv,) = _norm_proj(x, norm_mix[i], [swa_w_in[j].astype(BF16)], TOKEN_TILE)
            o_p, pk, pv = _swa_prompt(qkv, swa_sinks[j], prompt_tables)
            o_s, sk, sv = _swa_sample(qkv[N_PROMPT:], j, cache_kt, cache_vt, swa_sinks[j], sample_pos)
            kv_shape = (WINDOW, SWA_KV_HEADS, SWA_HEAD_DIM)
            outs["pk"].append(pk.reshape((BATCH,) + kv_shape))
            outs["pv"].append(pv.reshape((BATCH,) + kv_shape))
            outs["sk"].append(window_rows(sk))
            outs["sv"].append(window_rows(sv))
            proj = (o_p, o_s, swa_w_out[j].astype(BF16))
        elif kind == 1:
            w = gdn_w_in[j].astype(BF16)
            z0 = GDN_CONV_DIM + GDN_V_DIM
            weights = [w[:, :GDN_CONV_DIM], w[:, GDN_CONV_DIM:z0],
                       _pad_cols(w[:, z0:z0 + GDN_V_HEADS], LANES), _pad_cols(w[:, z0 + GDN_V_HEADS:], LANES)]
            qkv, z, b, a = _norm_proj(x, norm_mix[i], weights, TOKEN_TILE // 2)
            o_p, pg, p_tail = _gdn(qkv, z, b, a, gdn_conv_w[j], gdn_a_log[j], gdn_dt_bias[j], gdn_norm[j],
                                   BATCH, SEQ // GDN_CHUNK, GDN_CHUNK, GDN_CHUNK)
            o_s, sg, s_tail = _gdn(*[_pad_sample(t[N_PROMPT:]) for t in (qkv, z, b, a)],
                                   gdn_conv_w[j], gdn_a_log[j], gdn_dt_bias[j], gdn_norm[j],
                                   DEC_BATCH, 1, PAD_CHUNK, DEC_SEQ, state=state_gdn[j], conv0=state_gdn_conv[j])
            outs["pg"].append(pg)
            outs["sg"].append(sg)
            keep = GDN_CONV - 1
            outs["pc"].append(p_tail[:, SUBLANES - keep:])
            outs["sc"].append(s_tail[:, DEC_SEQ - keep:DEC_SEQ])
            proj = (o_p, _unpad_sample(o_s), gdn_w_out[j].astype(BF16))
        else:
            (xin,) = _norm_proj(x, norm_mix[i], [hgrn_w_in[j].astype(BF16)], TOKEN_TILE)
            o_p, ph = _hgrn(xin, hgrn_lb_logits, hgrn_norm[j], i, BATCH, SEQ // HG_CHUNK, HG_CHUNK, HG_CHUNK)
            o_s, sh = _hgrn(_pad_sample(xin[N_PROMPT:]), hgrn_lb_logits, hgrn_norm[j], i,
                            DEC_BATCH, 1, PAD_CHUNK, DEC_SEQ, state=state_hgrn[j])
            outs["ph"].append(ph)
            outs["sh"].append(sh)
            proj = (o_p, _unpad_sample(o_s), hgrn_w_out[j].astype(BF16))
    y_prompt, y_sample = ffn(x, DEPTH - 1, 1, proj=proj, final_g=final_norm)
    y_prompt = y_prompt.reshape(BATCH, SEQ, D_MODEL)
    y_sample = y_sample.reshape(DEC_BATCH, DEC_SEQ, D_MODEL)
    st = lambda k: jnp.stack(outs[k])
    return (y_prompt, y_sample, st("pk"), st("pv"), st("pg"), st("pc"), st("ph"),
            st("sk"), st("sv"), st("sg"), st("sc"), st("sh"))
```

```python
import functools
import math

import jax
import jax.numpy as jnp
from jax import lax
from jax.experimental import pallas as pl
from jax.experimental.pallas import tpu as pltpu

D_MODEL = 1024
BATCH = 8
SEQ = 2048
DEPTH = 4
DEC_BATCH = 128
DEC_SEQ = 4
PAST_LEN = 8192
N_MIXERS = 3

SWA_HEAD_DIM = 64
SWA_HEADS = 16
SWA_KV_HEADS = 4
SWA_GROUP = 4
WINDOW = 128
ROT_DIM = 16
ROPE_THETA = 500000.0
SWA_Q = SWA_HEADS * SWA_HEAD_DIM
SWA_KV = SWA_KV_HEADS * SWA_HEAD_DIM

GDN_HEAD_DIM = 128
GDN_QK_HEADS = 8
GDN_V_HEADS = 16
GDN_KEY_DIM = 1024
GDN_V_DIM = 2048
GDN_CONV_DIM = 4096
GDN_CONV = 4

HG_HEAD_DIM = 128
HG_HEADS = 8
HG_DIM = 1024

D_FF = 2816
NORM_EPS = 1e-6
NEG_INF = -1e30
LOG2_E = 1.4426950408889634

LANES = 128
SUBLANES = 8
TOKEN_TILE = 512
N_PROMPT = BATCH * SEQ
N_SAMPLE = DEC_BATCH * DEC_SEQ
N_TOKENS = N_PROMPT + N_SAMPLE
N_PROMPT_TILES = N_PROMPT // TOKEN_TILE
FFN_CHUNK = 256
VMEM_LIMIT = 56 * 1024 * 1024
GDN_CHUNK = 64
HG_CHUNK = 64
PAD_CHUNK = 8
SWA_SAMPLE_BLOCK = 8
SAMPLE_GROUP = 2

F32 = jnp.float32
BF16 = jnp.bfloat16


def _const_spec(shape):
    nd = len(shape)
    return pl.BlockSpec(shape, lambda *_: (0,) * nd, pipeline_mode=pl.Buffered(1))


def _rms(x, g):
    return x * lax.rsqrt(jnp.mean(x * x, axis=-1, keepdims=True) + NORM_EPS) * g


def _sigmoid(x):
    return 1.0 / (1.0 + jnp.exp(-x))


def _silu(x):
    return x * _sigmoid(x)


def _dot(a, b):
    return jnp.dot(a.astype(BF16), b.astype(BF16), preferred_element_type=F32)


def _dot_nt(a, b):
    return lax.dot_general(a.astype(BF16), b.astype(BF16), (((1,), (1,)), ((), ())),
                           preferred_element_type=F32)


def _dot_tn(a, b):
    return lax.dot_general(a.astype(BF16), b.astype(BF16), (((0,), (0,)), ((), ())),
                           preferred_element_type=F32)


def _dot_f32(a, b):
    return jnp.dot(a, b, preferred_element_type=F32, precision=lax.Precision.HIGHEST)


def _ffn_kernel(*refs, split_in, has_proj, final_norm):
    it = iter(refs)
    is_prompt = pl.program_id(0) < N_PROMPT_TILES
    if split_in:
        xp_ref, xs_ref = next(it), next(it)
        x = jnp.where(is_prompt, xp_ref[...], xs_ref[...])
    else:
        x = next(it)[...]
    if has_proj:
        op_ref, os_ref, wp_ref = next(it), next(it), next(it)
    g_ref, wg_ref, wu_ref, wo_ref = next(it), next(it), next(it), next(it)
    gf_ref = next(it) if final_norm else None
    out_refs = list(it)

    if has_proj:
        o = jnp.where(is_prompt, op_ref[...], os_ref[...])
        x = x + jnp.dot(o, wp_ref[...], preferred_element_type=F32)
    h = _rms(x, g_ref[...]).astype(BF16)
    y = jnp.zeros_like(x)
    for c in range(D_FF // FFN_CHUNK):
        cols = slice(c * FFN_CHUNK, (c + 1) * FFN_CHUNK)
        gate = jnp.dot(h, wg_ref[:, cols].astype(BF16), preferred_element_type=F32)
        up = jnp.dot(h, wu_ref[:, cols].astype(BF16), preferred_element_type=F32)
        act = (_silu(gate) * up).astype(BF16)
        y = y + jnp.dot(act, wo_ref[cols, :].astype(BF16), preferred_element_type=F32)
    x = x + 0.5 * y
    if final_norm:
        x = _rms(x, gf_ref[...])
    if len(out_refs) == 1:
        out_refs[0][...] = x
    else:
        @pl.when(is_prompt)
        def _():
            out_refs[0][...] = x

        @pl.when(jnp.logical_not(is_prompt))
        def _():
            out_refs[1][...] = x


def _ffn(x, layer, which, norm_ffn, w_in, w_out, proj=None, final_g=None):
    tile = pl.BlockSpec((TOKEN_TILE, D_MODEL), lambda i: (i, 0))
    prompt_tile = lambda w: pl.BlockSpec((TOKEN_TILE, w), lambda i: (jnp.minimum(i, N_PROMPT_TILES - 1), 0))
    sample_tile = lambda w: pl.BlockSpec((TOKEN_TILE, w), lambda i: (0, 0))
    split_in = isinstance(x, tuple)
    in_specs = [prompt_tile(D_MODEL), sample_tile(D_MODEL)] if split_in else [tile]
    args = list(x) if split_in else [x]
    if proj is not None:
        o_p, o_s, w_p = proj
        k = w_p.shape[0]
        in_specs += [prompt_tile(k), sample_tile(k), _const_spec((k, D_MODEL))]
        args += [o_p, o_s, w_p]
    once = pl.Buffered(1)
    in_specs += [pl.BlockSpec((None, None, 1, D_MODEL), lambda i: (layer, which, 0, 0), pipeline_mode=once),
                 pl.BlockSpec((None, None, D_MODEL, D_FF), lambda i: (layer, which, 0, 0), pipeline_mode=once),
                 pl.BlockSpec((None, None, D_MODEL, D_FF), lambda i: (layer, which, 0, 1), pipeline_mode=once),
                 pl.BlockSpec((None, None, D_FF, D_MODEL), lambda i: (layer, which, 0, 0), pipeline_mode=once)]
    args += [norm_ffn, w_in, w_in, w_out]
    if final_g is not None:
        in_specs.append(_const_spec((1, D_MODEL)))
        args.append(final_g.reshape(1, D_MODEL))
        out_specs = [prompt_tile(D_MODEL), sample_tile(D_MODEL)]
        out_shape = [jax.ShapeDtypeStruct((N_PROMPT, D_MODEL), F32), jax.ShapeDtypeStruct((N_SAMPLE, D_MODEL), F32)]
    else:
        out_specs = tile
        out_shape = jax.ShapeDtypeStruct((N_TOKENS, D_MODEL), F32)
    return pl.pallas_call(
        functools.partial(_ffn_kernel, split_in=split_in, has_proj=proj is not None, final_norm=final_g is not None),
        grid=(N_TOKENS // TOKEN_TILE,),
        in_specs=in_specs,
        out_specs=out_specs,
        out_shape=out_shape,
        compiler_params=pltpu.CompilerParams(dimension_semantics=("arbitrary",), vmem_limit_bytes=VMEM_LIMIT),
        name="ffn",
    )(*args)


def _norm_proj_kernel(*refs, n_out):
    x_ref, g_ref = refs[0], refs[1]
    w_refs = refs[2:2 + n_out]
    out_refs = refs[2 + n_out:]
    h = _rms(x_ref[...], g_ref[...]).astype(BF16)
    for w_ref, out_ref in zip(w_refs, out_refs):
        out_ref[...] = jnp.dot(h, w_ref[...], preferred_element_type=F32)


def _norm_proj(x, g, weights, tile_rows):
    in_specs = [pl.BlockSpec((tile_rows, D_MODEL), lambda i: (i, 0)), _const_spec((1, D_MODEL))]
    in_specs += [_const_spec(w.shape) for w in weights]
    return pl.pallas_call(
        functools.partial(_norm_proj_kernel, n_out=len(weights)),
        grid=(N_TOKENS // tile_rows,),
        in_specs=in_specs,
        out_specs=[pl.BlockSpec((tile_rows, w.shape[1]), lambda i: (i, 0)) for w in weights],
        out_shape=[jax.ShapeDtypeStruct((N_TOKENS, w.shape[1]), F32) for w in weights],
        compiler_params=pltpu.CompilerParams(dimension_semantics=("arbitrary",), vmem_limit_bytes=VMEM_LIMIT),
        name="norm_proj",
    )(x, g.reshape(1, D_MODEL), *weights)


def _rope_tables(pos, width):
    half = ROT_DIM // 2
    inv_freq = ROPE_THETA ** (-jnp.arange(0, ROT_DIM, 2, dtype=F32) / ROT_DIM)
    ang = pos.astype(F32)[:, None] * inv_freq[None, :]
    cos, sin = jnp.cos(ang), jnp.sin(ang)
    n = pos.shape[0]
    rest = SWA_HEAD_DIM - ROT_DIM
    c = jnp.concatenate([cos, cos, jnp.ones((n, rest), F32)], axis=1)
    s_lo = jnp.concatenate([-sin, jnp.zeros((n, half + rest), F32)], axis=1)
    s_hi = jnp.concatenate([jnp.zeros((n, half), F32), sin, jnp.zeros((n, rest), F32)], axis=1)
    reps = width // SWA_HEAD_DIM
    return tuple(jnp.tile(t, (1, reps)) for t in (c, s_lo, s_hi))


def _rope(x, c, s_lo, s_hi):
    half = ROT_DIM // 2
    outs = []
    for j in range(x.shape[1] // LANES):
        xs = x[:, j * LANES:(j + 1) * LANES]
        outs.append(xs * c + pltpu.roll(xs, LANES - half, 1) * s_lo + pltpu.roll(xs, half, 1) * s_hi)
    return outs[0] if len(outs) == 1 else jnp.concatenate(outs, axis=1)


def _swa_prompt_kernel(sink_ref, q_ref, k_ref, v_ref, c_ref, slo_ref, shi_ref,
                       o_ref, kc_ref, vc_ref, kprev, vprev):
    n = pl.program_id(1)

    @pl.when(n == 0)
    def _():
        kprev[...] = jnp.zeros_like(kprev)
        vprev[...] = jnp.zeros_like(vprev)

    c, s_lo, s_hi = c_ref[...], slo_ref[...], shi_ref[...]
    q = _rope(q_ref[...], c, s_lo, s_hi) * (SWA_HEAD_DIM ** -0.5 * LOG2_E)
    k_cur = _rope(k_ref[...], c, s_lo, s_hi)
    v_cur = v_ref[...]
    slot0 = lax.broadcasted_iota(jnp.int32, (2 * WINDOW, 1), 0) == 0
    keys = jnp.where(slot0, 0.0, jnp.concatenate([kprev[...], k_cur], axis=0))
    vals = jnp.where(slot0, 0.0, jnp.concatenate([vprev[...], v_cur], axis=0))
    qi = lax.broadcasted_iota(jnp.int32, (WINDOW, 2 * WINDOW), 0)
    kj = lax.broadcasted_iota(jnp.int32, (WINDOW, 2 * WINDOW), 1)
    low = jnp.maximum(qi, jnp.where(n > 0, 0, WINDOW - 1))
    bias_one = jnp.where(jnp.logical_and(kj > low, kj <= qi + WINDOW), 0.0, NEG_INF)
    ones = jnp.ones((2 * WINDOW, LANES), BF16)

    low_half = lax.broadcasted_iota(jnp.int32, (1, LANES), 1) < SWA_HEAD_DIM

    def both_halves(t):
        out = []
        for grp in range(SWA_KV // LANES):
            g = t[:, grp * LANES:(grp + 1) * LANES]
            swapped = pltpu.roll(g, SWA_HEAD_DIM, 1)
            out += [jnp.where(low_half, g, swapped), jnp.where(low_half, swapped, g)]
        return out

    keys2, vals2 = both_halves(keys), both_halves(vals)
    scores = []
    for kv in range(SWA_KV_HEADS):
        heads = range(kv * SWA_GROUP, (kv + 1) * SWA_GROUP)
        parts, bias = [], []
        for h in heads:
            g = q[:, (h // 2) * LANES:(h // 2 + 1) * LANES]
            parts.append(jnp.where(low_half, g, 0.0) if h % 2 == 0 else jnp.where(low_half, 0.0, g))
            bias.append(jnp.where(kj == 0, sink_ref[h] * LOG2_E, bias_one))
        scores.append(_dot_nt(jnp.concatenate(parts, axis=0), keys2[kv])
                      + jnp.concatenate(bias, axis=0))
    outs = []
    for kv in range(SWA_KV_HEADS):
        s = scores[kv]
        p = jnp.exp2(s - jnp.max(s, axis=-1, keepdims=True)).astype(BF16)
        denom = jnp.dot(p, ones, preferred_element_type=F32)
        outs.append(_dot(p, vals2[kv]) / denom)
    for kv in range(SWA_KV_HEADS):
        for pair in range(SWA_GROUP // 2):
            even, odd = (outs[kv][g * WINDOW:(g + 1) * WINDOW] for g in (2 * pair, 2 * pair + 1))
            grp = (kv * SWA_GROUP) // 2 + pair
            o_ref[:, grp * LANES:(grp + 1) * LANES] = jnp.where(low_half, even, odd).astype(o_ref.dtype)

    kprev[...] = k_cur
    vprev[...] = v_cur

    @pl.when(n == pl.num_programs(1) - 1)
    def _():
        kc_ref[0] = k_cur
        vc_ref[0] = v_cur


def _swa_prompt(qkv, sinks, tables):
    nb = SEQ // WINDOW
    q_blocks = SWA_Q // SWA_KV
    tab = pl.BlockSpec((WINDOW, LANES), lambda b, n, *_: (n, 0))
    cache = pl.BlockSpec((1, WINDOW, SWA_KV), lambda b, n, *_: (b, 0, 0))
    return pl.pallas_call(
        _swa_prompt_kernel,
        grid_spec=pltpu.PrefetchScalarGridSpec(
            num_scalar_prefetch=1,
            grid=(BATCH, nb),
            in_specs=[pl.BlockSpec((WINDOW, SWA_Q), lambda b, n, *_: (b * nb + n, 0)),
                      pl.BlockSpec((WINDOW, SWA_KV), lambda b, n, *_: (b * nb + n, q_blocks)),
                      pl.BlockSpec((WINDOW, SWA_KV), lambda b, n, *_: (b * nb + n, q_blocks + 1)),
                      tab, tab, tab],
            out_specs=[pl.BlockSpec((WINDOW, SWA_Q), lambda b, n, *_: (b * nb + n, 0)), cache, cache],
            scratch_shapes=[pltpu.VMEM((WINDOW, SWA_KV), F32), pltpu.VMEM((WINDOW, SWA_KV), F32)]),
        out_shape=[jax.ShapeDtypeStruct((N_PROMPT, SWA_Q), BF16),
                   jax.ShapeDtypeStruct((BATCH, WINDOW, SWA_KV), F32),
                   jax.ShapeDtypeStruct((BATCH, WINDOW, SWA_KV), F32)],
        compiler_params=pltpu.CompilerParams(dimension_semantics=("arbitrary", "arbitrary")),
        name="swa_prompt",
    )(sinks, qkv, qkv, qkv, *tables)


def _swa_sample_kernel(q_ref, kn_ref, vn_ref, kt_ref, vt_ref, sink_ref,
                       qc_ref, qlo_ref, qhi_ref, kc_tab, klo_tab, khi_tab,
                       o_ref, kto_ref, vto_ref, knew, vnew):
    rows = SWA_KV_HEADS * DEC_SEQ * SWA_GROUP
    token = lambda w: jnp.bitwise_and(jnp.right_shift(lax.broadcasted_iota(jnp.int32, (rows, w), 0), 2), DEC_SEQ - 1)
    valid_old = lax.broadcasted_iota(jnp.int32, (rows, WINDOW), 1) > token(WINDOW)
    valid_new = lax.broadcasted_iota(jnp.int32, (rows, PAD_CHUNK), 1) <= token(PAD_CHUNK)
    sink = sink_ref[:, 0:1]
    lane_head = jnp.right_shift(lax.broadcasted_iota(jnp.int32, (DEC_SEQ * SWA_GROUP, SWA_KV), 1), 6)
    zero_tail = jnp.zeros((PAD_CHUNK - DEC_SEQ, SWA_KV), F32)
    slot = lax.broadcasted_iota(jnp.int32, (SWA_KV, WINDOW), 1)
    place = (lax.broadcasted_iota(jnp.int32, (PAD_CHUNK, WINDOW), 1)
             == lax.broadcasted_iota(jnp.int32, (PAD_CHUNK, WINDOW), 0) + (WINDOW - DEC_SEQ)).astype(F32)

    def shifted(old_t, new_rows):
        appended = lax.dot_general(new_rows, place, (((0,), (0,)), ((), ())), preferred_element_type=F32,
                                   precision=lax.Precision.HIGHEST)
        return jnp.where(slot >= WINDOW - DEC_SEQ, appended, pltpu.roll(old_t, WINDOW - DEC_SEQ, 1))

    for b in range(SWA_SAMPLE_BLOCK):
        knew[0:DEC_SEQ, :] = _rope(kn_ref[b], kc_tab[...], klo_tab[...], khi_tab[...])
        vnew[0:DEC_SEQ, :] = vn_ref[b]
        knew[DEC_SEQ:PAD_CHUNK, :] = zero_tail
        vnew[DEC_SEQ:PAD_CHUNK, :] = zero_tail
        k_new, v_new = knew[...], vnew[...]
        k_old, v_old = kt_ref[b], vt_ref[b]

        q = _rope(q_ref[b], qc_ref[...], qlo_ref[...], qhi_ref[...]) * (SWA_HEAD_DIM ** -0.5)
        s_old = jnp.where(valid_old, _dot(q, k_old), NEG_INF)
        s_new = jnp.where(valid_new, _dot_nt(q, k_new), NEG_INF)
        m = jnp.maximum(jnp.maximum(jnp.max(s_old, axis=-1, keepdims=True),
                                    jnp.max(s_new, axis=-1, keepdims=True)), sink)
        p_old = jnp.exp(s_old - m)
        p_new = jnp.exp(s_new - m)
        denom = (jnp.sum(p_old, axis=-1, keepdims=True) + jnp.sum(p_new, axis=-1, keepdims=True)
                 + jnp.exp(sink - m))
        o_all = (_dot_nt(p_old, v_old) + _dot(p_new, v_new)) / denom
        per_head = DEC_SEQ * SWA_GROUP
        o = jnp.zeros((per_head, SWA_KV), F32)
        for h in range(SWA_KV_HEADS):
            o = o + jnp.where(lane_head == h, o_all[h * per_head:(h + 1) * per_head, :], 0.0)
        o_ref[b] = o.astype(o_ref.dtype)
        kto_ref[b] = shifted(k_old, k_new)
        vto_ref[b] = shifted(v_old, v_new)


def _swa_sample(qkv_s, layer, cache_kt, cache_vt, sinks, pos):
    q = qkv_s[:, :SWA_Q].reshape(DEC_BATCH, DEC_SEQ, SWA_KV_HEADS, SWA_GROUP, SWA_HEAD_DIM)
    q = jnp.transpose(q, (0, 2, 1, 3, 4))
    eye = jnp.eye(SWA_KV_HEADS, dtype=F32)
    rows = SWA_KV_HEADS * DEC_SEQ * SWA_GROUP
    q_bd = (q[:, :, :, :, None, :] * eye[None, :, None, None, :, None]).reshape(DEC_BATCH, rows, SWA_KV)
    k_new = qkv_s[:, SWA_Q:SWA_Q + SWA_KV].reshape(DEC_BATCH, DEC_SEQ, SWA_KV)
    v_new = qkv_s[:, SWA_Q + SWA_KV:].reshape(DEC_BATCH, DEC_SEQ, SWA_KV)
    row_t = (jnp.arange(rows) // SWA_GROUP) % DEC_SEQ
    q_tabs = _rope_tables(pos[row_t], LANES)
    k_tabs = _rope_tables(pos, LANES)
    row_head = (jnp.arange(rows) // (DEC_SEQ * SWA_GROUP)) * SWA_GROUP + jnp.arange(rows) % SWA_GROUP
    sink_rows = jnp.broadcast_to(sinks[row_head][:, None], (rows, LANES))

    nblk = SWA_SAMPLE_BLOCK
    blk = lambda r, c: pl.BlockSpec((nblk, r, c), lambda i: (i, 0, 0))
    window_in = pl.BlockSpec((None, nblk, SWA_KV, WINDOW), lambda i: (layer, i, 0, 0))
    per_head = DEC_SEQ * SWA_GROUP
    o, ko, vo = pl.pallas_call(
        _swa_sample_kernel,
        grid=(DEC_BATCH // nblk,),
        in_specs=[blk(rows, SWA_KV), blk(DEC_SEQ, SWA_KV), blk(DEC_SEQ, SWA_KV),
                  window_in, window_in, _const_spec((rows, LANES)),
                  _const_spec((rows, LANES)), _const_spec((rows, LANES)), _const_spec((rows, LANES)),
                  _const_spec((DEC_SEQ, LANES)), _const_spec((DEC_SEQ, LANES)), _const_spec((DEC_SEQ, LANES))],
        out_specs=[blk(per_head, SWA_KV), blk(SWA_KV, WINDOW), blk(SWA_KV, WINDOW)],
        out_shape=[jax.ShapeDtypeStruct((DEC_BATCH, per_head, SWA_KV), BF16),
                   jax.ShapeDtypeStruct((DEC_BATCH, SWA_KV, WINDOW), F32),
                   jax.ShapeDtypeStruct((DEC_BATCH, SWA_KV, WINDOW), F32)],
        scratch_shapes=[pltpu.VMEM((PAD_CHUNK, SWA_KV), F32), pltpu.VMEM((PAD_CHUNK, SWA_KV), F32)],
        compiler_params=pltpu.CompilerParams(dimension_semantics=("arbitrary",)),
        name="swa_sample",
    )(q_bd, k_new, v_new, cache_kt, cache_vt, sink_rows, *q_tabs, *k_tabs)
    o = o.reshape(DEC_BATCH, DEC_SEQ, SWA_GROUP, SWA_KV_HEADS, SWA_HEAD_DIM)
    o = jnp.transpose(o, (0, 1, 3, 2, 4)).reshape(N_SAMPLE, SWA_Q)
    return o, ko, vo


def _mixer_out_dtype(chunk):
    return BF16 if chunk % (2 * SUBLANES) == 0 else F32


def _tri(n, strict=False):
    i = lax.broadcasted_iota(jnp.int32, (n, n), 0)
    j = lax.broadcasted_iota(jnp.int32, (n, n), 1)
    return (j < i) if strict else (j <= i)


def _gdn_kernel(*refs, chunk, n_valid, has_state, group):
    it = iter(refs)
    x_ref, z_ref, b_ref, a_ref = next(it), next(it), next(it), next(it)
    cw_ref, alog_ref, dtb_ref, ng_ref = next(it), next(it), next(it), next(it)
    if has_state:
        s0_ref, c0_ref = next(it), next(it)
    o_ref, s_ref, tail_ref = next(it), next(it), next(it)
    prev = next(it)
    c = pl.program_id(1)
    seqs = range(group)

    @pl.when(c == 0)
    def _():
        if has_state:
            s_ref[...] = s0_ref[...]
            for s in seqs:
                prev[s, 0:SUBLANES - GDN_CONV + 1, :] = jnp.zeros((SUBLANES - GDN_CONV + 1, GDN_CONV_DIM), F32)
                prev[s, SUBLANES - GDN_CONV + 1:SUBLANES, :] = c0_ref[s]
        else:
            s_ref[...] = jnp.zeros_like(s_ref)
            prev[...] = jnp.zeros_like(prev)

    incl = _tri(chunk)
    strict = _tri(chunk, strict=True)
    incl_f = incl.astype(F32)
    upper_f = (lax.broadcasted_iota(jnp.int32, (chunk, chunk), 0)
               <= lax.broadcasted_iota(jnp.int32, (chunk, chunk), 1)).astype(F32)
    eye = (lax.broadcasted_iota(jnp.int32, (chunk, chunk), 0)
           == lax.broadcasted_iota(jnp.int32, (chunk, chunk), 1)).astype(F32)
    live = lax.broadcasted_iota(jnp.int32, (chunk, LANES), 0) < n_valid
    n_pow = max(2, math.ceil(math.log2(min(chunk, n_valid))))
    ng = ng_ref[...]
    rep = GDN_V_HEADS // GDN_QK_HEADS
    head_cols = lambda base, h: slice(base + h * GDN_HEAD_DIM, base + (h + 1) * GDN_HEAD_DIM)

    xc, beta, gcum, gcum_t = [], [], [], []
    for s in seqs:
        rows = slice(s * chunk, (s + 1) * chunk)
        x = x_ref[rows, :]
        full = jnp.concatenate([prev[s], x], axis=0)
        acc = x * cw_ref[GDN_CONV - 1:GDN_CONV, :]
        for tap in range(GDN_CONV - 1):
            back = GDN_CONV - 1 - tap
            acc = acc + pltpu.roll(full, back, 0)[SUBLANES:, :] * cw_ref[tap:tap + 1, :]
        prev[s] = x[chunk - SUBLANES:chunk, :]
        tail_ref[s] = x[chunk - SUBLANES:chunk, :]
        xc.append(_silu(acc))
        beta.append(jnp.where(live, _sigmoid(b_ref[rows, :]), 0.0))
        a = a_ref[rows, :] + dtb_ref[...]
        softplus = jnp.maximum(a, 0.0) + jnp.log(1.0 + jnp.exp(-jnp.abs(a)))
        g = jnp.where(live, -jnp.exp(alog_ref[...]) * softplus, 0.0) * LOG2_E
        gcum.append(_dot_f32(incl_f, g))
        gcum_t.append(lax.dot_general(g, upper_f, (((0,), (0,)), ((), ())), preferred_element_type=F32,
                                      precision=lax.Precision.HIGHEST))

    pairs = [(s, hq) for s in seqs for hq in range(GDN_QK_HEADS)]
    units = [(s, h) for s in seqs for h in range(GDN_V_HEADS)]
    qn, kn, kk, qk, ks, qs = {}, {}, {}, {}, {}, {}
    for s, hq in pairs:
        qh = xc[s][:, head_cols(0, hq)]
        kh = xc[s][:, head_cols(GDN_KEY_DIM, hq)]
        qn[s, hq] = qh * lax.rsqrt(jnp.sum(qh * qh, axis=-1, keepdims=True) + NORM_EPS) * (GDN_HEAD_DIM ** -0.5)
        kn[s, hq] = kh * lax.rsqrt(jnp.sum(kh * kh, axis=-1, keepdims=True) + NORM_EPS)
        kq = jnp.concatenate([kn[s, hq], qn[s, hq]], axis=0)
        scores = _dot_nt(kq, kn[s, hq])
        kk[s, hq] = jnp.where(strict, scores[:chunk], 0.0)
        qk[s, hq] = scores[chunk:]
        s_pair = jnp.concatenate([s_ref[s, hq * rep + r] for r in range(rep)], axis=1)
        against_state = _dot(kq, s_pair)
        for r in range(rep):
            ks[s, hq * rep + r] = against_state[:chunk, head_cols(0, r)]
            qs[s, hq * rep + r] = against_state[chunk:, head_cols(0, r)]

    gc = {(s, h): gcum[s][:, h:h + 1] for s, h in units}
    bc = {(s, h): beta[s][:, h:h + 1] for s, h in units}
    decay = {(s, h): jnp.exp2(jnp.where(incl, gc[s, h] - gcum_t[s][h:h + 1, :], NEG_INF)) for s, h in units}
    q_pow = {(s, h): kk[s, h // rep] * decay[s, h] * (-bc[s, h]) for s, h in units}
    t_mat = {u: eye + q_pow[u] for u in units}
    q_pow = {u: _dot(q_pow[u], q_pow[u]) for u in units}
    for _ in range(n_pow - 2):
        both = {u: _dot(jnp.concatenate([t_mat[u], q_pow[u]], axis=0), q_pow[u]) for u in units}
        t_mat = {u: t_mat[u] + both[u][:chunk] for u in units}
        q_pow = {u: both[u][chunk:] for u in units}
    t_mat = {u: t_mat[u] + _dot(t_mat[u], q_pow[u]) for u in units}

    egc = {u: jnp.exp2(gc[u]) for u in units}
    v_new = {}
    for s, h in units:
        vh = xc[s][:, head_cols(2 * GDN_KEY_DIM, h)]
        v_new[s, h] = _dot(t_mat[s, h], bc[s, h] * (vh - egc[s, h] * ks[s, h]))
    for s, h in units:
        rows = slice(s * chunk, (s + 1) * chunk)
        o = egc[s, h] * qs[s, h] + _dot(qk[s, h // rep] * decay[s, h], v_new[s, h])
        g_last = gcum[s][chunk - 1:chunk, h:h + 1]
        s_ref[s, h] = (s_ref[s, h] * jnp.exp2(g_last)
                       + _dot_tn(kn[s, h // rep] * jnp.exp2(g_last - gc[s, h]), v_new[s, h]))
        zh = z_ref[rows, head_cols(0, h)]
        o_ref[rows, head_cols(0, h)] = (_rms(o, ng) * _silu(zh)).astype(o_ref.dtype)


def _gdn(x, z, b, a, conv_w, a_log, dt_bias, norm_g, n_seq, n_chunks, chunk, n_valid, state=None, conv0=None):
    has_state = state is not None
    rows = n_seq * n_chunks * chunk
    group = SAMPLE_GROUP if n_chunks == 1 else 1
    blk = lambda w: pl.BlockSpec((group * chunk, w), lambda s, c: (s * n_chunks + c, 0))
    pad_heads = lambda v: jnp.pad(v.astype(F32), (0, LANES - GDN_V_HEADS)).reshape(1, LANES)
    state_spec = pl.BlockSpec((group, GDN_V_HEADS, GDN_HEAD_DIM, GDN_HEAD_DIM), lambda s, c: (s, 0, 0, 0))
    in_specs = [blk(GDN_CONV_DIM), blk(GDN_V_DIM), blk(LANES), blk(LANES),
                _const_spec((GDN_CONV, GDN_CONV_DIM)), _const_spec((1, LANES)), _const_spec((1, LANES)),
                _const_spec((1, GDN_HEAD_DIM))]
    args = [x, z, b, a, conv_w, pad_heads(a_log), pad_heads(dt_bias), norm_g.reshape(1, GDN_HEAD_DIM)]
    if has_state:
        in_specs += [state_spec, pl.BlockSpec((group, GDN_CONV - 1, GDN_CONV_DIM), lambda s, c: (s, 0, 0))]
        args += [state, conv0]
    return pl.pallas_call(
        functools.partial(_gdn_kernel, chunk=chunk, n_valid=n_valid, has_state=has_state, group=group),
        grid=(n_seq // group, n_chunks),
        in_specs=in_specs,
        out_specs=[blk(GDN_V_DIM), state_spec,
                   pl.BlockSpec((group, SUBLANES, GDN_CONV_DIM), lambda s, c: (s, 0, 0))],
        out_shape=[jax.ShapeDtypeStruct((rows, GDN_V_DIM), _mixer_out_dtype(chunk)),
                   jax.ShapeDtypeStruct((n_seq, GDN_V_HEADS, GDN_HEAD_DIM, GDN_HEAD_DIM), F32),
                   jax.ShapeDtypeStruct((n_seq, SUBLANES, GDN_CONV_DIM), F32)],
        scratch_shapes=[pltpu.VMEM((group, SUBLANES, GDN_CONV_DIM), F32)],
        compiler_params=pltpu.CompilerParams(dimension_semantics=("arbitrary", "arbitrary"),
                                             vmem_limit_bytes=VMEM_LIMIT),
        name="gdn",
    )(*args)


def _hgrn_kernel(*refs, chunk, n_valid, has_state, layer, group):
    it = iter(refs)
    x_ref, lbl_ref, ng_ref = next(it), next(it), next(it)
    if has_state:
        s0_ref = next(it)
    o_ref, s_ref = next(it), next(it)
    st = next(it)
    c = pl.program_id(1)

    units = [(s, h) for s in range(group) for h in range(HG_HEADS)]

    @pl.when(c == 0)
    def _():
        for s, h in units:
            st[s, h] = s0_ref[s, h].T if has_state else jnp.zeros((HG_HEAD_DIM, HG_HEAD_DIM), F32)

    logits = [lbl_ref[i:i + 1, :] for i in range(DEPTH)]
    mx = functools.reduce(jnp.maximum, logits)
    ex = [jnp.exp(l - mx) for l in logits]
    tot = functools.reduce(lambda u, v: u + v, ex)
    probs = [e / tot for e in ex]
    lb = functools.reduce(lambda u, v: u + v, probs[:layer + 1]) - probs[0]

    total = group * chunk
    row = lax.broadcasted_iota(jnp.int32, (total, HG_DIM), 0)
    live = jnp.bitwise_and(row, chunk - 1) < n_valid
    fz = x_ref[:, HG_DIM:2 * HG_DIM]
    log_f = jnp.where(live, jnp.log(lb + (1.0 - lb) * _sigmoid(fz)), 0.0)
    k_all = jnp.where(live, (1.0 - lb) * _sigmoid(-fz), 0.0)
    q_all = _silu(x_ref[:, 0:HG_DIM])
    ri = lax.broadcasted_iota(jnp.int32, (total, total), 0)
    rj = lax.broadcasted_iota(jnp.int32, (total, total), 1)
    incl_f = jnp.logical_and(rj <= ri, rj >= ri - jnp.bitwise_and(ri, chunk - 1)).astype(F32)
    b2_all = _dot_f32(incl_f, log_f) * LOG2_E
    ones_l = jnp.ones((LANES, LANES), BF16)
    ng = ng_ref[...]
    n_blk = chunk // SUBLANES
    sub = lax.broadcasted_iota(jnp.int32, (SUBLANES, LANES), 0)

    rows8 = lambda t, bi: t[bi * SUBLANES:(bi + 1) * SUBLANES, :]
    seq_rows = lambda s: slice(s * chunk, (s + 1) * chunk)
    head_cols = lambda base, h: slice(base + h * HG_HEAD_DIM, base + (h + 1) * HG_HEAD_DIM)
    q_h = {(s, h): q_all[seq_rows(s), head_cols(0, h)] for s, h in units}
    k_h = {(s, h): k_all[seq_rows(s), head_cols(0, h)] for s, h in units}
    b_h = {(s, h): b2_all[seq_rows(s), head_cols(0, h)] for s, h in units}
    v_h = {(s, h): x_ref[seq_rows(s), head_cols(2 * HG_DIM, h)] for s, h in units}
    seg = [0]
    for bi in range(1, n_blk):
        seg.append(seg[-1] + bi * SUBLANES)

    o_inter, w_sum, p_all = {}, {}, {}
    for u in units:
        q, k, b2 = q_h[u], k_h[u], b_h[u]
        o_inter[u] = _dot_nt(q * jnp.exp2(b2), st[u])
        tiles = []
        for bi in range(n_blk):
            for jj in range(SUBLANES):
                j = bi * SUBLANES + jj
                e = jnp.where(sub >= jj, rows8(b2, bi) - b2[j:j + 1, :], NEG_INF)
                tiles.append(rows8(q, bi) * k[j:j + 1, :] * jnp.exp2(e))
        w_sum[u] = jnp.dot(jnp.concatenate(tiles, axis=0).astype(BF16), ones_l, preferred_element_type=F32)
        if n_blk > 1:
            q_t, k_t = [], []
            for bi in range(1, n_blk):
                r = b2[bi * SUBLANES - 1:bi * SUBLANES, :]
                q_t.append(rows8(q, bi) * jnp.exp2(rows8(b2, bi) - r))
                k_t.append(k[0:bi * SUBLANES, :] * jnp.exp2(r - b2[0:bi * SUBLANES, :]))
            p_all[u] = _dot_nt(jnp.concatenate(q_t, axis=0), jnp.concatenate(k_t, axis=0))

    if n_blk > 1:
        p_shape = (SUBLANES * (n_blk - 1), seg[-1])
        prow = jnp.right_shift(lax.broadcasted_iota(jnp.int32, p_shape, 0), 3)
        pcol = lax.broadcasted_iota(jnp.int32, p_shape, 1)
        own = functools.reduce(jnp.logical_or, [
            jnp.logical_and(prow == bi - 1, jnp.logical_and(pcol >= seg[bi - 1], pcol < seg[bi]))
            for bi in range(1, n_blk)])
    o_intra = {}
    for u in units:
        v = v_h[u]
        o_blocks = []
        for bi in range(n_blk):
            terms = [rows8(w_sum[u], bi * SUBLANES + jj) * v[bi * SUBLANES + jj:bi * SUBLANES + jj + 1, :]
                     for jj in range(SUBLANES)]
            o_blocks.append(functools.reduce(lambda a, w: a + w, terms))
        if n_blk > 1:
            v_t = jnp.concatenate([v[0:bi * SUBLANES, :] for bi in range(1, n_blk)], axis=0)
            o_off = _dot(jnp.where(own, p_all[u], 0.0), v_t)
            for bi in range(1, n_blk):
                o_blocks[bi] = o_blocks[bi] + rows8(o_off, bi - 1)
        o_intra[u] = o_blocks[0] if n_blk == 1 else jnp.concatenate(o_blocks, axis=0)

    for u in units:
        s, h = u
        k, b2 = k_h[u], b_h[u]
        b_last = b2[chunk - 1:chunk, :]
        st[u] = st[u] * jnp.exp2(b_last) + _dot_tn(v_h[u], k * jnp.exp2(b_last - b2))
        gate = x_ref[seq_rows(s), head_cols(3 * HG_DIM, h)]
        o = o_inter[u] + o_intra[u]
        o_ref[seq_rows(s), head_cols(0, h)] = (_rms(o, ng) * _silu(gate)).astype(o_ref.dtype)

    @pl.when(c == pl.num_programs(1) - 1)
    def _():
        for s, h in units:
            s_ref[s, h] = st[s, h].T


def _hgrn(x, lb_logits, norm_g, layer, n_seq, n_chunks, chunk, n_valid, state=None):
    has_state = state is not None
    rows = n_seq * n_chunks * chunk
    group = SAMPLE_GROUP if n_chunks == 1 else 1
    state_spec = pl.BlockSpec((group, HG_HEADS, HG_HEAD_DIM, HG_HEAD_DIM), lambda s, c: (s, 0, 0, 0))
    in_specs = [pl.BlockSpec((group * chunk, 4 * HG_DIM), lambda s, c: (s * n_chunks + c, 0)),
                _const_spec((DEPTH, HG_DIM)), _const_spec((1, HG_HEAD_DIM))]
    args = [x, lb_logits, norm_g.reshape(1, HG_HEAD_DIM)]
    if has_state:
        in_specs.append(state_spec)
        args.append(state)
    return pl.pallas_call(
        functools.partial(_hgrn_kernel, chunk=chunk, n_valid=n_valid, has_state=has_state, layer=layer, group=group),
        grid=(n_seq // group, n_chunks),
        in_specs=in_specs,
        out_specs=[pl.BlockSpec((group * chunk, HG_DIM), lambda s, c: (s * n_chunks + c, 0)), state_spec],
        out_shape=[jax.ShapeDtypeStruct((rows, HG_DIM), _mixer_out_dtype(chunk)),
                   jax.ShapeDtypeStruct((n_seq, HG_HEADS, HG_HEAD_DIM, HG_HEAD_DIM), F32)],
        scratch_shapes=[pltpu.VMEM((group, HG_HEADS, HG_HEAD_DIM, HG_HEAD_DIM), F32)],
        compiler_params=pltpu.CompilerParams(dimension_semantics=("arbitrary", "arbitrary"),
                                             vmem_limit_bytes=VMEM_LIMIT),
        name="hgrn",
    )(*args)


def _pad_sample(t):
    w = t.shape[1]
    t = jnp.pad(t.reshape(DEC_BATCH, DEC_SEQ, w), ((0, 0), (0, PAD_CHUNK - DEC_SEQ), (0, 0)))
    return t.reshape(DEC_BATCH * PAD_CHUNK, w)


def _unpad_sample(t):
    w = t.shape[1]
    return t.reshape(DEC_BATCH, PAD_CHUNK, w)[:, :DEC_SEQ].reshape(N_SAMPLE, w).astype(BF16)


def _pad_cols(w, n):
    return jnp.pad(w, ((0, 0), (0, n - w.shape[1])))


def kernel(x_prompt, x_sample, cache_swa_k, cache_swa_v, state_gdn, state_gdn_conv, state_hgrn, norm_ffn, ffn_w_in, ffn_w_out, norm_mix, swa_w_in, swa_w_out, swa_sinks, gdn_w_in, gdn_conv_w, gdn_a_log, gdn_dt_bias, gdn_norm, gdn_w_out, hgrn_w_in, hgrn_lb_logits, hgrn_norm, hgrn_w_out, final_norm):
    x = (x_prompt.reshape(N_PROMPT, D_MODEL), x_sample.reshape(N_SAMPLE, D_MODEL))
    ffn = functools.partial(_ffn, norm_ffn=norm_ffn.reshape(DEPTH, 2, 1, D_MODEL),
                            w_in=ffn_w_in, w_out=ffn_w_out)
    prompt_tables = _rope_tables(jnp.arange(SEQ), LANES)
    sample_pos = PAST_LEN + jnp.arange(DEC_SEQ)
    window_cols = lambda c: jnp.transpose(c, (0, 1, 3, 4, 2)).reshape(c.shape[0], DEC_BATCH, SWA_KV, WINDOW)
    window_rows = lambda c: jnp.transpose(c.reshape(DEC_BATCH, SWA_KV_HEADS, SWA_HEAD_DIM, WINDOW), (0, 3, 1, 2))
    cache_kt, cache_vt = window_cols(cache_swa_k), window_cols(cache_swa_v)

    outs = {k: [] for k in ("pk", "pv", "pg", "pc", "ph", "sk", "sv", "sg", "sc", "sh")}
    proj = None
    for i in range(DEPTH):
        kind, j = i % N_MIXERS, i // N_MIXERS
        if proj is not None:
            x = ffn(x, i - 1, 1, proj=proj)
        x = ffn(x, i, 0)
        if kind == 0:
            (qkv,) = _norm_proj(x, norm_mix[i], [swa_w_in[j].astype(BF16)], TOKEN_TILE)
            o_p, pk, pv = _swa_prompt(qkv, swa_sinks[j], prompt_tables)
            o_s, sk, sv = _swa_sample(qkv[N_PROMPT:], j, cache_kt, cache_vt, swa_sinks[j], sample_pos)
            kv_shape = (WINDOW, SWA_KV_HEADS, SWA_HEAD_DIM)
            outs["pk"].append(pk.reshape((BATCH,) + kv_shape))
            outs["pv"].append(pv.reshape((BATCH,) + kv_shape))
            outs["sk"].append(window_rows(sk))
            outs["sv"].append(window_rows(sv))
            proj = (o_p, o_s, swa_w_out[j].astype(BF16))
        elif kind == 1:
            w = gdn_w_in[j].astype(BF16)
            z0 = GDN_CONV_DIM + GDN_V_DIM
            weights = [w[:, :GDN_CONV_DIM], w[:, GDN_CONV_DIM:z0],
                       _pad_cols(w[:, z0:z0 + GDN_V_HEADS], LANES), _pad_cols(w[:, z0 + GDN_V_HEADS:], LANES)]
            qkv, z, b, a = _norm_proj(x, norm_mix[i], weights, TOKEN_TILE // 2)
            o_p, pg, p_tail = _gdn(qkv, z, b, a, gdn_conv_w[j], gdn_a_log[j], gdn_dt_bias[j], gdn_norm[j],
                                   BATCH, SEQ // GDN_CHUNK, GDN_CHUNK, GDN_CHUNK)
            o_s, sg, s_tail = _gdn(*[_pad_sample(t[N_PROMPT:]) for t in (qkv, z, b, a)],
                                   gdn_conv_w[j], gdn_a_log[j], gdn_dt_bias[j], gdn_norm[j],
                                   DEC_BATCH, 1, PAD_CHUNK, DEC_SEQ, state=state_gdn[j], conv0=state_gdn_conv[j])
            outs["pg"].append(pg)
            outs["sg"].append(sg)
            keep = GDN_CONV - 1
            outs["pc"].append(p_tail[:, SUBLANES - keep:])
            outs["sc"].append(s_tail[:, DEC_SEQ - keep:DEC_SEQ])
            proj = (o_p, _unpad_sample(o_s), gdn_w_out[j].astype(BF16))
        else:
            (xin,) = _norm_proj(x, norm_mix[i], [hgrn_w_in[j].astype(BF16)], TOKEN_TILE)
            o_p, ph = _hgrn(xin, hgrn_lb_logits, hgrn_norm[j], i, BATCH, SEQ // HG_CHUNK, HG_CHUNK, HG_CHUNK)
            o_s, sh = _hgrn(_pad_sample(xin[N_PROMPT:]), hgrn_lb_logits, hgrn_norm[j], i,
                            DEC_BATCH, 1, PAD_CHUNK, DEC_SEQ, state=state_hgrn[j])
            outs["ph"].append(ph)
            outs["sh"].append(sh)
            proj = (o_p, _unpad_sample(o_s), hgrn_w_out[j].astype(BF16))
    y_prompt, y_sample = ffn(x, DEPTH - 1, 1, proj=proj, final_g=final_norm)
    y_prompt = y_prompt.reshape(BATCH, SEQ, D_MODEL)
    y_sample = y_sample.reshape(DEC_BATCH, DEC_SEQ, D_MODEL)
    st = lambda k: jnp.stack(outs[k])
    return (y_prompt, y_sample, st("pk"), st("pv"), st("pg"), st("pc"), st("ph"),
            st("sk"), st("sv"), st("sg"), st("sc"), st("sh"))
```

```python
import functools
import math

import jax
import jax.numpy as jnp
from jax import lax
from jax.experimental import pallas as pl
from jax.experimental.pallas import tpu as pltpu

D_MODEL = 1024
BATCH = 8
SEQ = 2048
DEPTH = 4
DEC_BATCH = 128
DEC_SEQ = 4
PAST_LEN = 8192
N_MIXERS = 3

SWA_HEAD_DIM = 64
SWA_HEADS = 16
SWA_KV_HEADS = 4
SWA_GROUP = 4
WINDOW = 128
ROT_DIM = 16
ROPE_THETA = 500000.0
SWA_Q = SWA_HEADS * SWA_HEAD_DIM
SWA_KV = SWA_KV_HEADS * SWA_HEAD_DIM

GDN_HEAD_DIM = 128
GDN_QK_HEADS = 8
GDN_V_HEADS = 16
GDN_KEY_DIM = 1024
GDN_V_DIM = 2048
GDN_CONV_DIM = 4096
GDN_CONV = 4

HG_HEAD_DIM = 128
HG_HEADS = 8
HG_DIM = 1024

D_FF = 2816
NORM_EPS = 1e-6
NEG_INF = -1e30
LOG2_E = 1.4426950408889634

LANES = 128
SUBLANES = 8
TOKEN_TILE = 512
N_PROMPT = BATCH * SEQ
N_SAMPLE = DEC_BATCH * DEC_SEQ
N_TOKENS = N_PROMPT + N_SAMPLE
N_PROMPT_TILES = N_PROMPT // TOKEN_TILE
FFN_CHUNK = 256
VMEM_LIMIT = 56 * 1024 * 1024
GDN_CHUNK = 64
HG_CHUNK = 64
PAD_CHUNK = 8
SWA_SAMPLE_BLOCK = 8
SAMPLE_GROUP = 2

F32 = jnp.float32
BF16 = jnp.bfloat16


def _const_spec(shape):
    nd = len(shape)
    return pl.BlockSpec(shape, lambda *_: (0,) * nd, pipeline_mode=pl.Buffered(1))


def _rms(x, g):
    return x * lax.rsqrt(jnp.mean(x * x, axis=-1, keepdims=True) + NORM_EPS) * g


def _sigmoid(x):
    return 1.0 / (1.0 + jnp.exp(-x))


def _silu(x):
    return x * _sigmoid(x)


def _dot(a, b):
    return jnp.dot(a.astype(BF16), b.astype(BF16), preferred_element_type=F32)


def _dot_nt(a, b):
    return lax.dot_general(a.astype(BF16), b.astype(BF16), (((1,), (1,)), ((), ())),
                           preferred_element_type=F32)


def _dot_tn(a, b):
    return lax.dot_general(a.astype(BF16), b.astype(BF16), (((0,), (0,)), ((), ())),
                           preferred_element_type=F32)


def _dot_f32(a, b):
    return jnp.dot(a, b, preferred_element_type=F32, precision=lax.Precision.HIGHEST)


def _ffn_kernel(*refs, split_in, has_proj, final_norm):
    it = iter(refs)
    is_prompt = pl.program_id(0) < N_PROMPT_TILES
    if split_in:
        xp_ref, xs_ref = next(it), next(it)
        x = jnp.where(is_prompt, xp_ref[...], xs_ref[...])
    else:
        x = next(it)[...]
    if has_proj:
        op_ref, os_ref, wp_ref = next(it), next(it), next(it)
    g_ref, wg_ref, wu_ref, wo_ref = next(it), next(it), next(it), next(it)
    gf_ref = next(it) if final_norm else None
    out_refs = list(it)

    if has_proj:
        o = jnp.where(is_prompt, op_ref[...], os_ref[...])
        x = x + jnp.dot(o, wp_ref[...], preferred_element_type=F32)
    h = _rms(x, g_ref[...]).astype(BF16)
    y = jnp.zeros_like(x)
    for c in range(D_FF // FFN_CHUNK):
        cols = slice(c * FFN_CHUNK, (c + 1) * FFN_CHUNK)
        gate = jnp.dot(h, wg_ref[:, cols].astype(BF16), preferred_element_type=F32)
        up = jnp.dot(h, wu_ref[:, cols].astype(BF16), preferred_element_type=F32)
        act = (_silu(gate) * up).astype(BF16)
        y = y + jnp.dot(act, wo_ref[cols, :].astype(BF16), preferred_element_type=F32)
    x = x + 0.5 * y
    if final_norm:
        x = _rms(x, gf_ref[...])
    if len(out_refs) == 1:
        out_refs[0][...] = x
    else:
        @pl.when(is_prompt)
        def _():
            out_refs[0][...] = x

        @pl.when(jnp.logical_not(is_prompt))
        def _():
            out_refs[1][...] = x


def _ffn(x, layer, which, norm_ffn, w_in, w_out, proj=None, final_g=None):
    tile = pl.BlockSpec((TOKEN_TILE, D_MODEL), lambda i: (i, 0))
    prompt_tile = lambda w: pl.BlockSpec((TOKEN_TILE, w), lambda i: (jnp.minimum(i, N_PROMPT_TILES - 1), 0))
    sample_tile = lambda w: pl.BlockSpec((TOKEN_TILE, w), lambda i: (0, 0))
    split_in = isinstance(x, tuple)
    in_specs = [prompt_tile(D_MODEL), sample_tile(D_MODEL)] if split_in else [tile]
    args = list(x) if split_in else [x]
    if proj is not None:
        o_p, o_s, w_p = proj
        k = w_p.shape[0]
        in_specs += [prompt_tile(k), sample_tile(k), _const_spec((k, D_MODEL))]
        args += [o_p, o_s, w_p]
    once = pl.Buffered(1)
    in_specs += [pl.BlockSpec((None, None, 1, D_MODEL), lambda i: (layer, which, 0, 0), pipeline_mode=once),
                 pl.BlockSpec((None, None, D_MODEL, D_FF), lambda i: (layer, which, 0, 0), pipeline_mode=once),
                 pl.BlockSpec((None, None, D_MODEL, D_FF), lambda i: (layer, which, 0, 1), pipeline_mode=once),
                 pl.BlockSpec((None, None, D_FF, D_MODEL), lambda i: (layer, which, 0, 0), pipeline_mode=once)]
    args += [norm_ffn, w_in, w_in, w_out]
    if final_g is not None:
        in_specs.append(_const_spec((1, D_MODEL)))
        args.append(final_g.reshape(1, D_MODEL))
        out_specs = [prompt_tile(D_MODEL), sample_tile(D_MODEL)]
        out_shape = [jax.ShapeDtypeStruct((N_PROMPT, D_MODEL), F32), jax.ShapeDtypeStruct((N_SAMPLE, D_MODEL), F32)]
    else:
        out_specs = tile
        out_shape = jax.ShapeDtypeStruct((N_TOKENS, D_MODEL), F32)
    return pl.pallas_call(
        functools.partial(_ffn_kernel, split_in=split_in, has_proj=proj is not None, final_norm=final_g is not None),
        grid=(N_TOKENS // TOKEN_TILE,),
        in_specs=in_specs,
        out_specs=out_specs,
        out_shape=out_shape,
        compiler_params=pltpu.CompilerParams(dimension_semantics=("arbitrary",), vmem_limit_bytes=VMEM_LIMIT),
        name="ffn",
    )(*args)


def _norm_proj_kernel(*refs, n_out):
    x_ref, g_ref = refs[0], refs[1]
    w_refs = refs[2:2 + n_out]
    out_refs = refs[2 + n_out:]
    h = _rms(x_ref[...], g_ref[...]).astype(BF16)
    for w_ref, out_ref in zip(w_refs, out_refs):
        out_ref[...] = jnp.dot(h, w_ref[...], preferred_element_type=F32)


def _norm_proj(x, g, weights, tile_rows):
    in_specs = [pl.BlockSpec((tile_rows, D_MODEL), lambda i: (i, 0)), _const_spec((1, D_MODEL))]
    in_specs += [_const_spec(w.shape) for w in weights]
    return pl.pallas_call(
        functools.partial(_norm_proj_kernel, n_out=len(weights)),
        grid=(N_TOKENS // tile_rows,),
        in_specs=in_specs,
        out_specs=[pl.BlockSpec((tile_rows, w.shape[1]), lambda i: (i, 0)) for w in weights],
        out_shape=[jax.ShapeDtypeStruct((N_TOKENS, w.shape[1]), F32) for w in weights],
        compiler_params=pltpu.CompilerParams(dimension_semantics=("arbitrary",), vmem_limit_bytes=VMEM_LIMIT),
        name="norm_proj",
    )(x, g.reshape(1, D_MODEL), *weights)


def _rope_tables(pos, width):
    half = ROT_DIM // 2
    inv_freq = ROPE_THETA ** (-jnp.arange(0, ROT_DIM, 2, dtype=F32) / ROT_DIM)
    ang = pos.astype(F32)[:, None] * inv_freq[None, :]
    cos, sin = jnp.cos(ang), jnp.sin(ang)
    n = pos.shape[0]
    rest = SWA_HEAD_DIM - ROT_DIM
    c = jnp.concatenate([cos, cos, jnp.ones((n, rest), F32)], axis=1)
    s_lo = jnp.concatenate([-sin, jnp.zeros((n, half + rest), F32)], axis=1)
    s_hi = jnp.concatenate([jnp.zeros((n, half), F32), sin, jnp.zeros((n, rest), F32)], axis=1)
    reps = width // SWA_HEAD_DIM
    return tuple(jnp.tile(t, (1, reps)) for t in (c, s_lo, s_hi))


def _rope(x, c, s_lo, s_hi):
    half = ROT_DIM // 2
    outs = []
    for j in range(x.shape[1] // LANES):
        xs = x[:, j * LANES:(j + 1) * LANES]
        outs.append(xs * c + pltpu.roll(xs, LANES - half, 1) * s_lo + pltpu.roll(xs, half, 1) * s_hi)
    return outs[0] if len(outs) == 1 else jnp.concatenate(outs, axis=1)


def _swa_prompt_kernel(sink_ref, q_ref, k_ref, v_ref, c_ref, slo_ref, shi_ref,
                       o_ref, kc_ref, vc_ref, kprev, vprev):
    n = pl.program_id(1)

    @pl.when(n == 0)
    def _():
        kprev[...] = jnp.zeros_like(kprev)
        vprev[...] = jnp.zeros_like(vprev)

    c, s_lo, s_hi = c_ref[...], slo_ref[...], shi_ref[...]
    q = _rope(q_ref[...], c, s_lo, s_hi) * (SWA_HEAD_DIM ** -0.5 * LOG2_E)
    k_cur = _rope(k_ref[...], c, s_lo, s_hi)
    v_cur = v_ref[...]
    slot0 = lax.broadcasted_iota(jnp.int32, (2 * WINDOW, 1), 0) == 0
    keys = jnp.where(slot0, 0.0, jnp.concatenate([kprev[...], k_cur], axis=0))
    vals = jnp.where(slot0, 0.0, jnp.concatenate([vprev[...], v_cur], axis=0))
    qi = lax.broadcasted_iota(jnp.int32, (WINDOW, 2 * WINDOW), 0)
    kj = lax.broadcasted_iota(jnp.int32, (WINDOW, 2 * WINDOW), 1)
    low = jnp.maximum(qi, jnp.where(n > 0, 0, WINDOW - 1))
    bias_one = jnp.where(jnp.logical_and(kj > low, kj <= qi + WINDOW), 0.0, NEG_INF)
    ones = jnp.ones((2 * WINDOW, LANES), BF16)

    low_half = lax.broadcasted_iota(jnp.int32, (1, LANES), 1) < SWA_HEAD_DIM

    def both_halves(t):
        out = []
        for grp in range(SWA_KV // LANES):
            g = t[:, grp * LANES:(grp + 1) * LANES]
            swapped = pltpu.roll(g, SWA_HEAD_DIM, 1)
            out += [jnp.where(low_half, g, swapped), jnp.where(low_half, swapped, g)]
        return out

    keys2, vals2 = both_halves(keys), both_halves(vals)
    scores = []
    for kv in range(SWA_KV_HEADS):
        heads = range(kv * SWA_GROUP, (kv + 1) * SWA_GROUP)
        parts, bias = [], []
        for h in heads:
            g = q[:, (h // 2) * LANES:(h // 2 + 1) * LANES]
            parts.append(jnp.where(low_half, g, 0.0) if h % 2 == 0 else jnp.where(low_half, 0.0, g))
            bias.append(jnp.where(kj == 0, sink_ref[h] * LOG2_E, bias_one))
        scores.append(_dot_nt(jnp.concatenate(parts, axis=0), keys2[kv])
                      + jnp.concatenate(bias, axis=0))
    outs = []
    for kv in range(SWA_KV_HEADS):
        s = scores[kv]
        p = jnp.exp2(s - jnp.max(s, axis=-1, keepdims=True)).astype(BF16)
        denom = jnp.dot(p, ones, preferred_element_type=F32)
        outs.append(_dot(p, vals2[kv]) / denom)
    for kv in range(SWA_KV_HEADS):
        for pair in range(SWA_GROUP // 2):
            even, odd = (outs[kv][g * WINDOW:(g + 1) * WINDOW] for g in (2 * pair, 2 * pair + 1))
            grp = (kv * SWA_GROUP) // 2 + pair
            o_ref[:, grp * LANES:(grp + 1) * LANES] = jnp.where(low_half, even, odd).astype(o_ref.dtype)

    kprev[...] = k_cur
    vprev[...] = v_cur

    @pl.when(n == pl.num_programs(1) - 1)
    def _():
        kc_ref[0] = k_cur
        vc_ref[0] = v_cur


def _swa_prompt(qkv, sinks, tables):
    nb = SEQ // WINDOW
    q_blocks = SWA_Q // SWA_KV
    tab = pl.BlockSpec((WINDOW, LANES), lambda b, n, *_: (n, 0))
    cache = pl.BlockSpec((1, WINDOW, SWA_KV), lambda b, n, *_: (b, 0, 0))
    return pl.pallas_call(
        _swa_prompt_kernel,
        grid_spec=pltpu.PrefetchScalarGridSpec(
            num_scalar_prefetch=1,
            grid=(BATCH, nb),
            in_specs=[pl.BlockSpec((WINDOW, SWA_Q), lambda b, n, *_: (b * nb + n, 0)),
                      pl.BlockSpec((WINDOW, SWA_KV), lambda b, n, *_: (b * nb + n, q_blocks)),
                      pl.BlockSpec((WINDOW, SWA_KV), lambda b, n, *_: (b * nb + n, q_blocks + 1)),
                      tab, tab, tab],
            out_specs=[pl.BlockSpec((WINDOW, SWA_Q), lambda b, n, *_: (b * nb + n, 0)), cache, cache],
            scratch_shapes=[pltpu.VMEM((WINDOW, SWA_KV), F32), pltpu.VMEM((WINDOW, SWA_KV), F32)]),
        out_shape=[jax.ShapeDtypeStruct((N_PROMPT, SWA_Q), BF16),
                   jax.ShapeDtypeStruct((BATCH, WINDOW, SWA_KV), F32),
                   jax.ShapeDtypeStruct((BATCH, WINDOW, SWA_KV), F32)],
        compiler_params=pltpu.CompilerParams(dimension_semantics=("arbitrary", "arbitrary")),
        name="swa_prompt",
    )(sinks, qkv, qkv, qkv, *tables)


def _swa_sample_kernel(q_ref, kn_ref, vn_ref, kt_ref, vt_ref, sink_ref,
                       qc_ref, qlo_ref, qhi_ref, kc_tab, klo_tab, khi_tab,
                       o_ref, kto_ref, vto_ref, knew, vnew):
    rows = SWA_KV_HEADS * DEC_SEQ * SWA_GROUP
    token = lambda w: jnp.bitwise_and(jnp.right_shift(lax.broadcasted_iota(jnp.int32, (rows, w), 0), 2), DEC_SEQ - 1)
    valid_old = lax.broadcasted_iota(jnp.int32, (rows, WINDOW), 1) > token(WINDOW)
    valid_new = lax.broadcasted_iota(jnp.int32, (rows, PAD_CHUNK), 1) <= token(PAD_CHUNK)
    sink = sink_ref[:, 0:1]
    lane_head = jnp.right_shift(lax.broadcasted_iota(jnp.int32, (DEC_SEQ * SWA_GROUP, SWA_KV), 1), 6)
    zero_tail = jnp.zeros((PAD_CHUNK - DEC_SEQ, SWA_KV), F32)
    slot = lax.broadcasted_iota(jnp.int32, (SWA_KV, WINDOW), 1)
    place = (lax.broadcasted_iota(jnp.int32, (PAD_CHUNK, WINDOW), 1)
             == lax.broadcasted_iota(jnp.int32, (PAD_CHUNK, WINDOW), 0) + (WINDOW - DEC_SEQ)).astype(F32)

    def shifted(old_t, new_rows):
        appended = lax.dot_general(new_rows, place, (((0,), (0,)), ((), ())), preferred_element_type=F32,
                                   precision=lax.Precision.HIGHEST)
        return jnp.where(slot >= WINDOW - DEC_SEQ, appended, pltpu.roll(old_t, WINDOW - DEC_SEQ, 1))

    for b in range(SWA_SAMPLE_BLOCK):
        knew[0:DEC_SEQ, :] = _rope(kn_ref[b], kc_tab[...], klo_tab[...], khi_tab[...])
        vnew[0:DEC_SEQ, :] = vn_ref[b]
        knew[DEC_SEQ:PAD_CHUNK, :] = zero_tail
        vnew[DEC_SEQ:PAD_CHUNK, :] = zero_tail
        k_new, v_new = knew[...], vnew[...]
        k_old, v_old = kt_ref[b], vt_ref[b]

        q = _rope(q_ref[b], qc_ref[...], qlo_ref[...], qhi_ref[...]) * (SWA_HEAD_DIM ** -0.5)
        s_old = jnp.where(valid_old, _dot(q, k_old), NEG_INF)
        s_new = jnp.where(valid_new, _dot_nt(q, k_new), NEG_INF)
        m = jnp.maximum(jnp.maximum(jnp.max(s_old, axis=-1, keepdims=True),
                                    jnp.max(s_new, axis=-1, keepdims=True)), sink)
        p_old = jnp.exp(s_old - m)
        p_new = jnp.exp(s_new - m)
        denom = (jnp.sum(p_old, axis=-1, keepdims=True) + jnp.sum(p_new, axis=-1, keepdims=True)
                 + jnp.exp(sink - m))
        o_all = (_dot_nt(p_old, v_old) + _dot(p_new, v_new)) / denom
        per_head = DEC_SEQ * SWA_GROUP
        o = jnp.zeros((per_head, SWA_KV), F32)
        for h in range(SWA_KV_HEADS):
            o = o + jnp.where(lane_head == h, o_all[h * per_head:(h + 1) * per_head, :], 0.0)
        o_ref[b] = o.astype(o_ref.dtype)
        kto_ref[b] = shifted(k_old, k_new)
        vto_ref[b] = shifted(v_old, v_new)


def _swa_sample(qkv_s, layer, cache_kt, cache_vt, sinks, pos):
    q = qkv_s[:, :SWA_Q].reshape(DEC_BATCH, DEC_SEQ, SWA_KV_HEADS, SWA_GROUP, SWA_HEAD_DIM)
    q = jnp.transpose(q, (0, 2, 1, 3, 4))
    eye = jnp.eye(SWA_KV_HEADS, dtype=F32)
    rows = SWA_KV_HEADS * DEC_SEQ * SWA_GROUP
    q_bd = (q[:, :, :, :, None, :] * eye[None, :, None, None, :, None]).reshape(DEC_BATCH, rows, SWA_KV)
    k_new = qkv_s[:, SWA_Q:SWA_Q + SWA_KV].reshape(DEC_BATCH, DEC_SEQ, SWA_KV)
    v_new = qkv_s[:, SWA_Q + SWA_KV:].reshape(DEC_BATCH, DEC_SEQ, SWA_KV)
    row_t = (jnp.arange(rows) // SWA_GROUP) % DEC_SEQ
    q_tabs = _rope_tables(pos[row_t], LANES)
    k_tabs = _rope_tables(pos, LANES)
    row_head = (jnp.arange(rows) // (DEC_SEQ * SWA_GROUP)) * SWA_GROUP + jnp.arange(rows) % SWA_GROUP
    sink_rows = jnp.broadcast_to(sinks[row_head][:, None], (rows, LANES))

    nblk = SWA_SAMPLE_BLOCK
    blk = lambda r, c: pl.BlockSpec((nblk, r, c), lambda i: (i, 0, 0))
    window_in = pl.BlockSpec((None, nblk, SWA_KV, WINDOW), lambda i: (layer, i, 0, 0))
    per_head = DEC_SEQ * SWA_GROUP
    o, ko, vo = pl.pallas_call(
        _swa_sample_kernel,
        grid=(DEC_BATCH // nblk,),
        in_specs=[blk(rows, SWA_KV), blk(DEC_SEQ, SWA_KV), blk(DEC_SEQ, SWA_KV),
                  window_in, window_in, _const_spec((rows, LANES)),
                  _const_spec((rows, LANES)), _const_spec((rows, LANES)), _const_spec((rows, LANES)),
                  _const_spec((DEC_SEQ, LANES)), _const_spec((DEC_SEQ, LANES)), _const_spec((DEC_SEQ, LANES))],
        out_specs=[blk(per_head, SWA_KV), blk(SWA_KV, WINDOW), blk(SWA_KV, WINDOW)],
        out_shape=[jax.ShapeDtypeStruct((DEC_BATCH, per_head, SWA_KV), BF16),
                   jax.ShapeDtypeStruct((DEC_BATCH, SWA_KV, WINDOW), F32),
                   jax.ShapeDtypeStruct((DEC_BATCH, SWA_KV, WINDOW), F32)],
        scratch_shapes=[pltpu.VMEM((PAD_CHUNK, SWA_KV), F32), pltpu.VMEM((PAD_CHUNK, SWA_KV), F32)],
        compiler_params=pltpu.CompilerParams(dimension_semantics=("arbitrary",)),
        name="swa_sample",
    )(q_bd, k_new, v_new, cache_kt, cache_vt, sink_rows, *q_tabs, *k_tabs)
    o = o.reshape(DEC_BATCH, DEC_SEQ, SWA_GROUP, SWA_KV_HEADS, SWA_HEAD_DIM)
    o = jnp.transpose(o, (0, 1, 3, 2, 4)).reshape(N_SAMPLE, SWA_Q)
    return o, ko, vo


def _mixer_out_dtype(chunk):
    return BF16 if chunk % (2 * SUBLANES) == 0 else F32


def _tri(n, strict=False):
    i = lax.broadcasted_iota(jnp.int32, (n, n), 0)
    j = lax.broadcasted_iota(jnp.int32, (n, n), 1)
    return (j < i) if strict else (j <= i)


def _spread_rows(blk, group, chunk, n_valid):
    if blk.shape[0] == group * chunk:
        return blk
    return jnp.concatenate([blk if s == 0 else pltpu.roll(blk, chunk - s * n_valid, 0) for s in range(group)], axis=0)


def _gather_rows(parts, chunk, n_valid, packed):
    if not packed:
        return jnp.concatenate(parts, axis=0) if len(parts) > 1 else parts[0]
    row = lax.broadcasted_iota(jnp.int32, (chunk, 1), 0)
    out = parts[0]
    for s in range(1, len(parts)):
        out = jnp.where(row >= s * n_valid, pltpu.roll(parts[s], s * n_valid, 0), out)
    return out


def _gdn_kernel(*refs, chunk, n_valid, has_state, group):
    it = iter(refs)
    x_ref, z_ref, b_ref, a_ref = next(it), next(it), next(it), next(it)
    cw_ref, alog_ref, dtb_ref, ng_ref = next(it), next(it), next(it), next(it)
    if has_state:
        s0_ref, c0_ref = next(it), next(it)
    o_ref, s_ref, tail_ref = next(it), next(it), next(it)
    prev = next(it)
    c = pl.program_id(1)
    seqs = range(group)

    @pl.when(c == 0)
    def _():
        if has_state:
            s_ref[...] = s0_ref[...]
            for s in seqs:
                prev[s, 0:SUBLANES - GDN_CONV + 1, :] = jnp.zeros((SUBLANES - GDN_CONV + 1, GDN_CONV_DIM), F32)
                prev[s, SUBLANES - GDN_CONV + 1:SUBLANES, :] = c0_ref[s]
        else:
            s_ref[...] = jnp.zeros_like(s_ref)
            prev[...] = jnp.zeros_like(prev)

    incl = _tri(chunk)
    strict = _tri(chunk, strict=True)
    incl_f = incl.astype(F32)
    upper_f = (lax.broadcasted_iota(jnp.int32, (chunk, chunk), 0)
               <= lax.broadcasted_iota(jnp.int32, (chunk, chunk), 1)).astype(F32)
    eye = (lax.broadcasted_iota(jnp.int32, (chunk, chunk), 0)
           == lax.broadcasted_iota(jnp.int32, (chunk, chunk), 1)).astype(F32)
    live = lax.broadcasted_iota(jnp.int32, (chunk, LANES), 0) < n_valid
    n_pow = max(2, math.ceil(math.log2(min(chunk, n_valid))))
    ng = ng_ref[...]
    rep = GDN_V_HEADS // GDN_QK_HEADS
    head_cols = lambda base, h: slice(base + h * GDN_HEAD_DIM, base + (h + 1) * GDN_HEAD_DIM)

    packed = x_ref.shape[0] != group * chunk
    x_all, z_all, b_all, a_all = (_spread_rows(r[...], group, chunk, n_valid) for r in (x_ref, z_ref, b_ref, a_ref))

    xc, beta, gcum, gcum_t = [], [], [], []
    for s in seqs:
        rows = slice(s * chunk, (s + 1) * chunk)
        x = x_all[rows, :]
        full = jnp.concatenate([prev[s], x], axis=0)
        acc = x * cw_ref[GDN_CONV - 1:GDN_CONV, :]
        for tap in range(GDN_CONV - 1):
            back = GDN_CONV - 1 - tap
            acc = acc + pltpu.roll(full, back, 0)[SUBLANES:, :] * cw_ref[tap:tap + 1, :]
        prev[s] = x[chunk - SUBLANES:chunk, :]
        tail_ref[s] = x[chunk - SUBLANES:chunk, :]
        xc.append(_silu(acc))
        beta.append(jnp.where(live, _sigmoid(b_all[rows, :]), 0.0))
        a = a_all[rows, :] + dtb_ref[...]
        softplus = jnp.maximum(a, 0.0) + jnp.log(1.0 + jnp.exp(-jnp.abs(a)))
        g = jnp.where(live, -jnp.exp(alog_ref[...]) * softplus, 0.0) * LOG2_E
        gcum.append(_dot_f32(incl_f, g))
        gcum_t.append(lax.dot_general(g, upper_f, (((0,), (0,)), ((), ())), preferred_element_type=F32,
                                      precision=lax.Precision.HIGHEST))

    pairs = [(s, hq) for s in seqs for hq in range(GDN_QK_HEADS)]
    units = [(s, h) for s in seqs for h in range(GDN_V_HEADS)]
    qn, kn, kk, qk, ks, qs = {}, {}, {}, {}, {}, {}
    for s, hq in pairs:
        qh = xc[s][:, head_cols(0, hq)]
        kh = xc[s][:, head_cols(GDN_KEY_DIM, hq)]
        qn[s, hq] = qh * lax.rsqrt(jnp.sum(qh * qh, axis=-1, keepdims=True) + NORM_EPS) * (GDN_HEAD_DIM ** -0.5)
        kn[s, hq] = kh * lax.rsqrt(jnp.sum(kh * kh, axis=-1, keepdims=True) + NORM_EPS)
        kq = jnp.concatenate([kn[s, hq], qn[s, hq]], axis=0)
        scores = _dot_nt(kq, kn[s, hq])
        kk[s, hq] = jnp.where(strict, scores[:chunk], 0.0)
        qk[s, hq] = scores[chunk:]
        s_pair = jnp.concatenate([s_ref[s, hq * rep + r] for r in range(rep)], axis=1)
        against_state = _dot(kq, s_pair)
        for r in range(rep):
            ks[s, hq * rep + r] = against_state[:chunk, head_cols(0, r)]
            qs[s, hq * rep + r] = against_state[chunk:, head_cols(0, r)]

    gc = {(s, h): gcum[s][:, h:h + 1] for s, h in units}
    bc = {(s, h): beta[s][:, h:h + 1] for s, h in units}
    decay = {(s, h): jnp.exp2(jnp.where(incl, gc[s, h] - gcum_t[s][h:h + 1, :], NEG_INF)) for s, h in units}
    q_pow = {(s, h): kk[s, h // rep] * decay[s, h] * (-bc[s, h]) for s, h in units}
    t_mat = {u: eye + q_pow[u] for u in units}
    q_pow = {u: _dot(q_pow[u], q_pow[u]) for u in units}
    for _ in range(n_pow - 2):
        both = {u: _dot(jnp.concatenate([t_mat[u], q_pow[u]], axis=0), q_pow[u]) for u in units}
        t_mat = {u: t_mat[u] + both[u][:chunk] for u in units}
        q_pow = {u: both[u][chunk:] for u in units}
    t_mat = {u: t_mat[u] + _dot(t_mat[u], q_pow[u]) for u in units}

    egc = {u: jnp.exp2(gc[u]) for u in units}
    v_new = {}
    for s, h in units:
        vh = xc[s][:, head_cols(2 * GDN_KEY_DIM, h)]
        v_new[s, h] = _dot(t_mat[s, h], bc[s, h] * (vh - egc[s, h] * ks[s, h]))
    gated = {}
    for s, h in units:
        o = egc[s, h] * qs[s, h] + _dot(qk[s, h // rep] * decay[s, h], v_new[s, h])
        g_last = gcum[s][chunk - 1:chunk, h:h + 1]
        s_ref[s, h] = (s_ref[s, h] * jnp.exp2(g_last)
                       + _dot_tn(kn[s, h // rep] * jnp.exp2(g_last - gc[s, h]), v_new[s, h]))
        zh = z_all[s * chunk:(s + 1) * chunk, head_cols(0, h)]
        gated[s, h] = _rms(o, ng) * _silu(zh)
    for h in range(GDN_V_HEADS):
        both = _gather_rows([gated[s, h] for s in seqs], chunk, n_valid, packed)
        o_ref[:, head_cols(0, h)] = both.astype(o_ref.dtype)


def _seq_blocks(n_chunks, chunk, n_valid, row_offset):
    group = SAMPLE_GROUP if n_chunks == 1 else 1
    block_rows = group * n_valid if n_chunks == 1 else chunk
    first = row_offset // block_rows
    return group, block_rows, lambda s, c: (first + s * n_chunks + c, 0)


def _gdn(x, z, b, a, conv_w, a_log, dt_bias, norm_g, n_seq, n_chunks, chunk, n_valid, row_offset=0,
         state=None, conv0=None):
    has_state = state is not None
    rows = n_seq * n_chunks * n_valid
    group, block_rows, row_map = _seq_blocks(n_chunks, chunk, n_valid, row_offset)
    blk = lambda w: pl.BlockSpec((block_rows, w), row_map)
    out_blk = pl.BlockSpec((block_rows, GDN_V_DIM), lambda s, c: (s * n_chunks + c, 0))
    pad_heads = lambda v: jnp.pad(v.astype(F32), (0, LANES - GDN_V_HEADS)).reshape(1, LANES)
    state_spec = pl.BlockSpec((group, GDN_V_HEADS, GDN_HEAD_DIM, GDN_HEAD_DIM), lambda s, c: (s, 0, 0, 0))
    in_specs = [blk(GDN_CONV_DIM), blk(GDN_V_DIM), blk(LANES), blk(LANES),
                _const_spec((GDN_CONV, GDN_CONV_DIM)), _const_spec((1, LANES)), _const_spec((1, LANES)),
                _const_spec((1, GDN_HEAD_DIM))]
    args = [x, z, b, a, conv_w, pad_heads(a_log), pad_heads(dt_bias), norm_g.reshape(1, GDN_HEAD_DIM)]
    if has_state:
        in_specs += [state_spec, pl.BlockSpec((group, GDN_CONV - 1, GDN_CONV_DIM), lambda s, c: (s, 0, 0))]
        args += [state, conv0]
    return pl.pallas_call(
        functools.partial(_gdn_kernel, chunk=chunk, n_valid=n_valid, has_state=has_state, group=group),
        grid=(n_seq // group, n_chunks),
        in_specs=in_specs,
        out_specs=[out_blk, state_spec,
                   pl.BlockSpec((group, SUBLANES, GDN_CONV_DIM), lambda s, c: (s, 0, 0))],
        out_shape=[jax.ShapeDtypeStruct((rows, GDN_V_DIM), _mixer_out_dtype(block_rows)),
                   jax.ShapeDtypeStruct((n_seq, GDN_V_HEADS, GDN_HEAD_DIM, GDN_HEAD_DIM), F32),
                   jax.ShapeDtypeStruct((n_seq, SUBLANES, GDN_CONV_DIM), F32)],
        scratch_shapes=[pltpu.VMEM((group, SUBLANES, GDN_CONV_DIM), F32)],
        compiler_params=pltpu.CompilerParams(dimension_semantics=("arbitrary", "arbitrary"),
                                             vmem_limit_bytes=VMEM_LIMIT),
        name="gdn",
    )(*args)


def _hgrn_kernel(*refs, chunk, n_valid, has_state, layer, group):
    it = iter(refs)
    x_ref, lbl_ref, ng_ref = next(it), next(it), next(it)
    if has_state:
        s0_ref = next(it)
    o_ref, s_ref = next(it), next(it)
    st = next(it)
    c = pl.program_id(1)

    units = [(s, h) for s in range(group) for h in range(HG_HEADS)]

    @pl.when(c == 0)
    def _():
        for s, h in units:
            st[s, h] = s0_ref[s, h].T if has_state else jnp.zeros((HG_HEAD_DIM, HG_HEAD_DIM), F32)

    logits = [lbl_ref[i:i + 1, :] for i in range(DEPTH)]
    mx = functools.reduce(jnp.maximum, logits)
    ex = [jnp.exp(l - mx) for l in logits]
    tot = functools.reduce(lambda u, v: u + v, ex)
    probs = [e / tot for e in ex]
    lb = functools.reduce(lambda u, v: u + v, probs[:layer + 1]) - probs[0]

    total = group * chunk
    packed = x_ref.shape[0] != total
    x_all = _spread_rows(x_ref[...], group, chunk, n_valid) if packed else x_ref
    row = lax.broadcasted_iota(jnp.int32, (total, HG_DIM), 0)
    live = jnp.bitwise_and(row, chunk - 1) < n_valid
    fz = x_all[:, HG_DIM:2 * HG_DIM]
    log_f = jnp.where(live, jnp.log(lb + (1.0 - lb) * _sigmoid(fz)), 0.0)
    k_all = jnp.where(live, (1.0 - lb) * _sigmoid(-fz), 0.0)
    q_all = _silu(x_all[:, 0:HG_DIM])
    ri = lax.broadcasted_iota(jnp.int32, (total, total), 0)
    rj = lax.broadcasted_iota(jnp.int32, (total, total), 1)
    incl_f = jnp.logical_and(rj <= ri, rj >= ri - jnp.bitwise_and(ri, chunk - 1)).astype(F32)
    b2_all = _dot_f32(incl_f, log_f) * LOG2_E
    ones_l = jnp.ones((LANES, LANES), BF16)
    ng = ng_ref[...]
    n_blk = chunk // SUBLANES
    sub = lax.broadcasted_iota(jnp.int32, (SUBLANES, LANES), 0)

    rows8 = lambda t, bi: t[bi * SUBLANES:(bi + 1) * SUBLANES, :]
    seq_rows = lambda s: slice(s * chunk, (s + 1) * chunk)
    head_cols = lambda base, h: slice(base + h * HG_HEAD_DIM, base + (h + 1) * HG_HEAD_DIM)
    q_h = {(s, h): q_all[seq_rows(s), head_cols(0, h)] for s, h in units}
    k_h = {(s, h): k_all[seq_rows(s), head_cols(0, h)] for s, h in units}
    b_h = {(s, h): b2_all[seq_rows(s), head_cols(0, h)] for s, h in units}
    v_h = {(s, h): x_all[seq_rows(s), head_cols(2 * HG_DIM, h)] for s, h in units}
    seg = [0]
    for bi in range(1, n_blk):
        seg.append(seg[-1] + bi * SUBLANES)

    o_inter, w_sum, p_all = {}, {}, {}
    for u in units:
        q, k, b2 = q_h[u], k_h[u], b_h[u]
        o_inter[u] = _dot_nt(q * jnp.exp2(b2), st[u])
        tiles = []
        for bi in range(n_blk):
            for jj in range(SUBLANES):
                j = bi * SUBLANES + jj
                e = jnp.where(sub >= jj, rows8(b2, bi) - b2[j:j + 1, :], NEG_INF)
                tiles.append(rows8(q, bi) * k[j:j + 1, :] * jnp.exp2(e))
        w_sum[u] = jnp.dot(jnp.concatenate(tiles, axis=0).astype(BF16), ones_l, preferred_element_type=F32)
        if n_blk > 1:
            q_t, k_t = [], []
            for bi in range(1, n_blk):
                r = b2[bi * SUBLANES - 1:bi * SUBLANES, :]
                q_t.append(rows8(q, bi) * jnp.exp2(rows8(b2, bi) - r))
                k_t.append(k[0:bi * SUBLANES, :] * jnp.exp2(r - b2[0:bi * SUBLANES, :]))
            p_all[u] = _dot_nt(jnp.concatenate(q_t, axis=0), jnp.concatenate(k_t, axis=0))

    if n_blk > 1:
        p_shape = (SUBLANES * (n_blk - 1), seg[-1])
        prow = jnp.right_shift(lax.broadcasted_iota(jnp.int32, p_shape, 0), 3)
        pcol = lax.broadcasted_iota(jnp.int32, p_shape, 1)
        own = functools.reduce(jnp.logical_or, [
            jnp.logical_and(prow == bi - 1, jnp.logical_and(pcol >= seg[bi - 1], pcol < seg[bi]))
            for bi in range(1, n_blk)])
    o_intra = {}
    for u in units:
        v = v_h[u]
        o_blocks = []
        for bi in range(n_blk):
            terms = [rows8(w_sum[u], bi * SUBLANES + jj) * v[bi * SUBLANES + jj:bi * SUBLANES + jj + 1, :]
                     for jj in range(SUBLANES)]
            o_blocks.append(functools.reduce(lambda a, w: a + w, terms))
        if n_blk > 1:
            v_t = jnp.concatenate([v[0:bi * SUBLANES, :] for bi in range(1, n_blk)], axis=0)
            o_off = _dot(jnp.where(own, p_all[u], 0.0), v_t)
            for bi in range(1, n_blk):
                o_blocks[bi] = o_blocks[bi] + rows8(o_off, bi - 1)
        o_intra[u] = o_blocks[0] if n_blk == 1 else jnp.concatenate(o_blocks, axis=0)

    gated = {}
    for u in units:
        s, h = u
        k, b2 = k_h[u], b_h[u]
        b_last = b2[chunk - 1:chunk, :]
        st[u] = st[u] * jnp.exp2(b_last) + _dot_tn(v_h[u], k * jnp.exp2(b_last - b2))
        gate = x_all[seq_rows(s), head_cols(3 * HG_DIM, h)]
        gated[u] = _rms(o_inter[u] + o_intra[u], ng) * _silu(gate)
    for h in range(HG_HEADS):
        both = _gather_rows([gated[s, h] for s in range(group)], chunk, n_valid, packed)
        o_ref[:, head_cols(0, h)] = both.astype(o_ref.dtype)

    @pl.when(c == pl.num_programs(1) - 1)
    def _():
        for s, h in units:
            s_ref[s, h] = st[s, h].T


def _hgrn(x, lb_logits, norm_g, layer, n_seq, n_chunks, chunk, n_valid, row_offset=0, state=None):
    has_state = state is not None
    rows = n_seq * n_chunks * n_valid
    group, block_rows, row_map = _seq_blocks(n_chunks, chunk, n_valid, row_offset)
    state_spec = pl.BlockSpec((group, HG_HEADS, HG_HEAD_DIM, HG_HEAD_DIM), lambda s, c: (s, 0, 0, 0))
    in_specs = [pl.BlockSpec((block_rows, 4 * HG_DIM), row_map),
                _const_spec((DEPTH, HG_DIM)), _const_spec((1, HG_HEAD_DIM))]
    args = [x, lb_logits, norm_g.reshape(1, HG_HEAD_DIM)]
    if has_state:
        in_specs.append(state_spec)
        args.append(state)
    return pl.pallas_call(
        functools.partial(_hgrn_kernel, chunk=chunk, n_valid=n_valid, has_state=has_state, layer=layer, group=group),
        grid=(n_seq // group, n_chunks),
        in_specs=in_specs,
        out_specs=[pl.BlockSpec((block_rows, HG_DIM), lambda s, c: (s * n_chunks + c, 0)), state_spec],
        out_shape=[jax.ShapeDtypeStruct((rows, HG_DIM), _mixer_out_dtype(block_rows)),
                   jax.ShapeDtypeStruct((n_seq, HG_HEADS, HG_HEAD_DIM, HG_HEAD_DIM), F32)],
        scratch_shapes=[pltpu.VMEM((group, HG_HEADS, HG_HEAD_DIM, HG_HEAD_DIM), F32)],
        compiler_params=pltpu.CompilerParams(dimension_semantics=("arbitrary", "arbitrary"),
                                             vmem_limit_bytes=VMEM_LIMIT),
        name="hgrn",
    )(*args)


def _pad_cols(w, n):
    return jnp.pad(w, ((0, 0), (0, n - w.shape[1])))


def kernel(x_prompt, x_sample, cache_swa_k, cache_swa_v, state_gdn, state_gdn_conv, state_hgrn, norm_ffn, ffn_w_in, ffn_w_out, norm_mix, swa_w_in, swa_w_out, swa_sinks, gdn_w_in, gdn_conv_w, gdn_a_log, gdn_dt_bias, gdn_norm, gdn_w_out, hgrn_w_in, hgrn_lb_logits, hgrn_norm, hgrn_w_out, final_norm):
    x = (x_prompt.reshape(N_PROMPT, D_MODEL), x_sample.reshape(N_SAMPLE, D_MODEL))
    ffn = functools.partial(_ffn, norm_ffn=norm_ffn.reshape(DEPTH, 2, 1, D_MODEL),
                            w_in=ffn_w_in, w_out=ffn_w_out)
    prompt_tables = _rope_tables(jnp.arange(SEQ), LANES)
    sample_pos = PAST_LEN + jnp.arange(DEC_SEQ)
    window_cols = lambda c: jnp.transpose(c, (0, 1, 3, 4, 2)).reshape(c.shape[0], DEC_BATCH, SWA_KV, WINDOW)
    window_rows = lambda c: jnp.transpose(c.reshape(DEC_BATCH, SWA_KV_HEADS, SWA_HEAD_DIM, WINDOW), (0, 3, 1, 2))
    cache_kt, cache_vt = window_cols(cache_swa_k), window_cols(cache_swa_v)

    outs = {k: [] for k in ("pk", "pv", "pg", "pc", "ph", "sk", "sv", "sg", "sc", "sh")}
    proj = None
    for i in range(DEPTH):
        kind, j = i % N_MIXERS, i // N_MIXERS
        if proj is not None:
            x = ffn(x, i - 1, 1, proj=proj)
        x = ffn(x, i, 0)
        if kind == 0:
            (qkv,) = _norm_proj(x, norm_mix[i], [swa_w_in[j].astype(BF16)], TOKEN_TILE)
            o_p, pk, pv = _swa_prompt(qkv, swa_sinks[j], prompt_tables)
            o_s, sk, sv = _swa_sample(qkv[N_PROMPT:], j, cache_kt, cache_vt, swa_sinks[j], sample_pos)
            kv_shape = (WINDOW, SWA_KV_HEADS, SWA_HEAD_DIM)
            outs["pk"].append(pk.reshape((BATCH,) + kv_shape))
            outs["pv"].append(pv.reshape((BATCH,) + kv_shape))
            outs["sk"].append(window_rows(sk))
            outs["sv"].append(window_rows(sv))
            proj = (o_p, o_s, swa_w_out[j].astype(BF16))
        elif kind == 1:
            w = gdn_w_in[j].astype(BF16)
            z0 = GDN_CONV_DIM + GDN_V_DIM
            weights = [w[:, :GDN_CONV_DIM], w[:, GDN_CONV_DIM:z0],
                       _pad_cols(w[:, z0:z0 + GDN_V_HEADS], LANES), _pad_cols(w[:, z0 + GDN_V_HEADS:], LANES)]
            qkv, z, b, a = _norm_proj(x, norm_mix[i], weights, TOKEN_TILE)
            o_p, pg, p_tail = _gdn(qkv, z, b, a, gdn_conv_w[j], gdn_a_log[j], gdn_dt_bias[j], gdn_norm[j],
                                   BATCH, SEQ // GDN_CHUNK, GDN_CHUNK, GDN_CHUNK)
            o_s, sg, s_tail = _gdn(qkv, z, b, a, gdn_conv_w[j], gdn_a_log[j], gdn_dt_bias[j], gdn_norm[j],
                                   DEC_BATCH, 1, PAD_CHUNK, DEC_SEQ, row_offset=N_PROMPT,
                                   state=state_gdn[j], conv0=state_gdn_conv[j])
            outs["pg"].append(pg)
            outs["sg"].append(sg)
            keep = GDN_CONV - 1
            outs["pc"].append(p_tail[:, SUBLANES - keep:])
            outs["sc"].append(s_tail[:, DEC_SEQ - keep:DEC_SEQ])
            proj = (o_p, o_s.astype(BF16), gdn_w_out[j].astype(BF16))
        else:
            (xin,) = _norm_proj(x, norm_mix[i], [hgrn_w_in[j].astype(BF16)], TOKEN_TILE)
            o_p, ph = _hgrn(xin, hgrn_lb_logits, hgrn_norm[j], i, BATCH, SEQ // HG_CHUNK, HG_CHUNK, HG_CHUNK)
            o_s, sh = _hgrn(xin, hgrn_lb_logits, hgrn_norm[j], i, DEC_BATCH, 1, PAD_CHUNK, DEC_SEQ,
                            row_offset=N_PROMPT, state=state_hgrn[j])
            outs["ph"].append(ph)
            outs["sh"].append(sh)
            proj = (o_p, o_s.astype(BF16), hgrn_w_out[j].astype(BF16))
    y_prompt, y_sample = ffn(x, DEPTH - 1, 1, proj=proj, final_g=final_norm)
    y_prompt = y_prompt.reshape(BATCH, SEQ, D_MODEL)
    y_sample = y_sample.reshape(DEC_BATCH, DEC_SEQ, D_MODEL)
    st = lambda k: jnp.stack(outs[k])
    return (y_prompt, y_sample, st("pk"), st("pv"), st("pg"), st("pc"), st("ph"),
            st("sk"), st("sv"), st("sg"), st("sc"), st("sh"))
```

```python
import functools
import math

import jax
import jax.numpy as jnp
from jax import lax
from jax.experimental import pallas as pl
from jax.experimental.pallas import tpu as pltpu

D_MODEL = 1024
BATCH = 8
SEQ = 2048
DEPTH = 4
DEC_BATCH = 128
DEC_SEQ = 4
PAST_LEN = 8192
N_MIXERS = 3

SWA_HEAD_DIM = 64
SWA_HEADS = 16
SWA_KV_HEADS = 4
SWA_GROUP = 4
WINDOW = 128
ROT_DIM = 16
ROPE_THETA = 500000.0
SWA_Q = SWA_HEADS * SWA_HEAD_DIM
SWA_KV = SWA_KV_HEADS * SWA_HEAD_DIM

GDN_HEAD_DIM = 128
GDN_QK_HEADS = 8
GDN_V_HEADS = 16
GDN_KEY_DIM = 1024
GDN_V_DIM = 2048
GDN_CONV_DIM = 4096
GDN_CONV = 4

HG_HEAD_DIM = 128
HG_HEADS = 8
HG_DIM = 1024

D_FF = 2816
NORM_EPS = 1e-6
NEG_INF = -1e30
LOG2_E = 1.4426950408889634

LANES = 128
SUBLANES = 8
TOKEN_TILE = 512
N_PROMPT = BATCH * SEQ
N_SAMPLE = DEC_BATCH * DEC_SEQ
N_TOKENS = N_PROMPT + N_SAMPLE
N_PROMPT_TILES = N_PROMPT // TOKEN_TILE
FFN_CHUNK = 256
VMEM_LIMIT = 56 * 1024 * 1024
GDN_CHUNK = 64
HG_CHUNK = 64
PAD_CHUNK = 8
SWA_SAMPLE_BLOCK = 8
SAMPLE_GROUP = 2

F32 = jnp.float32
BF16 = jnp.bfloat16


def _const_spec(shape):
    nd = len(shape)
    return pl.BlockSpec(shape, lambda *_: (0,) * nd, pipeline_mode=pl.Buffered(1))


def _rms(x, g):
    return x * lax.rsqrt(jnp.mean(x * x, axis=-1, keepdims=True) + NORM_EPS) * g


def _sigmoid(x):
    return 1.0 / (1.0 + jnp.exp(-x))


def _silu(x):
    return x * _sigmoid(x)


def _dot(a, b):
    return jnp.dot(a.astype(BF16), b.astype(BF16), preferred_element_type=F32)


def _dot_nt(a, b):
    return lax.dot_general(a.astype(BF16), b.astype(BF16), (((1,), (1,)), ((), ())),
                           preferred_element_type=F32)


def _dot_tn(a, b):
    return lax.dot_general(a.astype(BF16), b.astype(BF16), (((0,), (0,)), ((), ())),
                           preferred_element_type=F32)


def _dot_f32(a, b):
    return jnp.dot(a, b, preferred_element_type=F32, precision=lax.Precision.HIGHEST)


def _ffn_kernel(*refs, split_in, has_proj, final_norm):
    it = iter(refs)
    is_prompt = pl.program_id(0) < N_PROMPT_TILES
    if split_in:
        xp_ref, xs_ref = next(it), next(it)
        x = jnp.where(is_prompt, xp_ref[...], xs_ref[...])
    else:
        x = next(it)[...]
    if has_proj:
        op_ref, os_ref, wp_ref = next(it), next(it), next(it)
    g_ref, wg_ref, wu_ref, wo_ref = next(it), next(it), next(it), next(it)
    gf_ref = next(it) if final_norm else None
    out_refs = list(it)

    if has_proj:
        o = jnp.where(is_prompt, op_ref[...], os_ref[...])
        x = x + jnp.dot(o, wp_ref[...], preferred_element_type=F32)
    h = _rms(x, g_ref[...]).astype(BF16)
    y = jnp.zeros_like(x)
    for c in range(D_FF // FFN_CHUNK):
        cols = slice(c * FFN_CHUNK, (c + 1) * FFN_CHUNK)
        gate = jnp.dot(h, wg_ref[:, cols].astype(BF16), preferred_element_type=F32)
        up = jnp.dot(h, wu_ref[:, cols].astype(BF16), preferred_element_type=F32)
        act = (_silu(gate) * up).astype(BF16)
        y = y + jnp.dot(act, wo_ref[cols, :].astype(BF16), preferred_element_type=F32)
    x = x + 0.5 * y
    if final_norm:
        x = _rms(x, gf_ref[...])
    if len(out_refs) == 1:
        out_refs[0][...] = x
    else:
        @pl.when(is_prompt)
        def _():
            out_refs[0][...] = x

        @pl.when(jnp.logical_not(is_prompt))
        def _():
            out_refs[1][...] = x


def _ffn(x, layer, which, norm_ffn, w_in, w_out, proj=None, final_g=None):
    tile = pl.BlockSpec((TOKEN_TILE, D_MODEL), lambda i: (i, 0))
    prompt_tile = lambda w: pl.BlockSpec((TOKEN_TILE, w), lambda i: (jnp.minimum(i, N_PROMPT_TILES - 1), 0))
    sample_tile = lambda w: pl.BlockSpec((TOKEN_TILE, w), lambda i: (0, 0))
    split_in = isinstance(x, tuple)
    in_specs = [prompt_tile(D_MODEL), sample_tile(D_MODEL)] if split_in else [tile]
    args = list(x) if split_in else [x]
    if proj is not None:
        o_p, o_s, w_p = proj
        k = w_p.shape[0]
        in_specs += [prompt_tile(k), sample_tile(k), _const_spec((k, D_MODEL))]
        args += [o_p, o_s, w_p]
    once = pl.Buffered(1)
    in_specs += [pl.BlockSpec((None, None, 1, D_MODEL), lambda i: (layer, which, 0, 0), pipeline_mode=once),
                 pl.BlockSpec((None, None, D_MODEL, D_FF), lambda i: (layer, which, 0, 0), pipeline_mode=once),
                 pl.BlockSpec((None, None, D_MODEL, D_FF), lambda i: (layer, which, 0, 1), pipeline_mode=once),
                 pl.BlockSpec((None, None, D_FF, D_MODEL), lambda i: (layer, which, 0, 0), pipeline_mode=once)]
    args += [norm_ffn, w_in, w_in, w_out]
    if final_g is not None:
        in_specs.append(_const_spec((1, D_MODEL)))
        args.append(final_g.reshape(1, D_MODEL))
        out_specs = [prompt_tile(D_MODEL), sample_tile(D_MODEL)]
        out_shape = [jax.ShapeDtypeStruct((N_PROMPT, D_MODEL), F32), jax.ShapeDtypeStruct((N_SAMPLE, D_MODEL), F32)]
    else:
        out_specs = tile
        out_shape = jax.ShapeDtypeStruct((N_TOKENS, D_MODEL), F32)
    return pl.pallas_call(
        functools.partial(_ffn_kernel, split_in=split_in, has_proj=proj is not None, final_norm=final_g is not None),
        grid=(N_TOKENS // TOKEN_TILE,),
        in_specs=in_specs,
        out_specs=out_specs,
        out_shape=out_shape,
        compiler_params=pltpu.CompilerParams(dimension_semantics=("arbitrary",), vmem_limit_bytes=VMEM_LIMIT),
        name="ffn",
    )(*args)


def _norm_proj_kernel(*refs, n_out):
    x_ref, g_ref = refs[0], refs[1]
    w_refs = refs[2:2 + n_out]
    out_refs = refs[2 + n_out:]
    h = _rms(x_ref[...], g_ref[...]).astype(BF16)
    for w_ref, out_ref in zip(w_refs, out_refs):
        out_ref[...] = jnp.dot(h, w_ref[...], preferred_element_type=F32)


def _norm_proj(x, g, weights, tile_rows):
    in_specs = [pl.BlockSpec((tile_rows, D_MODEL), lambda i: (i, 0)), _const_spec((1, D_MODEL))]
    in_specs += [_const_spec(w.shape) for w in weights]
    return pl.pallas_call(
        functools.partial(_norm_proj_kernel, n_out=len(weights)),
        grid=(N_TOKENS // tile_rows,),
        in_specs=in_specs,
        out_specs=[pl.BlockSpec((tile_rows, w.shape[1]), lambda i: (i, 0)) for w in weights],
        out_shape=[jax.ShapeDtypeStruct((N_TOKENS, w.shape[1]), F32) for w in weights],
        compiler_params=pltpu.CompilerParams(dimension_semantics=("arbitrary",), vmem_limit_bytes=VMEM_LIMIT),
        name="norm_proj",
    )(x, g.reshape(1, D_MODEL), *weights)


def _rope_tables(pos, width):
    half = ROT_DIM // 2
    inv_freq = ROPE_THETA ** (-jnp.arange(0, ROT_DIM, 2, dtype=F32) / ROT_DIM)
    ang = pos.astype(F32)[:, None] * inv_freq[None, :]
    cos, sin = jnp.cos(ang), jnp.sin(ang)
    n = pos.shape[0]
    rest = SWA_HEAD_DIM - ROT_DIM
    c = jnp.concatenate([cos, cos, jnp.ones((n, rest), F32)], axis=1)
    s_lo = jnp.concatenate([-sin, jnp.zeros((n, half + rest), F32)], axis=1)
    s_hi = jnp.concatenate([jnp.zeros((n, half), F32), sin, jnp.zeros((n, rest), F32)], axis=1)
    reps = width // SWA_HEAD_DIM
    return tuple(jnp.tile(t, (1, reps)) for t in (c, s_lo, s_hi))


def _rope(x, c, s_lo, s_hi):
    half = ROT_DIM // 2
    outs = []
    for j in range(x.shape[1] // LANES):
        xs = x[:, j * LANES:(j + 1) * LANES]
        outs.append(xs * c + pltpu.roll(xs, LANES - half, 1) * s_lo + pltpu.roll(xs, half, 1) * s_hi)
    return outs[0] if len(outs) == 1 else jnp.concatenate(outs, axis=1)


def _swa_prompt_kernel(sink_ref, q_ref, k_ref, v_ref, c_ref, slo_ref, shi_ref,
                       o_ref, kc_ref, vc_ref, kprev, vprev):
    n = pl.program_id(1)

    @pl.when(n == 0)
    def _():
        kprev[...] = jnp.zeros_like(kprev)
        vprev[...] = jnp.zeros_like(vprev)

    c, s_lo, s_hi = c_ref[...], slo_ref[...], shi_ref[...]
    q = _rope(q_ref[...], c, s_lo, s_hi) * (SWA_HEAD_DIM ** -0.5 * LOG2_E)
    k_cur = _rope(k_ref[...], c, s_lo, s_hi)
    v_cur = v_ref[...]
    slot0 = lax.broadcasted_iota(jnp.int32, (2 * WINDOW, 1), 0) == 0
    keys = jnp.where(slot0, 0.0, jnp.concatenate([kprev[...], k_cur], axis=0))
    vals = jnp.where(slot0, 0.0, jnp.concatenate([vprev[...], v_cur], axis=0))
    qi = lax.broadcasted_iota(jnp.int32, (WINDOW, 2 * WINDOW), 0)
    kj = lax.broadcasted_iota(jnp.int32, (WINDOW, 2 * WINDOW), 1)
    low = jnp.maximum(qi, jnp.where(n > 0, 0, WINDOW - 1))
    bias_one = jnp.where(jnp.logical_and(kj > low, kj <= qi + WINDOW), 0.0, NEG_INF)
    ones = jnp.ones((2 * WINDOW, LANES), BF16)

    low_half = lax.broadcasted_iota(jnp.int32, (1, LANES), 1) < SWA_HEAD_DIM

    def both_halves(t):
        out = []
        for grp in range(SWA_KV // LANES):
            g = t[:, grp * LANES:(grp + 1) * LANES]
            swapped = pltpu.roll(g, SWA_HEAD_DIM, 1)
            out += [jnp.where(low_half, g, swapped), jnp.where(low_half, swapped, g)]
        return out

    keys2, vals2 = both_halves(keys), both_halves(vals)
    scores = []
    for kv in range(SWA_KV_HEADS):
        heads = range(kv * SWA_GROUP, (kv + 1) * SWA_GROUP)
        parts, bias = [], []
        for h in heads:
            g = q[:, (h // 2) * LANES:(h // 2 + 1) * LANES]
            parts.append(jnp.where(low_half, g, 0.0) if h % 2 == 0 else jnp.where(low_half, 0.0, g))
            bias.append(jnp.where(kj == 0, sink_ref[h] * LOG2_E, bias_one))
        scores.append(_dot_nt(jnp.concatenate(parts, axis=0), keys2[kv])
                      + jnp.concatenate(bias, axis=0))
    outs = []
    for kv in range(SWA_KV_HEADS):
        s = scores[kv]
        p = jnp.exp2(s - jnp.max(s, axis=-1, keepdims=True)).astype(BF16)
        denom = jnp.dot(p, ones, preferred_element_type=F32)
        outs.append(_dot(p, vals2[kv]) / denom)
    for kv in range(SWA_KV_HEADS):
        for pair in range(SWA_GROUP // 2):
            even, odd = (outs[kv][g * WINDOW:(g + 1) * WINDOW] for g in (2 * pair, 2 * pair + 1))
            grp = (kv * SWA_GROUP) // 2 + pair
            o_ref[:, grp * LANES:(grp + 1) * LANES] = jnp.where(low_half, even, odd).astype(o_ref.dtype)

    kprev[...] = k_cur
    vprev[...] = v_cur

    @pl.when(n == pl.num_programs(1) - 1)
    def _():
        kc_ref[0] = k_cur
        vc_ref[0] = v_cur


def _swa_prompt(qkv, sinks, tables):
    nb = SEQ // WINDOW
    q_blocks = SWA_Q // SWA_KV
    tab = pl.BlockSpec((WINDOW, LANES), lambda b, n, *_: (n, 0))
    cache = pl.BlockSpec((1, WINDOW, SWA_KV), lambda b, n, *_: (b, 0, 0))
    return pl.pallas_call(
        _swa_prompt_kernel,
        grid_spec=pltpu.PrefetchScalarGridSpec(
            num_scalar_prefetch=1,
            grid=(BATCH, nb),
            in_specs=[pl.BlockSpec((WINDOW, SWA_Q), lambda b, n, *_: (b * nb + n, 0)),
                      pl.BlockSpec((WINDOW, SWA_KV), lambda b, n, *_: (b * nb + n, q_blocks)),
                      pl.BlockSpec((WINDOW, SWA_KV), lambda b, n, *_: (b * nb + n, q_blocks + 1)),
                      tab, tab, tab],
            out_specs=[pl.BlockSpec((WINDOW, SWA_Q), lambda b, n, *_: (b * nb + n, 0)), cache, cache],
            scratch_shapes=[pltpu.VMEM((WINDOW, SWA_KV), F32), pltpu.VMEM((WINDOW, SWA_KV), F32)]),
        out_shape=[jax.ShapeDtypeStruct((N_PROMPT, SWA_Q), BF16),
                   jax.ShapeDtypeStruct((BATCH, WINDOW, SWA_KV), F32),
                   jax.ShapeDtypeStruct((BATCH, WINDOW, SWA_KV), F32)],
        compiler_params=pltpu.CompilerParams(dimension_semantics=("arbitrary", "arbitrary")),
        name="swa_prompt",
    )(sinks, qkv, qkv, qkv, *tables)


def _swa_sample_kernel(q_ref, kn_ref, vn_ref, kt_ref, vt_ref, sink_ref,
                       qc_ref, qlo_ref, qhi_ref, kc_tab, klo_tab, khi_tab,
                       o_ref, kto_ref, vto_ref, knew, vnew):
    rows = SWA_KV_HEADS * DEC_SEQ * SWA_GROUP
    token = lambda w: jnp.bitwise_and(jnp.right_shift(lax.broadcasted_iota(jnp.int32, (rows, w), 0), 2), DEC_SEQ - 1)
    valid_old = lax.broadcasted_iota(jnp.int32, (rows, WINDOW), 1) > token(WINDOW)
    valid_new = lax.broadcasted_iota(jnp.int32, (rows, PAD_CHUNK), 1) <= token(PAD_CHUNK)
    sink = sink_ref[:, 0:1]
    lane_head = jnp.right_shift(lax.broadcasted_iota(jnp.int32, (DEC_SEQ * SWA_GROUP, SWA_KV), 1), 6)
    zero_tail = jnp.zeros((PAD_CHUNK - DEC_SEQ, SWA_KV), F32)
    slot = lax.broadcasted_iota(jnp.int32, (SWA_KV, WINDOW), 1)
    place = (lax.broadcasted_iota(jnp.int32, (PAD_CHUNK, WINDOW), 1)
             == lax.broadcasted_iota(jnp.int32, (PAD_CHUNK, WINDOW), 0) + (WINDOW - DEC_SEQ)).astype(BF16)

    def shifted(old_t, new_rows):
        hi = new_rows.astype(BF16)
        rest = new_rows - hi.astype(F32)
        mid = rest.astype(BF16)
        lo = (rest - mid.astype(F32)).astype(BF16)
        appended = _dot_tn(hi, place) + _dot_tn(mid, place) + _dot_tn(lo, place)
        return jnp.where(slot >= WINDOW - DEC_SEQ, appended, pltpu.roll(old_t, WINDOW - DEC_SEQ, 1))

    seqs = range(SWA_SAMPLE_BLOCK)
    for b in seqs:
        knew[b, 0:DEC_SEQ, :] = _rope(kn_ref[b], kc_tab[...], klo_tab[...], khi_tab[...])
        vnew[b, 0:DEC_SEQ, :] = vn_ref[b]
        knew[b, DEC_SEQ:PAD_CHUNK, :] = zero_tail
        vnew[b, DEC_SEQ:PAD_CHUNK, :] = zero_tail
    k_new = [knew[b] for b in seqs]
    v_new = [vnew[b] for b in seqs]
    q = [_rope(q_ref[b], qc_ref[...], qlo_ref[...], qhi_ref[...]) * (SWA_HEAD_DIM ** -0.5) for b in seqs]
    s_old = [jnp.where(valid_old, _dot(q[b], kt_ref[b]), NEG_INF) for b in seqs]
    s_new = [jnp.where(valid_new, _dot_nt(q[b], k_new[b]), NEG_INF) for b in seqs]
    m = [jnp.maximum(jnp.maximum(jnp.max(s_old[b], axis=-1, keepdims=True),
                                 jnp.max(s_new[b], axis=-1, keepdims=True)), sink) for b in seqs]
    p_old = [jnp.exp(s_old[b] - m[b]) for b in seqs]
    p_new = [jnp.exp(s_new[b] - m[b]) for b in seqs]
    per_head = DEC_SEQ * SWA_GROUP
    for b in seqs:
        denom = (jnp.sum(p_old[b], axis=-1, keepdims=True) + jnp.sum(p_new[b], axis=-1, keepdims=True)
                 + jnp.exp(sink - m[b]))
        o_all = (_dot_nt(p_old[b], vt_ref[b]) + _dot(p_new[b], v_new[b])) / denom
        o = jnp.zeros((per_head, SWA_KV), F32)
        for h in range(SWA_KV_HEADS):
            o = o + jnp.where(lane_head == h, o_all[h * per_head:(h + 1) * per_head, :], 0.0)
        o_ref[b] = o.astype(o_ref.dtype)
    for b in seqs:
        kto_ref[b] = shifted(kt_ref[b], k_new[b])
        vto_ref[b] = shifted(vt_ref[b], v_new[b])


def _swa_sample(qkv_s, layer, cache_kt, cache_vt, sinks, pos):
    q = qkv_s[:, :SWA_Q].reshape(DEC_BATCH, DEC_SEQ, SWA_KV_HEADS, SWA_GROUP, SWA_HEAD_DIM)
    q = jnp.transpose(q, (0, 2, 1, 3, 4))
    eye = jnp.eye(SWA_KV_HEADS, dtype=F32)
    rows = SWA_KV_HEADS * DEC_SEQ * SWA_GROUP
    q_bd = (q[:, :, :, :, None, :] * eye[None, :, None, None, :, None]).reshape(DEC_BATCH, rows, SWA_KV)
    k_new = qkv_s[:, SWA_Q:SWA_Q + SWA_KV].reshape(DEC_BATCH, DEC_SEQ, SWA_KV)
    v_new = qkv_s[:, SWA_Q + SWA_KV:].reshape(DEC_BATCH, DEC_SEQ, SWA_KV)
    row_t = (jnp.arange(rows) // SWA_GROUP) % DEC_SEQ
    q_tabs = _rope_tables(pos[row_t], LANES)
    k_tabs = _rope_tables(pos, LANES)
    row_head = (jnp.arange(rows) // (DEC_SEQ * SWA_GROUP)) * SWA_GROUP + jnp.arange(rows) % SWA_GROUP
    sink_rows = jnp.broadcast_to(sinks[row_head][:, None], (rows, LANES))

    nblk = SWA_SAMPLE_BLOCK
    blk = lambda r, c: pl.BlockSpec((nblk, r, c), lambda i: (i, 0, 0))
    window_in = pl.BlockSpec((None, nblk, SWA_KV, WINDOW), lambda i: (layer, i, 0, 0))
    per_head = DEC_SEQ * SWA_GROUP
    o, ko, vo = pl.pallas_call(
        _swa_sample_kernel,
        grid=(DEC_BATCH // nblk,),
        in_specs=[blk(rows, SWA_KV), blk(DEC_SEQ, SWA_KV), blk(DEC_SEQ, SWA_KV),
                  window_in, window_in, _const_spec((rows, LANES)),
                  _const_spec((rows, LANES)), _const_spec((rows, LANES)), _const_spec((rows, LANES)),
                  _const_spec((DEC_SEQ, LANES)), _const_spec((DEC_SEQ, LANES)), _const_spec((DEC_SEQ, LANES))],
        out_specs=[blk(per_head, SWA_KV), blk(SWA_KV, WINDOW), blk(SWA_KV, WINDOW)],
        out_shape=[jax.ShapeDtypeStruct((DEC_BATCH, per_head, SWA_KV), BF16),
                   jax.ShapeDtypeStruct((DEC_BATCH, SWA_KV, WINDOW), F32),
                   jax.ShapeDtypeStruct((DEC_BATCH, SWA_KV, WINDOW), F32)],
        scratch_shapes=[pltpu.VMEM((nblk, PAD_CHUNK, SWA_KV), F32), pltpu.VMEM((nblk, PAD_CHUNK, SWA_KV), F32)],
        compiler_params=pltpu.CompilerParams(dimension_semantics=("arbitrary",)),
        name="swa_sample",
    )(q_bd, k_new, v_new, cache_kt, cache_vt, sink_rows, *q_tabs, *k_tabs)
    o = o.reshape(DEC_BATCH, DEC_SEQ, SWA_GROUP, SWA_KV_HEADS, SWA_HEAD_DIM)
    o = jnp.transpose(o, (0, 1, 3, 2, 4)).reshape(N_SAMPLE, SWA_Q)
    return o, ko, vo


def _mixer_out_dtype(chunk):
    return BF16 if chunk % (2 * SUBLANES) == 0 else F32


def _tri(n, strict=False):
    i = lax.broadcasted_iota(jnp.int32, (n, n), 0)
    j = lax.broadcasted_iota(jnp.int32, (n, n), 1)
    return (j < i) if strict else (j <= i)


def _spread_rows(blk, group, chunk, n_valid):
    if blk.shape[0] == group * chunk:
        return blk
    return jnp.concatenate([blk if s == 0 else pltpu.roll(blk, chunk - s * n_valid, 0) for s in range(group)], axis=0)


def _gather_rows(parts, chunk, n_valid, packed):
    if not packed:
        return jnp.concatenate(parts, axis=0) if len(parts) > 1 else parts[0]
    row = lax.broadcasted_iota(jnp.int32, (chunk, 1), 0)
    out = parts[0]
    for s in range(1, len(parts)):
        out = jnp.where(row >= s * n_valid, pltpu.roll(parts[s], s * n_valid, 0), out)
    return out


def _gdn_kernel(*refs, chunk, n_valid, has_state, group):
    it = iter(refs)
    x_ref, z_ref, b_ref, a_ref = next(it), next(it), next(it), next(it)
    cw_ref, alog_ref, dtb_ref, ng_ref = next(it), next(it), next(it), next(it)
    if has_state:
        s0_ref, c0_ref = next(it), next(it)
    o_ref, s_ref, tail_ref = next(it), next(it), next(it)
    prev = next(it)
    c = pl.program_id(1)
    seqs = range(group)

    @pl.when(c == 0)
    def _():
        if has_state:
            s_ref[...] = s0_ref[...]
            for s in seqs:
                prev[s, 0:SUBLANES - GDN_CONV + 1, :] = jnp.zeros((SUBLANES - GDN_CONV + 1, GDN_CONV_DIM), F32)
                prev[s, SUBLANES - GDN_CONV + 1:SUBLANES, :] = c0_ref[s]
        else:
            s_ref[...] = jnp.zeros_like(s_ref)
            prev[...] = jnp.zeros_like(prev)

    incl = _tri(chunk)
    strict = _tri(chunk, strict=True)
    incl_f = incl.astype(F32)
    upper_f = (lax.broadcasted_iota(jnp.int32, (chunk, chunk), 0)
               <= lax.broadcasted_iota(jnp.int32, (chunk, chunk), 1)).astype(F32)
    eye = (lax.broadcasted_iota(jnp.int32, (chunk, chunk), 0)
           == lax.broadcasted_iota(jnp.int32, (chunk, chunk), 1)).astype(F32)
    live = lax.broadcasted_iota(jnp.int32, (chunk, LANES), 0) < n_valid
    n_pow = max(2, math.ceil(math.log2(min(chunk, n_valid))))
    ng = ng_ref[...]
    rep = GDN_V_HEADS // GDN_QK_HEADS
    head_cols = lambda base, h: slice(base + h * GDN_HEAD_DIM, base + (h + 1) * GDN_HEAD_DIM)

    packed = x_ref.shape[0] != group * chunk
    x_all, z_all, b_all, a_all = (_spread_rows(r[...], group, chunk, n_valid) for r in (x_ref, z_ref, b_ref, a_ref))

    xc, beta, gcum, gcum_t = [], [], [], []
    for s in seqs:
        rows = slice(s * chunk, (s + 1) * chunk)
        x = x_all[rows, :]
        full = jnp.concatenate([prev[s], x], axis=0)
        acc = x * cw_ref[GDN_CONV - 1:GDN_CONV, :]
        for tap in range(GDN_CONV - 1):
            back = GDN_CONV - 1 - tap
            acc = acc + pltpu.roll(full, back, 0)[SUBLANES:, :] * cw_ref[tap:tap + 1, :]
        prev[s] = x[chunk - SUBLANES:chunk, :]
        tail_ref[s] = x[chunk - SUBLANES:chunk, :]
        xc.append(_silu(acc))
        beta.append(jnp.where(live, _sigmoid(b_all[rows, :]), 0.0))
        a = a_all[rows, :] + dtb_ref[...]
        softplus = jnp.maximum(a, 0.0) + jnp.log(1.0 + jnp.exp(-jnp.abs(a)))
        g = jnp.where(live, -jnp.exp(alog_ref[...]) * softplus, 0.0) * LOG2_E
        gcum.append(_dot_f32(incl_f, g))
        gcum_t.append(lax.dot_general(g, upper_f, (((0,), (0,)), ((), ())), preferred_element_type=F32,
                                      precision=lax.Precision.HIGHEST))

    pairs = [(s, hq) for s in seqs for hq in range(GDN_QK_HEADS)]
    units = [(s, h) for s in seqs for h in range(GDN_V_HEADS)]
    qn, kn, kk, qk, ks, qs = {}, {}, {}, {}, {}, {}
    for s, hq in pairs:
        qh = xc[s][:, head_cols(0, hq)]
        kh = xc[s][:, head_cols(GDN_KEY_DIM, hq)]
        qn[s, hq] = qh * lax.rsqrt(jnp.sum(qh * qh, axis=-1, keepdims=True) + NORM_EPS) * (GDN_HEAD_DIM ** -0.5)
        kn[s, hq] = kh * lax.rsqrt(jnp.sum(kh * kh, axis=-1, keepdims=True) + NORM_EPS)
        kq = jnp.concatenate([kn[s, hq], qn[s, hq]], axis=0)
        scores = _dot_nt(kq, kn[s, hq])
        kk[s, hq] = jnp.where(strict, scores[:chunk], 0.0)
        qk[s, hq] = scores[chunk:]
        s_pair = jnp.concatenate([s_ref[s, hq * rep + r] for r in range(rep)], axis=1)
        against_state = _dot(kq, s_pair)
        for r in range(rep):
            ks[s, hq * rep + r] = against_state[:chunk, head_cols(0, r)]
            qs[s, hq * rep + r] = against_state[chunk:, head_cols(0, r)]

    gc = {(s, h): gcum[s][:, h:h + 1] for s, h in units}
    bc = {(s, h): beta[s][:, h:h + 1] for s, h in units}
    decay = {(s, h): jnp.exp2(jnp.where(incl, gc[s, h] - gcum_t[s][h:h + 1, :], NEG_INF)) for s, h in units}
    q_pow = {(s, h): kk[s, h // rep] * decay[s, h] * (-bc[s, h]) for s, h in units}
    t_mat = {u: eye + q_pow[u] for u in units}
    q_pow = {u: _dot(q_pow[u], q_pow[u]) for u in units}
    for _ in range(n_pow - 2):
        both = {u: _dot(jnp.concatenate([t_mat[u], q_pow[u]], axis=0), q_pow[u]) for u in units}
        t_mat = {u: t_mat[u] + both[u][:chunk] for u in units}
        q_pow = {u: both[u][chunk:] for u in units}
    t_mat = {u: t_mat[u] + _dot(t_mat[u], q_pow[u]) for u in units}

    egc = {u: jnp.exp2(gc[u]) for u in units}
    v_new = {}
    for s, h in units:
        vh = xc[s][:, head_cols(2 * GDN_KEY_DIM, h)]
        v_new[s, h] = _dot(t_mat[s, h], bc[s, h] * (vh - egc[s, h] * ks[s, h]))
    gated = {}
    for s, h in units:
        o = egc[s, h] * qs[s, h] + _dot(qk[s, h // rep] * decay[s, h], v_new[s, h])
        g_last = gcum[s][chunk - 1:chunk, h:h + 1]
        s_ref[s, h] = (s_ref[s, h] * jnp.exp2(g_last)
                       + _dot_tn(kn[s, h // rep] * jnp.exp2(g_last - gc[s, h]), v_new[s, h]))
        zh = z_all[s * chunk:(s + 1) * chunk, head_cols(0, h)]
        gated[s, h] = _rms(o, ng) * _silu(zh)
    for h in range(GDN_V_HEADS):
        both = _gather_rows([gated[s, h] for s in seqs], chunk, n_valid, packed)
        o_ref[:, head_cols(0, h)] = both.astype(o_ref.dtype)


def _seq_blocks(n_chunks, chunk, n_valid, row_offset):
    group = SAMPLE_GROUP if n_chunks == 1 else 1
    block_rows = group * n_valid if n_chunks == 1 else chunk
    first = row_offset // block_rows
    return group, block_rows, lambda s, c: (first + s * n_chunks + c, 0)


def _gdn(x, z, b, a, conv_w, a_log, dt_bias, norm_g, n_seq, n_chunks, chunk, n_valid, row_offset=0,
         state=None, conv0=None):
    has_state = state is not None
    rows = n_seq * n_chunks * n_valid
    group, block_rows, row_map = _seq_blocks(n_chunks, chunk, n_valid, row_offset)
    blk = lambda w: pl.BlockSpec((block_rows, w), row_map)
    out_blk = pl.BlockSpec((block_rows, GDN_V_DIM), lambda s, c: (s * n_chunks + c, 0))
    pad_heads = lambda v: jnp.pad(v.astype(F32), (0, LANES - GDN_V_HEADS)).reshape(1, LANES)
    state_spec = pl.BlockSpec((group, GDN_V_HEADS, GDN_HEAD_DIM, GDN_HEAD_DIM), lambda s, c: (s, 0, 0, 0))
    in_specs = [blk(GDN_CONV_DIM), blk(GDN_V_DIM), blk(LANES), blk(LANES),
                _const_spec((GDN_CONV, GDN_CONV_DIM)), _const_spec((1, LANES)), _const_spec((1, LANES)),
                _const_spec((1, GDN_HEAD_DIM))]
    args = [x, z, b, a, conv_w, pad_heads(a_log), pad_heads(dt_bias), norm_g.reshape(1, GDN_HEAD_DIM)]
    if has_state:
        in_specs += [state_spec, pl.BlockSpec((group, GDN_CONV - 1, GDN_CONV_DIM), lambda s, c: (s, 0, 0))]
        args += [state, conv0]
    return pl.pallas_call(
        functools.partial(_gdn_kernel, chunk=chunk, n_valid=n_valid, has_state=has_state, group=group),
        grid=(n_seq // group, n_chunks),
        in_specs=in_specs,
        out_specs=[out_blk, state_spec,
                   pl.BlockSpec((group, SUBLANES, GDN_CONV_DIM), lambda s, c: (s, 0, 0))],
        out_shape=[jax.ShapeDtypeStruct((rows, GDN_V_DIM), _mixer_out_dtype(block_rows)),
                   jax.ShapeDtypeStruct((n_seq, GDN_V_HEADS, GDN_HEAD_DIM, GDN_HEAD_DIM), F32),
                   jax.ShapeDtypeStruct((n_seq, SUBLANES, GDN_CONV_DIM), F32)],
        scratch_shapes=[pltpu.VMEM((group, SUBLANES, GDN_CONV_DIM), F32)],
        compiler_params=pltpu.CompilerParams(dimension_semantics=("arbitrary", "arbitrary"),
                                             vmem_limit_bytes=VMEM_LIMIT),
        name="gdn",
    )(*args)


def _hgrn_kernel(*refs, chunk, n_valid, has_state, layer, group):
    it = iter(refs)
    x_ref, lbl_ref, ng_ref = next(it), next(it), next(it)
    if has_state:
        s0_ref = next(it)
    o_ref, s_ref = next(it), next(it)
    st = next(it)
    c = pl.program_id(1)

    units = [(s, h) for s in range(group) for h in range(HG_HEADS)]

    @pl.when(c == 0)
    def _():
        for s, h in units:
            st[s, h] = s0_ref[s, h].T if has_state else jnp.zeros((HG_HEAD_DIM, HG_HEAD_DIM), F32)

    logits = [lbl_ref[i:i + 1, :] for i in range(DEPTH)]
    mx = functools.reduce(jnp.maximum, logits)
    ex = [jnp.exp(l - mx) for l in logits]
    tot = functools.reduce(lambda u, v: u + v, ex)
    probs = [e / tot for e in ex]
    lb = functools.reduce(lambda u, v: u + v, probs[:layer + 1]) - probs[0]

    total = group * chunk
    packed = x_ref.shape[0] != total
    x_all = _spread_rows(x_ref[...], group, chunk, n_valid) if packed else x_ref
    row = lax.broadcasted_iota(jnp.int32, (total, HG_DIM), 0)
    live = jnp.bitwise_and(row, chunk - 1) < n_valid
    fz = x_all[:, HG_DIM:2 * HG_DIM]
    log_f = jnp.where(live, jnp.log(lb + (1.0 - lb) * _sigmoid(fz)), 0.0)
    k_all = jnp.where(live, (1.0 - lb) * _sigmoid(-fz), 0.0)
    q_all = _silu(x_all[:, 0:HG_DIM])
    ri = lax.broadcasted_iota(jnp.int32, (total, total), 0)
    rj = lax.broadcasted_iota(jnp.int32, (total, total), 1)
    incl_f = jnp.logical_and(rj <= ri, rj >= ri - jnp.bitwise_and(ri, chunk - 1)).astype(F32)
    b2_all = _dot_f32(incl_f, log_f) * LOG2_E
    ones_l = jnp.ones((LANES, LANES), BF16)
    ng = ng_ref[...]
    n_blk = chunk // SUBLANES
    sub = lax.broadcasted_iota(jnp.int32, (SUBLANES, LANES), 0)

    rows8 = lambda t, bi: t[bi * SUBLANES:(bi + 1) * SUBLANES, :]
    seq_rows = lambda s: slice(s * chunk, (s + 1) * chunk)
    head_cols = lambda base, h: slice(base + h * HG_HEAD_DIM, base + (h + 1) * HG_HEAD_DIM)
    q_h = {(s, h): q_all[seq_rows(s), head_cols(0, h)] for s, h in units}
    k_h = {(s, h): k_all[seq_rows(s), head_cols(0, h)] for s, h in units}
    b_h = {(s, h): b2_all[seq_rows(s), head_cols(0, h)] for s, h in units}
    v_h = {(s, h): x_all[seq_rows(s), head_cols(2 * HG_DIM, h)] for s, h in units}
    seg = [0]
    for bi in range(1, n_blk):
        seg.append(seg[-1] + bi * SUBLANES)

    o_inter, w_sum, p_all = {}, {}, {}
    for u in units:
        q, k, b2 = q_h[u], k_h[u], b_h[u]
        o_inter[u] = _dot_nt(q * jnp.exp2(b2), st[u])
        tiles = []
        for bi in range(n_blk):
            for jj in range(SUBLANES):
                j = bi * SUBLANES + jj
                e = jnp.where(sub >= jj, rows8(b2, bi) - b2[j:j + 1, :], NEG_INF)
                tiles.append(rows8(q, bi) * k[j:j + 1, :] * jnp.exp2(e))
        w_sum[u] = jnp.dot(jnp.concatenate(tiles, axis=0).astype(BF16), ones_l, preferred_element_type=F32)
        if n_blk > 1:
            q_t, k_t = [], []
            for bi in range(1, n_blk):
                r = b2[bi * SUBLANES - 1:bi * SUBLANES, :]
                q_t.append(rows8(q, bi) * jnp.exp2(rows8(b2, bi) - r))
                k_t.append(k[0:bi * SUBLANES, :] * jnp.exp2(r - b2[0:bi * SUBLANES, :]))
            p_all[u] = _dot_nt(jnp.concatenate(q_t, axis=0), jnp.concatenate(k_t, axis=0))

    if n_blk > 1:
        p_shape = (SUBLANES * (n_blk - 1), seg[-1])
        prow = jnp.right_shift(lax.broadcasted_iota(jnp.int32, p_shape, 0), 3)
        pcol = lax.broadcasted_iota(jnp.int32, p_shape, 1)
        own = functools.reduce(jnp.logical_or, [
            jnp.logical_and(prow == bi - 1, jnp.logical_and(pcol >= seg[bi - 1], pcol < seg[bi]))
            for bi in range(1, n_blk)])
    o_intra = {}
    for u in units:
        v = v_h[u]
        o_blocks = []
        for bi in range(n_blk):
            terms = [rows8(w_sum[u], bi * SUBLANES + jj) * v[bi * SUBLANES + jj:bi * SUBLANES + jj + 1, :]
                     for jj in range(SUBLANES)]
            o_blocks.append(functools.reduce(lambda a, w: a + w, terms))
        if n_blk > 1:
            v_t = jnp.concatenate([v[0:bi * SUBLANES, :] for bi in range(1, n_blk)], axis=0)
            o_off = _dot(jnp.where(own, p_all[u], 0.0), v_t)
            for bi in range(1, n_blk):
                o_blocks[bi] = o_blocks[bi] + rows8(o_off, bi - 1)
        o_intra[u] = o_blocks[0] if n_blk == 1 else jnp.concatenate(o_blocks, axis=0)

    gated = {}
    for u in units:
        s, h = u
        k, b2 = k_h[u], b_h[u]
        b_last = b2[chunk - 1:chunk, :]
        st[u] = st[u] * jnp.exp2(b_last) + _dot_tn(v_h[u], k * jnp.exp2(b_last - b2))
        gate = x_all[seq_rows(s), head_cols(3 * HG_DIM, h)]
        gated[u] = _rms(o_inter[u] + o_intra[u], ng) * _silu(gate)
    for h in range(HG_HEADS):
        both = _gather_rows([gated[s, h] for s in range(group)], chunk, n_valid, packed)
        o_ref[:, head_cols(0, h)] = both.astype(o_ref.dtype)

    @pl.when(c == pl.num_programs(1) - 1)
    def _():
        for s, h in units:
            s_ref[s, h] = st[s, h].T


def _hgrn(x, lb_logits, norm_g, layer, n_seq, n_chunks, chunk, n_valid, row_offset=0, state=None):
    has_state = state is not None
    rows = n_seq * n_chunks * n_valid
    group, block_rows, row_map = _seq_blocks(n_chunks, chunk, n_valid, row_offset)
    state_spec = pl.BlockSpec((group, HG_HEADS, HG_HEAD_DIM, HG_HEAD_DIM), lambda s, c: (s, 0, 0, 0))
    in_specs = [pl.BlockSpec((block_rows, 4 * HG_DIM), row_map),
                _const_spec((DEPTH, HG_DIM)), _const_spec((1, HG_HEAD_DIM))]
    args = [x, lb_logits, norm_g.reshape(1, HG_HEAD_DIM)]
    if has_state:
        in_specs.append(state_spec)
        args.append(state)
    return pl.pallas_call(
        functools.partial(_hgrn_kernel, chunk=chunk, n_valid=n_valid, has_state=has_state, layer=layer, group=group),
        grid=(n_seq // group, n_chunks),
        in_specs=in_specs,
        out_specs=[pl.BlockSpec((block_rows, HG_DIM), lambda s, c: (s * n_chunks + c, 0)), state_spec],
        out_shape=[jax.ShapeDtypeStruct((rows, HG_DIM), _mixer_out_dtype(block_rows)),
                   jax.ShapeDtypeStruct((n_seq, HG_HEADS, HG_HEAD_DIM, HG_HEAD_DIM), F32)],
        scratch_shapes=[pltpu.VMEM((group, HG_HEADS, HG_HEAD_DIM, HG_HEAD_DIM), F32)],
        compiler_params=pltpu.CompilerParams(dimension_semantics=("arbitrary", "arbitrary"),
                                             vmem_limit_bytes=VMEM_LIMIT),
        name="hgrn",
    )(*args)


def _pad_cols(w, n):
    return jnp.pad(w, ((0, 0), (0, n - w.shape[1])))


def kernel(x_prompt, x_sample, cache_swa_k, cache_swa_v, state_gdn, state_gdn_conv, state_hgrn, norm_ffn, ffn_w_in, ffn_w_out, norm_mix, swa_w_in, swa_w_out, swa_sinks, gdn_w_in, gdn_conv_w, gdn_a_log, gdn_dt_bias, gdn_norm, gdn_w_out, hgrn_w_in, hgrn_lb_logits, hgrn_norm, hgrn_w_out, final_norm):
    x = (x_prompt.reshape(N_PROMPT, D_MODEL), x_sample.reshape(N_SAMPLE, D_MODEL))
    ffn = functools.partial(_ffn, norm_ffn=norm_ffn.reshape(DEPTH, 2, 1, D_MODEL),
                            w_in=ffn_w_in, w_out=ffn_w_out)
    prompt_tables = _rope_tables(jnp.arange(SEQ), LANES)
    sample_pos = PAST_LEN + jnp.arange(DEC_SEQ)
    window_cols = lambda c: jnp.transpose(c, (0, 1, 3, 4, 2)).reshape(c.shape[0], DEC_BATCH, SWA_KV, WINDOW)
    window_rows = lambda c: jnp.transpose(c.reshape(DEC_BATCH, SWA_KV_HEADS, SWA_HEAD_DIM, WINDOW), (0, 3, 1, 2))
    cache_kt, cache_vt = window_cols(cache_swa_k), window_cols(cache_swa_v)

    outs = {k: [] for k in ("pk", "pv", "pg", "pc", "ph", "sk", "sv", "sg", "sc", "sh")}
    proj = None
    for i in range(DEPTH):
        kind, j = i % N_MIXERS, i // N_MIXERS
        if proj is not None:
            x = ffn(x, i - 1, 1, proj=proj)
        x = ffn(x, i, 0)
        if kind == 0:
            (qkv,) = _norm_proj(x, norm_mix[i], [swa_w_in[j].astype(BF16)], TOKEN_TILE)
            o_p, pk, pv = _swa_prompt(qkv, swa_sinks[j], prompt_tables)
            o_s, sk, sv = _swa_sample(qkv[N_PROMPT:], j, cache_kt, cache_vt, swa_sinks[j], sample_pos)
            kv_shape = (WINDOW, SWA_KV_HEADS, SWA_HEAD_DIM)
            outs["pk"].append(pk.reshape((BATCH,) + kv_shape))
            outs["pv"].append(pv.reshape((BATCH,) + kv_shape))
            outs["sk"].append(window_rows(sk))
            outs["sv"].append(window_rows(sv))
            proj = (o_p, o_s, swa_w_out[j].astype(BF16))
        elif kind == 1:
            w = gdn_w_in[j].astype(BF16)
            z0 = GDN_CONV_DIM + GDN_V_DIM
            weights = [w[:, :GDN_CONV_DIM], w[:, GDN_CONV_DIM:z0],
                       _pad_cols(w[:, z0:z0 + GDN_V_HEADS], LANES), _pad_cols(w[:, z0 + GDN_V_HEADS:], LANES)]
            qkv, z, b, a = _norm_proj(x, norm_mix[i], weights, TOKEN_TILE)
            o_p, pg, p_tail = _gdn(qkv, z, b, a, gdn_conv_w[j], gdn_a_log[j], gdn_dt_bias[j], gdn_norm[j],
                                   BATCH, SEQ // GDN_CHUNK, GDN_CHUNK, GDN_CHUNK)
            o_s, sg, s_tail = _gdn(qkv, z, b, a, gdn_conv_w[j], gdn_a_log[j], gdn_dt_bias[j], gdn_norm[j],
                                   DEC_BATCH, 1, PAD_CHUNK, DEC_SEQ, row_offset=N_PROMPT,
                                   state=state_gdn[j], conv0=state_gdn_conv[j])
            outs["pg"].append(pg)
            outs["sg"].append(sg)
            keep = GDN_CONV - 1
            outs["pc"].append(p_tail[:, SUBLANES - keep:])
            outs["sc"].append(s_tail[:, DEC_SEQ - keep:DEC_SEQ])
            proj = (o_p, o_s.astype(BF16), gdn_w_out[j].astype(BF16))
        else:
            (xin,) = _norm_proj(x, norm_mix[i], [hgrn_w_in[j].astype(BF16)], TOKEN_TILE)
            o_p, ph = _hgrn(xin, hgrn_lb_logits, hgrn_norm[j], i, BATCH, SEQ // HG_CHUNK, HG_CHUNK, HG_CHUNK)
            o_s, sh = _hgrn(xin, hgrn_lb_logits, hgrn_norm[j], i, DEC_BATCH, 1, PAD_CHUNK, DEC_SEQ,
                            row_offset=N_PROMPT, state=state_hgrn[j])
            outs["ph"].append(ph)
            outs["sh"].append(sh)
            proj = (o_p, o_s.astype(BF16), hgrn_w_out[j].astype(BF16))
    y_prompt, y_sample = ffn(x, DEPTH - 1, 1, proj=proj, final_g=final_norm)
    y_prompt = y_prompt.reshape(BATCH, SEQ, D_MODEL)
    y_sample = y_sample.reshape(DEC_BATCH, DEC_SEQ, D_MODEL)
    st = lambda k: jnp.stack(outs[k])
    return (y_prompt, y_sample, st("pk"), st("pv"), st("pg"), st("pc"), st("ph"),
            st("sk"), st("sv"), st("sg"), st("sc"), st("sh"))
```

```python
import functools
import math

import jax
import jax.numpy as jnp
from jax import lax
from jax.experimental import pallas as pl
from jax.experimental.pallas import tpu as pltpu

D_MODEL = 1024
BATCH = 8
SEQ = 2048
DEPTH = 4
DEC_BATCH = 128
DEC_SEQ = 4
PAST_LEN = 8192
N_MIXERS = 3

SWA_HEAD_DIM = 64
SWA_HEADS = 16
SWA_KV_HEADS = 4
SWA_GROUP = 4
WINDOW = 128
ROT_DIM = 16
ROPE_THETA = 500000.0
SWA_Q = SWA_HEADS * SWA_HEAD_DIM
SWA_KV = SWA_KV_HEADS * SWA_HEAD_DIM
SWA_IN = SWA_Q + 2 * SWA_KV

GDN_HEAD_DIM = 128
GDN_QK_HEADS = 8
GDN_V_HEADS = 16
GDN_KEY_DIM = 1024
GDN_V_DIM = 2048
GDN_CONV_DIM = 4096
GDN_CONV = 4

HG_HEAD_DIM = 128
HG_HEADS = 8
HG_DIM = 1024

D_FF = 2816
NORM_EPS = 1e-6
NEG_INF = -1e30
LOG2_E = 1.4426950408889634

LANES = 128
SUBLANES = 8
TOKEN_TILE = 512
N_PROMPT = BATCH * SEQ
N_SAMPLE = DEC_BATCH * DEC_SEQ
N_TOKENS = N_PROMPT + N_SAMPLE
N_PROMPT_TILES = N_PROMPT // TOKEN_TILE
FFN_CHUNK = 256
VMEM_LIMIT = 56 * 1024 * 1024
GDN_CHUNK = 64
HG_CHUNK = 64
PAD_CHUNK = 8
SWA_SAMPLE_BLOCK = 8
SAMPLE_GROUP = 2

F32 = jnp.float32
BF16 = jnp.bfloat16


def _const_spec(shape):
    nd = len(shape)
    return pl.BlockSpec(shape, lambda *_: (0,) * nd, pipeline_mode=pl.Buffered(1))


def _rms(x, g):
    return x * lax.rsqrt(jnp.mean(x * x, axis=-1, keepdims=True) + NORM_EPS) * g


def _sigmoid(x):
    return 1.0 / (1.0 + jnp.exp(-x))


def _silu(x):
    return x * _sigmoid(x)


def _dot(a, b):
    return jnp.dot(a.astype(BF16), b.astype(BF16), preferred_element_type=F32)


def _dot_nt(a, b):
    return lax.dot_general(a.astype(BF16), b.astype(BF16), (((1,), (1,)), ((), ())),
                           preferred_element_type=F32)


def _dot_tn(a, b):
    return lax.dot_general(a.astype(BF16), b.astype(BF16), (((0,), (0,)), ((), ())),
                           preferred_element_type=F32)


def _dot_f32(a, b):
    return jnp.dot(a, b, preferred_element_type=F32, precision=lax.Precision.HIGHEST)


def _ffn_kernel(*refs, split_in, has_proj, final_norm):
    it = iter(refs)
    is_prompt = pl.program_id(0) < N_PROMPT_TILES
    if split_in:
        xp_ref, xs_ref = next(it), next(it)
        x = jnp.where(is_prompt, xp_ref[...], xs_ref[...])
    else:
        x = next(it)[...]
    if has_proj:
        op_ref, os_ref, wp_ref = next(it), next(it), next(it)
    g_ref, wg_ref, wu_ref, wo_ref = next(it), next(it), next(it), next(it)
    gf_ref = next(it) if final_norm else None
    out_refs = list(it)

    if has_proj:
        o = jnp.where(is_prompt, op_ref[...], os_ref[...])
        x = x + jnp.dot(o, wp_ref[...], preferred_element_type=F32)
    h = _rms(x, g_ref[...]).astype(BF16)
    y = jnp.zeros_like(x)
    for c in range(D_FF // FFN_CHUNK):
        cols = slice(c * FFN_CHUNK, (c + 1) * FFN_CHUNK)
        gate = jnp.dot(h, wg_ref[:, cols].astype(BF16), preferred_element_type=F32)
        up = jnp.dot(h, wu_ref[:, cols].astype(BF16), preferred_element_type=F32)
        act = (_silu(gate) * up).astype(BF16)
        y = y + jnp.dot(act, wo_ref[cols, :].astype(BF16), preferred_element_type=F32)
    x = x + 0.5 * y
    if final_norm:
        x = _rms(x, gf_ref[...])
    if len(out_refs) == 1:
        out_refs[0][...] = x
    else:
        @pl.when(is_prompt)
        def _():
            out_refs[0][...] = x

        @pl.when(jnp.logical_not(is_prompt))
        def _():
            out_refs[1][...] = x


def _ffn(x, layer, which, norm_ffn, w_in, w_out, proj=None, final_g=None):
    tile = pl.BlockSpec((TOKEN_TILE, D_MODEL), lambda i: (i, 0))
    prompt_tile = lambda w: pl.BlockSpec((TOKEN_TILE, w), lambda i: (jnp.minimum(i, N_PROMPT_TILES - 1), 0))
    sample_tile = lambda w: pl.BlockSpec((TOKEN_TILE, w), lambda i: (0, 0))
    split_in = isinstance(x, tuple)
    in_specs = [prompt_tile(D_MODEL), sample_tile(D_MODEL)] if split_in else [tile]
    args = list(x) if split_in else [x]
    if proj is not None:
        o_p, o_s, w_p = proj
        k = w_p.shape[0]
        in_specs += [prompt_tile(k), sample_tile(k), _const_spec((k, D_MODEL))]
        args += [o_p, o_s, w_p]
    once = pl.Buffered(1)
    in_specs += [pl.BlockSpec((None, None, 1, D_MODEL), lambda i: (layer, which, 0, 0), pipeline_mode=once),
                 pl.BlockSpec((None, None, D_MODEL, D_FF), lambda i: (layer, which, 0, 0), pipeline_mode=once),
                 pl.BlockSpec((None, None, D_MODEL, D_FF), lambda i: (layer, which, 0, 1), pipeline_mode=once),
                 pl.BlockSpec((None, None, D_FF, D_MODEL), lambda i: (layer, which, 0, 0), pipeline_mode=once)]
    args += [norm_ffn, w_in, w_in, w_out]
    if final_g is not None:
        in_specs.append(_const_spec((1, D_MODEL)))
        args.append(final_g.reshape(1, D_MODEL))
        out_specs = [prompt_tile(D_MODEL), sample_tile(D_MODEL)]
        out_shape = [jax.ShapeDtypeStruct((N_PROMPT, D_MODEL), F32), jax.ShapeDtypeStruct((N_SAMPLE, D_MODEL), F32)]
    else:
        out_specs = tile
        out_shape = jax.ShapeDtypeStruct((N_TOKENS, D_MODEL), F32)
    return pl.pallas_call(
        functools.partial(_ffn_kernel, split_in=split_in, has_proj=proj is not None, final_norm=final_g is not None),
        grid=(N_TOKENS // TOKEN_TILE,),
        in_specs=in_specs,
        out_specs=out_specs,
        out_shape=out_shape,
        compiler_params=pltpu.CompilerParams(dimension_semantics=("arbitrary",), vmem_limit_bytes=VMEM_LIMIT),
        name="ffn",
    )(*args)


def _norm_proj_kernel(x_ref, g_ref, w_ref, *out_refs, segments):
    h = _rms(x_ref[...], g_ref[...]).astype(BF16)
    for (start, width, _), out_ref in zip(segments, out_refs):
        y = jnp.dot(h, w_ref[:, start:start + width].astype(BF16), preferred_element_type=F32)
        if width == out_ref.shape[1]:
            out_ref[...] = y
        else:
            out_ref[...] = jnp.zeros_like(out_ref)
            out_ref[:, 0:width] = y


def _norm_proj(x, g, w, layer, segments, tile_rows):
    once = pl.Buffered(1)
    in_specs = [pl.BlockSpec((tile_rows, D_MODEL), lambda i: (i, 0)), _const_spec((1, D_MODEL)),
                pl.BlockSpec((None,) + w.shape[1:], lambda i: (layer, 0, 0), pipeline_mode=once)]
    return pl.pallas_call(
        functools.partial(_norm_proj_kernel, segments=segments),
        grid=(N_TOKENS // tile_rows,),
        in_specs=in_specs,
        out_specs=[pl.BlockSpec((tile_rows, out_w), lambda i: (i, 0)) for _, _, out_w in segments],
        out_shape=[jax.ShapeDtypeStruct((N_TOKENS, out_w), F32) for _, _, out_w in segments],
        compiler_params=pltpu.CompilerParams(dimension_semantics=("arbitrary",), vmem_limit_bytes=VMEM_LIMIT),
        name="norm_proj",
    )(x, g.reshape(1, D_MODEL), w)


def _rope_tables(pos, width):
    half = ROT_DIM // 2
    inv_freq = ROPE_THETA ** (-jnp.arange(0, ROT_DIM, 2, dtype=F32) / ROT_DIM)
    ang = pos.astype(F32)[:, None] * inv_freq[None, :]
    cos, sin = jnp.cos(ang), jnp.sin(ang)
    n = pos.shape[0]
    rest = SWA_HEAD_DIM - ROT_DIM
    c = jnp.concatenate([cos, cos, jnp.ones((n, rest), F32)], axis=1)
    s_lo = jnp.concatenate([-sin, jnp.zeros((n, half + rest), F32)], axis=1)
    s_hi = jnp.concatenate([jnp.zeros((n, half), F32), sin, jnp.zeros((n, rest), F32)], axis=1)
    reps = width // SWA_HEAD_DIM
    return tuple(jnp.tile(t, (1, reps)) for t in (c, s_lo, s_hi))


def _rope(x, c, s_lo, s_hi):
    half = ROT_DIM // 2
    outs = []
    for j in range(x.shape[1] // LANES):
        xs = x[:, j * LANES:(j + 1) * LANES]
        outs.append(xs * c + pltpu.roll(xs, LANES - half, 1) * s_lo + pltpu.roll(xs, half, 1) * s_hi)
    return outs[0] if len(outs) == 1 else jnp.concatenate(outs, axis=1)


def _swa_prompt_kernel(sink_ref, q_ref, k_ref, v_ref, c_ref, slo_ref, shi_ref,
                       o_ref, kc_ref, vc_ref, kprev, vprev):
    n = pl.program_id(1)

    @pl.when(n == 0)
    def _():
        kprev[...] = jnp.zeros_like(kprev)
        vprev[...] = jnp.zeros_like(vprev)

    c, s_lo, s_hi = c_ref[...], slo_ref[...], shi_ref[...]
    q = _rope(q_ref[...], c, s_lo, s_hi) * (SWA_HEAD_DIM ** -0.5 * LOG2_E)
    k_cur = _rope(k_ref[...], c, s_lo, s_hi)
    v_cur = v_ref[...]
    slot0 = lax.broadcasted_iota(jnp.int32, (2 * WINDOW, 1), 0) == 0
    keys = jnp.where(slot0, 0.0, jnp.concatenate([kprev[...], k_cur], axis=0))
    vals = jnp.where(slot0, 0.0, jnp.concatenate([vprev[...], v_cur], axis=0))
    qi = lax.broadcasted_iota(jnp.int32, (WINDOW, 2 * WINDOW), 0)
    kj = lax.broadcasted_iota(jnp.int32, (WINDOW, 2 * WINDOW), 1)
    low = jnp.maximum(qi, jnp.where(n > 0, 0, WINDOW - 1))
    bias_one = jnp.where(jnp.logical_and(kj > low, kj <= qi + WINDOW), 0.0, NEG_INF)
    ones = jnp.ones((2 * WINDOW, LANES), BF16)

    low_half = lax.broadcasted_iota(jnp.int32, (1, LANES), 1) < SWA_HEAD_DIM

    def both_halves(t):
        out = []
        for grp in range(SWA_KV // LANES):
            g = t[:, grp * LANES:(grp + 1) * LANES]
            swapped = pltpu.roll(g, SWA_HEAD_DIM, 1)
            out += [jnp.where(low_half, g, swapped), jnp.where(low_half, swapped, g)]
        return out

    keys2, vals2 = both_halves(keys), both_halves(vals)
    scores = []
    for kv in range(SWA_KV_HEADS):
        heads = range(kv * SWA_GROUP, (kv + 1) * SWA_GROUP)
        parts, bias = [], []
        for h in heads:
            g = q[:, (h // 2) * LANES:(h // 2 + 1) * LANES]
            parts.append(jnp.where(low_half, g, 0.0) if h % 2 == 0 else jnp.where(low_half, 0.0, g))
            bias.append(jnp.where(kj == 0, sink_ref[h] * LOG2_E, bias_one))
        scores.append(_dot_nt(jnp.concatenate(parts, axis=0), keys2[kv])
                      + jnp.concatenate(bias, axis=0))
    outs = []
    for kv in range(SWA_KV_HEADS):
        s = scores[kv]
        p = jnp.exp2(s - jnp.max(s, axis=-1, keepdims=True)).astype(BF16)
        denom = jnp.dot(p, ones, preferred_element_type=F32)
        outs.append(_dot(p, vals2[kv]) / denom)
    for kv in range(SWA_KV_HEADS):
        for pair in range(SWA_GROUP // 2):
            even, odd = (outs[kv][g * WINDOW:(g + 1) * WINDOW] for g in (2 * pair, 2 * pair + 1))
            grp = (kv * SWA_GROUP) // 2 + pair
            o_ref[:, grp * LANES:(grp + 1) * LANES] = jnp.where(low_half, even, odd).astype(o_ref.dtype)

    kprev[...] = k_cur
    vprev[...] = v_cur

    @pl.when(n == pl.num_programs(1) - 1)
    def _():
        kc_ref[0] = k_cur
        vc_ref[0] = v_cur


def _swa_prompt(qkv, sinks, tables):
    nb = SEQ // WINDOW
    q_blocks = SWA_Q // SWA_KV
    tab = pl.BlockSpec((WINDOW, LANES), lambda b, n, *_: (n, 0))
    cache = pl.BlockSpec((1, WINDOW, SWA_KV), lambda b, n, *_: (b, 0, 0))
    return pl.pallas_call(
        _swa_prompt_kernel,
        grid_spec=pltpu.PrefetchScalarGridSpec(
            num_scalar_prefetch=1,
            grid=(BATCH, nb),
            in_specs=[pl.BlockSpec((WINDOW, SWA_Q), lambda b, n, *_: (b * nb + n, 0)),
                      pl.BlockSpec((WINDOW, SWA_KV), lambda b, n, *_: (b * nb + n, q_blocks)),
                      pl.BlockSpec((WINDOW, SWA_KV), lambda b, n, *_: (b * nb + n, q_blocks + 1)),
                      tab, tab, tab],
            out_specs=[pl.BlockSpec((WINDOW, SWA_Q), lambda b, n, *_: (b * nb + n, 0)), cache, cache],
            scratch_shapes=[pltpu.VMEM((WINDOW, SWA_KV), F32), pltpu.VMEM((WINDOW, SWA_KV), F32)]),
        out_shape=[jax.ShapeDtypeStruct((N_PROMPT, SWA_Q), BF16),
                   jax.ShapeDtypeStruct((BATCH, WINDOW, SWA_KV), F32),
                   jax.ShapeDtypeStruct((BATCH, WINDOW, SWA_KV), F32)],
        compiler_params=pltpu.CompilerParams(dimension_semantics=("arbitrary", "arbitrary")),
        name="swa_prompt",
    )(sinks, qkv, qkv, qkv, *tables)


def _swa_sample_kernel(q_ref, kn_ref, vn_ref, kt_ref, vt_ref, sink_ref,
                       qc_ref, qlo_ref, qhi_ref, kc_tab, klo_tab, khi_tab,
                       o_ref, kto_ref, vto_ref, knew, vnew):
    rows = SWA_KV_HEADS * DEC_SEQ * SWA_GROUP
    token = lambda w: jnp.bitwise_and(jnp.right_shift(lax.broadcasted_iota(jnp.int32, (rows, w), 0), 2), DEC_SEQ - 1)
    valid_old = lax.broadcasted_iota(jnp.int32, (rows, WINDOW), 1) > token(WINDOW)
    valid_new = lax.broadcasted_iota(jnp.int32, (rows, PAD_CHUNK), 1) <= token(PAD_CHUNK)
    sink = sink_ref[:, 0:1]
    lane_head = jnp.right_shift(lax.broadcasted_iota(jnp.int32, (DEC_SEQ * SWA_GROUP, SWA_KV), 1), 6)
    zero_tail = jnp.zeros((PAD_CHUNK - DEC_SEQ, SWA_KV), F32)
    slot = lax.broadcasted_iota(jnp.int32, (SWA_KV, WINDOW), 1)
    place = (lax.broadcasted_iota(jnp.int32, (PAD_CHUNK, WINDOW), 1)
             == lax.broadcasted_iota(jnp.int32, (PAD_CHUNK, WINDOW), 0) + (WINDOW - DEC_SEQ)).astype(BF16)

    def shifted(old_t, new_rows):
        hi = new_rows.astype(BF16)
        rest = new_rows - hi.astype(F32)
        mid = rest.astype(BF16)
        lo = (rest - mid.astype(F32)).astype(BF16)
        appended = _dot_tn(hi, place) + _dot_tn(mid, place) + _dot_tn(lo, place)
        return jnp.where(slot >= WINDOW - DEC_SEQ, appended, pltpu.roll(old_t, WINDOW - DEC_SEQ, 1))

    seqs = range(SWA_SAMPLE_BLOCK)
    for b in seqs:
        knew[b, 0:DEC_SEQ, :] = _rope(kn_ref[b], kc_tab[...], klo_tab[...], khi_tab[...])
        vnew[b, 0:DEC_SEQ, :] = vn_ref[b]
        knew[b, DEC_SEQ:PAD_CHUNK, :] = zero_tail
        vnew[b, DEC_SEQ:PAD_CHUNK, :] = zero_tail
    k_new = [knew[b] for b in seqs]
    v_new = [vnew[b] for b in seqs]
    q = [_rope(q_ref[b], qc_ref[...], qlo_ref[...], qhi_ref[...]) * (SWA_HEAD_DIM ** -0.5) for b in seqs]
    s_old = [jnp.where(valid_old, _dot(q[b], kt_ref[b]), NEG_INF) for b in seqs]
    s_new = [jnp.where(valid_new, _dot_nt(q[b], k_new[b]), NEG_INF) for b in seqs]
    m = [jnp.maximum(jnp.maximum(jnp.max(s_old[b], axis=-1, keepdims=True),
                                 jnp.max(s_new[b], axis=-1, keepdims=True)), sink) for b in seqs]
    p_old = [jnp.exp(s_old[b] - m[b]) for b in seqs]
    p_new = [jnp.exp(s_new[b] - m[b]) for b in seqs]
    per_head = DEC_SEQ * SWA_GROUP
    for b in seqs:
        denom = (jnp.sum(p_old[b], axis=-1, keepdims=True) + jnp.sum(p_new[b], axis=-1, keepdims=True)
                 + jnp.exp(sink - m[b]))
        o_all = (_dot_nt(p_old[b], vt_ref[b]) + _dot(p_new[b], v_new[b])) / denom
        o = jnp.zeros((per_head, SWA_KV), F32)
        for h in range(SWA_KV_HEADS):
            o = o + jnp.where(lane_head == h, o_all[h * per_head:(h + 1) * per_head, :], 0.0)
        o_ref[b] = o.astype(o_ref.dtype)
    for b in seqs:
        kto_ref[b] = shifted(kt_ref[b], k_new[b])
        vto_ref[b] = shifted(vt_ref[b], v_new[b])


def _swa_sample(qkv_s, layer, cache_kt, cache_vt, sinks, pos):
    q = qkv_s[:, :SWA_Q].reshape(DEC_BATCH, DEC_SEQ, SWA_KV_HEADS, SWA_GROUP, SWA_HEAD_DIM)
    q = jnp.transpose(q, (0, 2, 1, 3, 4))
    eye = jnp.eye(SWA_KV_HEADS, dtype=F32)
    rows = SWA_KV_HEADS * DEC_SEQ * SWA_GROUP
    q_bd = (q[:, :, :, :, None, :] * eye[None, :, None, None, :, None]).reshape(DEC_BATCH, rows, SWA_KV)
    k_new = qkv_s[:, SWA_Q:SWA_Q + SWA_KV].reshape(DEC_BATCH, DEC_SEQ, SWA_KV)
    v_new = qkv_s[:, SWA_Q + SWA_KV:].reshape(DEC_BATCH, DEC_SEQ, SWA_KV)
    row_t = (jnp.arange(rows) // SWA_GROUP) % DEC_SEQ
    q_tabs = _rope_tables(pos[row_t], LANES)
    k_tabs = _rope_tables(pos, LANES)
    row_head = (jnp.arange(rows) // (DEC_SEQ * SWA_GROUP)) * SWA_GROUP + jnp.arange(rows) % SWA_GROUP
    sink_rows = jnp.broadcast_to(sinks[row_head][:, None], (rows, LANES))

    nblk = SWA_SAMPLE_BLOCK
    blk = lambda r, c: pl.BlockSpec((nblk, r, c), lambda i: (i, 0, 0))
    window_in = pl.BlockSpec((None, nblk, SWA_KV, WINDOW), lambda i: (layer, i, 0, 0))
    per_head = DEC_SEQ * SWA_GROUP
    o, ko, vo = pl.pallas_call(
        _swa_sample_kernel,
        grid=(DEC_BATCH // nblk,),
        in_specs=[blk(rows, SWA_KV), blk(DEC_SEQ, SWA_KV), blk(DEC_SEQ, SWA_KV),
                  window_in, window_in, _const_spec((rows, LANES)),
                  _const_spec((rows, LANES)), _const_spec((rows, LANES)), _const_spec((rows, LANES)),
                  _const_spec((DEC_SEQ, LANES)), _const_spec((DEC_SEQ, LANES)), _const_spec((DEC_SEQ, LANES))],
        out_specs=[blk(per_head, SWA_KV), blk(SWA_KV, WINDOW), blk(SWA_KV, WINDOW)],
        out_shape=[jax.ShapeDtypeStruct((DEC_BATCH, per_head, SWA_KV), BF16),
                   jax.ShapeDtypeStruct((DEC_BATCH, SWA_KV, WINDOW), F32),
                   jax.ShapeDtypeStruct((DEC_BATCH, SWA_KV, WINDOW), F32)],
        scratch_shapes=[pltpu.VMEM((nblk, PAD_CHUNK, SWA_KV), F32), pltpu.VMEM((nblk, PAD_CHUNK, SWA_KV), F32)],
        compiler_params=pltpu.CompilerParams(dimension_semantics=("arbitrary",)),
        name="swa_sample",
    )(q_bd, k_new, v_new, cache_kt, cache_vt, sink_rows, *q_tabs, *k_tabs)
    o = o.reshape(DEC_BATCH, DEC_SEQ, SWA_GROUP, SWA_KV_HEADS, SWA_HEAD_DIM)
    o = jnp.transpose(o, (0, 1, 3, 2, 4)).reshape(N_SAMPLE, SWA_Q)
    return o, ko, vo


def _mixer_out_dtype(chunk):
    return BF16 if chunk % (2 * SUBLANES) == 0 else F32


def _tri(n, strict=False):
    i = lax.broadcasted_iota(jnp.int32, (n, n), 0)
    j = lax.broadcasted_iota(jnp.int32, (n, n), 1)
    return (j < i) if strict else (j <= i)


def _spread_rows(blk, group, chunk, n_valid):
    if blk.shape[0] == group * chunk:
        return blk
    return jnp.concatenate([blk if s == 0 else pltpu.roll(blk, chunk - s * n_valid, 0) for s in range(group)], axis=0)


def _gather_rows(parts, chunk, n_valid, packed):
    if not packed:
        return jnp.concatenate(parts, axis=0) if len(parts) > 1 else parts[0]
    row = lax.broadcasted_iota(jnp.int32, (chunk, 1), 0)
    out = parts[0]
    for s in range(1, len(parts)):
        out = jnp.where(row >= s * n_valid, pltpu.roll(parts[s], s * n_valid, 0), out)
    return out


def _gdn_kernel(*refs, chunk, n_valid, has_state, group):
    it = iter(refs)
    x_ref, z_ref, ba_ref = next(it), next(it), next(it)
    cw_ref, alog_ref, dtb_ref, ng_ref = next(it), next(it), next(it), next(it)
    if has_state:
        s0_ref, c0_ref = next(it), next(it)
    o_ref, s_ref, tail_ref = next(it), next(it), next(it)
    prev = next(it)
    g_lane = GDN_V_HEADS
    c = pl.program_id(1)
    seqs = range(group)

    @pl.when(c == 0)
    def _():
        if has_state:
            s_ref[...] = s0_ref[...]
            for s in seqs:
                prev[s, 0:SUBLANES - GDN_CONV + 1, :] = jnp.zeros((SUBLANES - GDN_CONV + 1, GDN_CONV_DIM), F32)
                prev[s, SUBLANES - GDN_CONV + 1:SUBLANES, :] = c0_ref[s]
        else:
            s_ref[...] = jnp.zeros_like(s_ref)
            prev[...] = jnp.zeros_like(prev)

    incl = _tri(chunk)
    strict = _tri(chunk, strict=True)
    incl_f = incl.astype(F32)
    upper_f = (lax.broadcasted_iota(jnp.int32, (chunk, chunk), 0)
               <= lax.broadcasted_iota(jnp.int32, (chunk, chunk), 1)).astype(F32)
    eye = (lax.broadcasted_iota(jnp.int32, (chunk, chunk), 0)
           == lax.broadcasted_iota(jnp.int32, (chunk, chunk), 1)).astype(F32)
    live = lax.broadcasted_iota(jnp.int32, (chunk, LANES), 0) < n_valid
    n_pow = max(2, math.ceil(math.log2(min(chunk, n_valid))))
    ng = ng_ref[...]
    rep = GDN_V_HEADS // GDN_QK_HEADS
    head_cols = lambda base, h: slice(base + h * GDN_HEAD_DIM, base + (h + 1) * GDN_HEAD_DIM)

    packed = x_ref.shape[0] != group * chunk
    x_all, z_all, ba_all = (_spread_rows(r[...], group, chunk, n_valid) for r in (x_ref, z_ref, ba_ref))

    xc, beta, gcum, gcum_t = [], [], [], []
    for s in seqs:
        rows = slice(s * chunk, (s + 1) * chunk)
        x = x_all[rows, :]
        full = jnp.concatenate([prev[s], x], axis=0)
        acc = x * cw_ref[GDN_CONV - 1:GDN_CONV, :]
        for tap in range(GDN_CONV - 1):
            back = GDN_CONV - 1 - tap
            acc = acc + pltpu.roll(full, back, 0)[SUBLANES:, :] * cw_ref[tap:tap + 1, :]
        prev[s] = x[chunk - SUBLANES:chunk, :]
        tail_ref[s] = x[chunk - SUBLANES:chunk, :]
        xc.append(_silu(acc))
        beta.append(jnp.where(live, _sigmoid(ba_all[rows, :]), 0.0))
        a = ba_all[rows, :] + dtb_ref[...]
        softplus = jnp.maximum(a, 0.0) + jnp.log(1.0 + jnp.exp(-jnp.abs(a)))
        g = jnp.where(live, -jnp.exp(alog_ref[...]) * softplus, 0.0) * LOG2_E
        gcum.append(_dot_f32(incl_f, g))
        gcum_t.append(lax.dot_general(g, upper_f, (((0,), (0,)), ((), ())), preferred_element_type=F32,
                                      precision=lax.Precision.HIGHEST))

    pairs = [(s, hq) for s in seqs for hq in range(GDN_QK_HEADS)]
    units = [(s, h) for s in seqs for h in range(GDN_V_HEADS)]
    qn, kn, kk, qk, ks, qs = {}, {}, {}, {}, {}, {}
    for s, hq in pairs:
        qh = xc[s][:, head_cols(0, hq)]
        kh = xc[s][:, head_cols(GDN_KEY_DIM, hq)]
        qn[s, hq] = qh * lax.rsqrt(jnp.sum(qh * qh, axis=-1, keepdims=True) + NORM_EPS) * (GDN_HEAD_DIM ** -0.5)
        kn[s, hq] = kh * lax.rsqrt(jnp.sum(kh * kh, axis=-1, keepdims=True) + NORM_EPS)
        kq = jnp.concatenate([kn[s, hq], qn[s, hq]], axis=0)
        scores = _dot_nt(kq, kn[s, hq])
        kk[s, hq] = jnp.where(strict, scores[:chunk], 0.0)
        qk[s, hq] = scores[chunk:]
        s_pair = jnp.concatenate([s_ref[s, hq * rep + r] for r in range(rep)], axis=1)
        against_state = _dot(kq, s_pair)
        for r in range(rep):
            ks[s, hq * rep + r] = against_state[:chunk, head_cols(0, r)]
            qs[s, hq * rep + r] = against_state[chunk:, head_cols(0, r)]

    gc = {(s, h): gcum[s][:, g_lane + h:g_lane + h + 1] for s, h in units}
    bc = {(s, h): beta[s][:, h:h + 1] for s, h in units}
    decay = {(s, h): jnp.exp2(jnp.where(incl, gc[s, h] - gcum_t[s][g_lane + h:g_lane + h + 1, :], NEG_INF))
             for s, h in units}
    q_pow = {(s, h): kk[s, h // rep] * decay[s, h] * (-bc[s, h]) for s, h in units}
    t_mat = {u: eye + q_pow[u] for u in units}
    q_pow = {u: _dot(q_pow[u], q_pow[u]) for u in units}
    for _ in range(n_pow - 2):
        both = {u: _dot(jnp.concatenate([t_mat[u], q_pow[u]], axis=0), q_pow[u]) for u in units}
        t_mat = {u: t_mat[u] + both[u][:chunk] for u in units}
        q_pow = {u: both[u][chunk:] for u in units}
    t_mat = {u: t_mat[u] + _dot(t_mat[u], q_pow[u]) for u in units}

    egc = {u: jnp.exp2(gc[u]) for u in units}
    v_new = {}
    for s, h in units:
        vh = xc[s][:, head_cols(2 * GDN_KEY_DIM, h)]
        v_new[s, h] = _dot(t_mat[s, h], bc[s, h] * (vh - egc[s, h] * ks[s, h]))
    gated = {}
    for s, h in units:
        o = egc[s, h] * qs[s, h] + _dot(qk[s, h // rep] * decay[s, h], v_new[s, h])
        g_last = gcum[s][chunk - 1:chunk, g_lane + h:g_lane + h + 1]
        s_ref[s, h] = (s_ref[s, h] * jnp.exp2(g_last)
                       + _dot_tn(kn[s, h // rep] * jnp.exp2(g_last - gc[s, h]), v_new[s, h]))
        zh = z_all[s * chunk:(s + 1) * chunk, head_cols(0, h)]
        gated[s, h] = _rms(o, ng) * _silu(zh)
    for h in range(GDN_V_HEADS):
        both = _gather_rows([gated[s, h] for s in seqs], chunk, n_valid, packed)
        o_ref[:, head_cols(0, h)] = both.astype(o_ref.dtype)


def _seq_blocks(n_chunks, chunk, n_valid, row_offset):
    group = SAMPLE_GROUP if n_chunks == 1 else 1
    block_rows = group * n_valid if n_chunks == 1 else chunk
    first = row_offset // block_rows
    return group, block_rows, lambda s, c: (first + s * n_chunks + c, 0)


def _gdn(x, z, ba, conv_w, a_log, dt_bias, norm_g, n_seq, n_chunks, chunk, n_valid, row_offset=0,
         state=None, conv0=None):
    has_state = state is not None
    rows = n_seq * n_chunks * n_valid
    group, block_rows, row_map = _seq_blocks(n_chunks, chunk, n_valid, row_offset)
    blk = lambda w: pl.BlockSpec((block_rows, w), row_map)
    out_blk = pl.BlockSpec((block_rows, GDN_V_DIM), lambda s, c: (s * n_chunks + c, 0))
    pad_heads = lambda v: jnp.pad(v.astype(F32), (GDN_V_HEADS, LANES - 2 * GDN_V_HEADS)).reshape(1, LANES)
    state_spec = pl.BlockSpec((group, GDN_V_HEADS, GDN_HEAD_DIM, GDN_HEAD_DIM), lambda s, c: (s, 0, 0, 0))
    in_specs = [blk(GDN_CONV_DIM), blk(GDN_V_DIM), blk(LANES),
                _const_spec((GDN_CONV, GDN_CONV_DIM)), _const_spec((1, LANES)), _const_spec((1, LANES)),
                _const_spec((1, GDN_HEAD_DIM))]
    args = [x, z, ba, conv_w, pad_heads(a_log), pad_heads(dt_bias), norm_g.reshape(1, GDN_HEAD_DIM)]
    if has_state:
        in_specs += [state_spec, pl.BlockSpec((group, GDN_CONV - 1, GDN_CONV_DIM), lambda s, c: (s, 0, 0))]
        args += [state, conv0]
    return pl.pallas_call(
        functools.partial(_gdn_kernel, chunk=chunk, n_valid=n_valid, has_state=has_state, group=group),
        grid=(n_seq // group, n_chunks),
        in_specs=in_specs,
        out_specs=[out_blk, state_spec,
                   pl.BlockSpec((group, SUBLANES, GDN_CONV_DIM), lambda s, c: (s, 0, 0))],
        out_shape=[jax.ShapeDtypeStruct((rows, GDN_V_DIM), _mixer_out_dtype(block_rows)),
                   jax.ShapeDtypeStruct((n_seq, GDN_V_HEADS, GDN_HEAD_DIM, GDN_HEAD_DIM), F32),
                   jax.ShapeDtypeStruct((n_seq, SUBLANES, GDN_CONV_DIM), F32)],
        scratch_shapes=[pltpu.VMEM((group, SUBLANES, GDN_CONV_DIM), F32)],
        compiler_params=pltpu.CompilerParams(dimension_semantics=("arbitrary", "arbitrary"),
                                             vmem_limit_bytes=VMEM_LIMIT),
        name="gdn",
    )(*args)


def _hgrn_kernel(*refs, chunk, n_valid, has_state, layer, group):
    it = iter(refs)
    x_ref, lbl_ref, ng_ref = next(it), next(it), next(it)
    if has_state:
        s0_ref = next(it)
    o_ref, s_ref = next(it), next(it)
    st = next(it)
    c = pl.program_id(1)

    units = [(s, h) for s in range(group) for h in range(HG_HEADS)]

    @pl.when(c == 0)
    def _():
        for s, h in units:
            st[s, h] = s0_ref[s, h].T if has_state else jnp.zeros((HG_HEAD_DIM, HG_HEAD_DIM), F32)

    logits = [lbl_ref[i:i + 1, :] for i in range(DEPTH)]
    mx = functools.reduce(jnp.maximum, logits)
    ex = [jnp.exp(l - mx) for l in logits]
    tot = functools.reduce(lambda u, v: u + v, ex)
    probs = [e / tot for e in ex]
    lb = functools.reduce(lambda u, v: u + v, probs[:layer + 1]) - probs[0]

    total = group * chunk
    packed = x_ref.shape[0] != total
    x_all = _spread_rows(x_ref[...], group, chunk, n_valid) if packed else x_ref
    row = lax.broadcasted_iota(jnp.int32, (total, HG_DIM), 0)
    live = jnp.bitwise_and(row, chunk - 1) < n_valid
    fz = x_all[:, HG_DIM:2 * HG_DIM]
    log_f = jnp.where(live, jnp.log(lb + (1.0 - lb) * _sigmoid(fz)), 0.0)
    k_all = jnp.where(live, (1.0 - lb) * _sigmoid(-fz), 0.0)
    q_all = _silu(x_all[:, 0:HG_DIM])
    ri = lax.broadcasted_iota(jnp.int32, (total, total), 0)
    rj = lax.broadcasted_iota(jnp.int32, (total, total), 1)
    incl_f = jnp.logical_and(rj <= ri, rj >= ri - jnp.bitwise_and(ri, chunk - 1)).astype(F32)
    b2_all = _dot_f32(incl_f, log_f) * LOG2_E
    ones_l = jnp.ones((LANES, LANES), BF16)
    ng = ng_ref[...]
    n_blk = chunk // SUBLANES
    sub = lax.broadcasted_iota(jnp.int32, (SUBLANES, LANES), 0)

    rows8 = lambda t, bi: t[bi * SUBLANES:(bi + 1) * SUBLANES, :]
    seq_rows = lambda s: slice(s * chunk, (s + 1) * chunk)
    head_cols = lambda base, h: slice(base + h * HG_HEAD_DIM, base + (h + 1) * HG_HEAD_DIM)
    q_h = {(s, h): q_all[seq_rows(s), head_cols(0, h)] for s, h in units}
    k_h = {(s, h): k_all[seq_rows(s), head_cols(0, h)] for s, h in units}
    b_h = {(s, h): b2_all[seq_rows(s), head_cols(0, h)] for s, h in units}
    v_h = {(s, h): x_all[seq_rows(s), head_cols(2 * HG_DIM, h)] for s, h in units}
    seg = [0]
    for bi in range(1, n_blk):
        seg.append(seg[-1] + bi * SUBLANES)

    o_inter, w_sum, p_all = {}, {}, {}
    for u in units:
        q, k, b2 = q_h[u], k_h[u], b_h[u]
        o_inter[u] = _dot_nt(q * jnp.exp2(b2), st[u])
        tiles = []
        for bi in range(n_blk):
            for jj in range(SUBLANES):
                j = bi * SUBLANES + jj
                e = jnp.where(sub >= jj, rows8(b2, bi) - b2[j:j + 1, :], NEG_INF)
                tiles.append(rows8(q, bi) * k[j:j + 1, :] * jnp.exp2(e))
        w_sum[u] = jnp.dot(jnp.concatenate(tiles, axis=0).astype(BF16), ones_l, preferred_element_type=F32)
        if n_blk > 1:
            q_t, k_t = [], []
            for bi in range(1, n_blk):
                r = b2[bi * SUBLANES - 1:bi * SUBLANES, :]
                q_t.append(rows8(q, bi) * jnp.exp2(rows8(b2, bi) - r))
                k_t.append(k[0:bi * SUBLANES, :] * jnp.exp2(r - b2[0:bi * SUBLANES, :]))
            p_all[u] = _dot_nt(jnp.concatenate(q_t, axis=0), jnp.concatenate(k_t, axis=0))

    if n_blk > 1:
        p_shape = (SUBLANES * (n_blk - 1), seg[-1])
        prow = jnp.right_shift(lax.broadcasted_iota(jnp.int32, p_shape, 0), 3)
        pcol = lax.broadcasted_iota(jnp.int32, p_shape, 1)
        own = functools.reduce(jnp.logical_or, [
            jnp.logical_and(prow == bi - 1, jnp.logical_and(pcol >= seg[bi - 1], pcol < seg[bi]))
            for bi in range(1, n_blk)])
    o_intra = {}
    for u in units:
        v = v_h[u]
        o_blocks = []
        for bi in range(n_blk):
            terms = [rows8(w_sum[u], bi * SUBLANES + jj) * v[bi * SUBLANES + jj:bi * SUBLANES + jj + 1, :]
                     for jj in range(SUBLANES)]
            o_blocks.append(functools.reduce(lambda a, w: a + w, terms))
        if n_blk > 1:
            v_t = jnp.concatenate([v[0:bi * SUBLANES, :] for bi in range(1, n_blk)], axis=0)
            o_off = _dot(jnp.where(own, p_all[u], 0.0), v_t)
            for bi in range(1, n_blk):
                o_blocks[bi] = o_blocks[bi] + rows8(o_off, bi - 1)
        o_intra[u] = o_blocks[0] if n_blk == 1 else jnp.concatenate(o_blocks, axis=0)

    gated = {}
    for u in units:
        s, h = u
        k, b2 = k_h[u], b_h[u]
        b_last = b2[chunk - 1:chunk, :]
        st[u] = st[u] * jnp.exp2(b_last) + _dot_tn(v_h[u], k * jnp.exp2(b_last - b2))
        gate = x_all[seq_rows(s), head_cols(3 * HG_DIM, h)]
        gated[u] = _rms(o_inter[u] + o_intra[u], ng) * _silu(gate)
    for h in range(HG_HEADS):
        both = _gather_rows([gated[s, h] for s in range(group)], chunk, n_valid, packed)
        o_ref[:, head_cols(0, h)] = both.astype(o_ref.dtype)

    @pl.when(c == pl.num_programs(1) - 1)
    def _():
        for s, h in units:
            s_ref[s, h] = st[s, h].T


def _hgrn(x, lb_logits, norm_g, layer, n_seq, n_chunks, chunk, n_valid, row_offset=0, state=None):
    has_state = state is not None
    rows = n_seq * n_chunks * n_valid
    group, block_rows, row_map = _seq_blocks(n_chunks, chunk, n_valid, row_offset)
    state_spec = pl.BlockSpec((group, HG_HEADS, HG_HEAD_DIM, HG_HEAD_DIM), lambda s, c: (s, 0, 0, 0))
    in_specs = [pl.BlockSpec((block_rows, 4 * HG_DIM), row_map),
                _const_spec((DEPTH, HG_DIM)), _const_spec((1, HG_HEAD_DIM))]
    args = [x, lb_logits, norm_g.reshape(1, HG_HEAD_DIM)]
    if has_state:
        in_specs.append(state_spec)
        args.append(state)
    return pl.pallas_call(
        functools.partial(_hgrn_kernel, chunk=chunk, n_valid=n_valid, has_state=has_state, layer=layer, group=group),
        grid=(n_seq // group, n_chunks),
        in_specs=in_specs,
        out_specs=[pl.BlockSpec((block_rows, HG_DIM), lambda s, c: (s * n_chunks + c, 0)), state_spec],
        out_shape=[jax.ShapeDtypeStruct((rows, HG_DIM), _mixer_out_dtype(block_rows)),
                   jax.ShapeDtypeStruct((n_seq, HG_HEADS, HG_HEAD_DIM, HG_HEAD_DIM), F32)],
        scratch_shapes=[pltpu.VMEM((group, HG_HEADS, HG_HEAD_DIM, HG_HEAD_DIM), F32)],
        compiler_params=pltpu.CompilerParams(dimension_semantics=("arbitrary", "arbitrary"),
                                             vmem_limit_bytes=VMEM_LIMIT),
        name="hgrn",
    )(*args)


def kernel(x_prompt, x_sample, cache_swa_k, cache_swa_v, state_gdn, state_gdn_conv, state_hgrn, norm_ffn, ffn_w_in, ffn_w_out, norm_mix, swa_w_in, swa_w_out, swa_sinks, gdn_w_in, gdn_conv_w, gdn_a_log, gdn_dt_bias, gdn_norm, gdn_w_out, hgrn_w_in, hgrn_lb_logits, hgrn_norm, hgrn_w_out, final_norm):
    x = (x_prompt.reshape(N_PROMPT, D_MODEL), x_sample.reshape(N_SAMPLE, D_MODEL))
    ffn = functools.partial(_ffn, norm_ffn=norm_ffn.reshape(DEPTH, 2, 1, D_MODEL),
                            w_in=ffn_w_in, w_out=ffn_w_out)
    prompt_tables = _rope_tables(jnp.arange(SEQ), LANES)
    sample_pos = PAST_LEN + jnp.arange(DEC_SEQ)
    window_cols = lambda c: jnp.transpose(c, (0, 1, 3, 4, 2)).reshape(c.shape[0], DEC_BATCH, SWA_KV, WINDOW)
    window_rows = lambda c: jnp.transpose(c.reshape(DEC_BATCH, SWA_KV_HEADS, SWA_HEAD_DIM, WINDOW), (0, 3, 1, 2))
    cache_kt, cache_vt = window_cols(cache_swa_k), window_cols(cache_swa_v)

    outs = {k: [] for k in ("pk", "pv", "pg", "pc", "ph", "sk", "sv", "sg", "sc", "sh")}
    proj = None
    for i in range(DEPTH):
        kind, j = i % N_MIXERS, i // N_MIXERS
        if proj is not None:
            x = ffn(x, i - 1, 1, proj=proj)
        x = ffn(x, i, 0)
        if kind == 0:
            (qkv,) = _norm_proj(x, norm_mix[i], swa_w_in, j, [(0, SWA_IN, SWA_IN)], TOKEN_TILE)
            o_p, pk, pv = _swa_prompt(qkv, swa_sinks[j], prompt_tables)
            o_s, sk, sv = _swa_sample(qkv[N_PROMPT:], j, cache_kt, cache_vt, swa_sinks[j], sample_pos)
            kv_shape = (WINDOW, SWA_KV_HEADS, SWA_HEAD_DIM)
            outs["pk"].append(pk.reshape((BATCH,) + kv_shape))
            outs["pv"].append(pv.reshape((BATCH,) + kv_shape))
            outs["sk"].append(window_rows(sk))
            outs["sv"].append(window_rows(sv))
            proj = (o_p, o_s, swa_w_out[j].astype(BF16))
        elif kind == 1:
            segments = [(0, GDN_CONV_DIM, GDN_CONV_DIM), (GDN_CONV_DIM, GDN_V_DIM, GDN_V_DIM),
                        (GDN_CONV_DIM + GDN_V_DIM, 2 * GDN_V_HEADS, LANES)]
            qkv, z, ba = _norm_proj(x, norm_mix[i], gdn_w_in, j, segments, TOKEN_TILE // 2)
            o_p, pg, p_tail = _gdn(qkv, z, ba, gdn_conv_w[j], gdn_a_log[j], gdn_dt_bias[j], gdn_norm[j],
                                   BATCH, SEQ // GDN_CHUNK, GDN_CHUNK, GDN_CHUNK)
            o_s, sg, s_tail = _gdn(qkv, z, ba, gdn_conv_w[j], gdn_a_log[j], gdn_dt_bias[j], gdn_norm[j],
                                   DEC_BATCH, 1, PAD_CHUNK, DEC_SEQ, row_offset=N_PROMPT,
                                   state=state_gdn[j], conv0=state_gdn_conv[j])
            outs["pg"].append(pg)
            outs["sg"].append(sg)
            keep = GDN_CONV - 1
            outs["pc"].append(p_tail[:, SUBLANES - keep:])
            outs["sc"].append(s_tail[:, DEC_SEQ - keep:DEC_SEQ])
            proj = (o_p, o_s.astype(BF16), gdn_w_out[j].astype(BF16))
        else:
            (xin,) = _norm_proj(x, norm_mix[i], hgrn_w_in, j, [(0, 4 * HG_DIM, 4 * HG_DIM)], TOKEN_TILE)
            o_p, ph = _hgrn(xin, hgrn_lb_logits, hgrn_norm[j], i, BATCH, SEQ // HG_CHUNK, HG_CHUNK, HG_CHUNK)
            o_s, sh = _hgrn(xin, hgrn_lb_logits, hgrn_norm[j], i, DEC_BATCH, 1, PAD_CHUNK, DEC_SEQ,
                            row_offset=N_PROMPT, state=state_hgrn[j])
            outs["ph"].append(ph)
            outs["sh"].append(sh)
            proj = (o_p, o_s.astype(BF16), hgrn_w_out[j].astype(BF16))
    y_prompt, y_sample = ffn(x, DEPTH - 1, 1, proj=proj, final_g=final_norm)
    y_prompt = y_prompt.reshape(BATCH, SEQ, D_MODEL)
    y_sample = y_sample.reshape(DEC_BATCH, DEC_SEQ, D_MODEL)
    st = lambda k: jnp.stack(outs[k])
    return (y_prompt, y_sample, st("pk"), st("pv"), st("pg"), st("pc"), st("ph"),
            st("sk"), st("sv"), st("sg"), st("sc"), st("sh"))
```

```python
import functools
import math

import jax
import jax.numpy as jnp
from jax import lax
from jax.experimental import pallas as pl
from jax.experimental.pallas import tpu as pltpu

D_MODEL = 1024
BATCH = 8
SEQ = 2048
DEPTH = 4
DEC_BATCH = 128
DEC_SEQ = 4
PAST_LEN = 8192
N_MIXERS = 3

SWA_HEAD_DIM = 64
SWA_HEADS = 16
SWA_KV_HEADS = 4
SWA_GROUP = 4
WINDOW = 128
ROT_DIM = 16
ROPE_THETA = 500000.0
SWA_Q = SWA_HEADS * SWA_HEAD_DIM
SWA_KV = SWA_KV_HEADS * SWA_HEAD_DIM
SWA_IN = SWA_Q + 2 * SWA_KV

GDN_HEAD_DIM = 128
GDN_QK_HEADS = 8
GDN_V_HEADS = 16
GDN_KEY_DIM = 1024
GDN_V_DIM = 2048
GDN_CONV_DIM = 4096
GDN_CONV = 4

HG_HEAD_DIM = 128
HG_HEADS = 8
HG_DIM = 1024

D_FF = 2816
NORM_EPS = 1e-6
NEG_INF = -1e30
LOG2_E = 1.4426950408889634

LANES = 128
SUBLANES = 8
TOKEN_TILE = 512
N_PROMPT = BATCH * SEQ
N_SAMPLE = DEC_BATCH * DEC_SEQ
N_TOKENS = N_PROMPT + N_SAMPLE
N_PROMPT_TILES = N_PROMPT // TOKEN_TILE
FFN_CHUNK = 256
VMEM_LIMIT = 56 * 1024 * 1024
GDN_CHUNK = 128
GDN_INV_BLOCK = 64
HG_CHUNK = 64
PAD_CHUNK = 8
SWA_SAMPLE_BLOCK = 8
SAMPLE_GROUP = 2

F32 = jnp.float32
BF16 = jnp.bfloat16


def _const_spec(shape):
    nd = len(shape)
    return pl.BlockSpec(shape, lambda *_: (0,) * nd, pipeline_mode=pl.Buffered(1))


def _rms(x, g):
    return x * lax.rsqrt(jnp.mean(x * x, axis=-1, keepdims=True) + NORM_EPS) * g


def _sigmoid(x):
    return 1.0 / (1.0 + jnp.exp(-x))


def _silu(x):
    return x * _sigmoid(x)


def _dot(a, b):
    return jnp.dot(a.astype(BF16), b.astype(BF16), preferred_element_type=F32)


def _dot_nt(a, b):
    return lax.dot_general(a.astype(BF16), b.astype(BF16), (((1,), (1,)), ((), ())),
                           preferred_element_type=F32)


def _dot_tn(a, b):
    return lax.dot_general(a.astype(BF16), b.astype(BF16), (((0,), (0,)), ((), ())),
                           preferred_element_type=F32)


def _dot_f32(a, b):
    return jnp.dot(a, b, preferred_element_type=F32, precision=lax.Precision.HIGHEST)


def _ffn_kernel(*refs, split_in, has_proj, final_norm):
    it = iter(refs)
    is_prompt = pl.program_id(0) < N_PROMPT_TILES
    if split_in:
        xp_ref, xs_ref = next(it), next(it)
        x = jnp.where(is_prompt, xp_ref[...], xs_ref[...])
    else:
        x = next(it)[...]
    if has_proj:
        op_ref, os_ref, wp_ref = next(it), next(it), next(it)
    g_ref, wg_ref, wu_ref, wo_ref = next(it), next(it), next(it), next(it)
    gf_ref = next(it) if final_norm else None
    out_refs = list(it)

    if has_proj:
        o = jnp.where(is_prompt, op_ref[...], os_ref[...])
        x = x + jnp.dot(o, wp_ref[...], preferred_element_type=F32)
    h = _rms(x, g_ref[...]).astype(BF16)
    y = jnp.zeros_like(x)
    for c in range(D_FF // FFN_CHUNK):
        cols = slice(c * FFN_CHUNK, (c + 1) * FFN_CHUNK)
        gate = jnp.dot(h, wg_ref[:, cols].astype(BF16), preferred_element_type=F32)
        up = jnp.dot(h, wu_ref[:, cols].astype(BF16), preferred_element_type=F32)
        act = (_silu(gate) * up).astype(BF16)
        y = y + jnp.dot(act, wo_ref[cols, :].astype(BF16), preferred_element_type=F32)
    x = x + 0.5 * y
    if final_norm:
        x = _rms(x, gf_ref[...])
    if len(out_refs) == 1:
        out_refs[0][...] = x
    else:
        @pl.when(is_prompt)
        def _():
            out_refs[0][...] = x

        @pl.when(jnp.logical_not(is_prompt))
        def _():
            out_refs[1][...] = x


def _ffn(x, layer, which, norm_ffn, w_in, w_out, proj=None, final_g=None):
    tile = pl.BlockSpec((TOKEN_TILE, D_MODEL), lambda i: (i, 0))
    prompt_tile = lambda w: pl.BlockSpec((TOKEN_TILE, w), lambda i: (jnp.minimum(i, N_PROMPT_TILES - 1), 0))
    sample_tile = lambda w: pl.BlockSpec((TOKEN_TILE, w), lambda i: (0, 0))
    split_in = isinstance(x, tuple)
    in_specs = [prompt_tile(D_MODEL), sample_tile(D_MODEL)] if split_in else [tile]
    args = list(x) if split_in else [x]
    if proj is not None:
        o_p, o_s, w_p = proj
        k = w_p.shape[0]
        in_specs += [prompt_tile(k), sample_tile(k), _const_spec((k, D_MODEL))]
        args += [o_p, o_s, w_p]
    once = pl.Buffered(1)
    in_specs += [pl.BlockSpec((None, None, 1, D_MODEL), lambda i: (layer, which, 0, 0), pipeline_mode=once),
                 pl.BlockSpec((None, None, D_MODEL, D_FF), lambda i: (layer, which, 0, 0), pipeline_mode=once),
                 pl.BlockSpec((None, None, D_MODEL, D_FF), lambda i: (layer, which, 0, 1), pipeline_mode=once),
                 pl.BlockSpec((None, None, D_FF, D_MODEL), lambda i: (layer, which, 0, 0), pipeline_mode=once)]
    args += [norm_ffn, w_in, w_in, w_out]
    if final_g is not None:
        in_specs.append(_const_spec((1, D_MODEL)))
        args.append(final_g.reshape(1, D_MODEL))
        out_specs = [prompt_tile(D_MODEL), sample_tile(D_MODEL)]
        out_shape = [jax.ShapeDtypeStruct((N_PROMPT, D_MODEL), F32), jax.ShapeDtypeStruct((N_SAMPLE, D_MODEL), F32)]
    else:
        out_specs = tile
        out_shape = jax.ShapeDtypeStruct((N_TOKENS, D_MODEL), F32)
    return pl.pallas_call(
        functools.partial(_ffn_kernel, split_in=split_in, has_proj=proj is not None, final_norm=final_g is not None),
        grid=(N_TOKENS // TOKEN_TILE,),
        in_specs=in_specs,
        out_specs=out_specs,
        out_shape=out_shape,
        compiler_params=pltpu.CompilerParams(dimension_semantics=("arbitrary",), vmem_limit_bytes=VMEM_LIMIT),
        name="ffn",
    )(*args)


def _norm_proj_kernel(x_ref, g_ref, w_ref, *out_refs, segments):
    h = _rms(x_ref[...], g_ref[...]).astype(BF16)
    for (start, width, _), out_ref in zip(segments, out_refs):
        y = jnp.dot(h, w_ref[:, start:start + width].astype(BF16), preferred_element_type=F32)
        if width == out_ref.shape[1]:
            out_ref[...] = y
        else:
            out_ref[...] = jnp.zeros_like(out_ref)
            out_ref[:, 0:width] = y


def _norm_proj(x, g, w, layer, segments, tile_rows):
    once = pl.Buffered(1)
    in_specs = [pl.BlockSpec((tile_rows, D_MODEL), lambda i: (i, 0)), _const_spec((1, D_MODEL)),
                pl.BlockSpec((None,) + w.shape[1:], lambda i: (layer, 0, 0), pipeline_mode=once)]
    return pl.pallas_call(
        functools.partial(_norm_proj_kernel, segments=segments),
        grid=(N_TOKENS // tile_rows,),
        in_specs=in_specs,
        out_specs=[pl.BlockSpec((tile_rows, out_w), lambda i: (i, 0)) for _, _, out_w in segments],
        out_shape=[jax.ShapeDtypeStruct((N_TOKENS, out_w), F32) for _, _, out_w in segments],
        compiler_params=pltpu.CompilerParams(dimension_semantics=("arbitrary",), vmem_limit_bytes=VMEM_LIMIT),
        name="norm_proj",
    )(x, g.reshape(1, D_MODEL), w)


def _rope_tables(pos, width):
    half = ROT_DIM // 2
    inv_freq = ROPE_THETA ** (-jnp.arange(0, ROT_DIM, 2, dtype=F32) / ROT_DIM)
    ang = pos.astype(F32)[:, None] * inv_freq[None, :]
    cos, sin = jnp.cos(ang), jnp.sin(ang)
    n = pos.shape[0]
    rest = SWA_HEAD_DIM - ROT_DIM
    c = jnp.concatenate([cos, cos, jnp.ones((n, rest), F32)], axis=1)
    s_lo = jnp.concatenate([-sin, jnp.zeros((n, half + rest), F32)], axis=1)
    s_hi = jnp.concatenate([jnp.zeros((n, half), F32), sin, jnp.zeros((n, rest), F32)], axis=1)
    reps = width // SWA_HEAD_DIM
    return tuple(jnp.tile(t, (1, reps)) for t in (c, s_lo, s_hi))


def _rope(x, c, s_lo, s_hi):
    half = ROT_DIM // 2
    outs = []
    for j in range(x.shape[1] // LANES):
        xs = x[:, j * LANES:(j + 1) * LANES]
        outs.append(xs * c + pltpu.roll(xs, LANES - half, 1) * s_lo + pltpu.roll(xs, half, 1) * s_hi)
    return outs[0] if len(outs) == 1 else jnp.concatenate(outs, axis=1)


def _swa_prompt_kernel(sink_ref, q_ref, k_ref, v_ref, c_ref, slo_ref, shi_ref,
                       o_ref, kc_ref, vc_ref, kprev, vprev):
    n = pl.program_id(1)

    @pl.when(n == 0)
    def _():
        kprev[...] = jnp.zeros_like(kprev)
        vprev[...] = jnp.zeros_like(vprev)

    c, s_lo, s_hi = c_ref[...], slo_ref[...], shi_ref[...]
    q = _rope(q_ref[...], c, s_lo, s_hi) * (SWA_HEAD_DIM ** -0.5 * LOG2_E)
    k_cur = _rope(k_ref[...], c, s_lo, s_hi)
    v_cur = v_ref[...]
    slot0 = lax.broadcasted_iota(jnp.int32, (2 * WINDOW, 1), 0) == 0
    keys = jnp.where(slot0, 0.0, jnp.concatenate([kprev[...], k_cur], axis=0))
    vals = jnp.where(slot0, 0.0, jnp.concatenate([vprev[...], v_cur], axis=0))
    qi = lax.broadcasted_iota(jnp.int32, (WINDOW, 2 * WINDOW), 0)
    kj = lax.broadcasted_iota(jnp.int32, (WINDOW, 2 * WINDOW), 1)
    low = jnp.maximum(qi, jnp.where(n > 0, 0, WINDOW - 1))
    bias_one = jnp.where(jnp.logical_and(kj > low, kj <= qi + WINDOW), 0.0, NEG_INF)
    ones = jnp.ones((2 * WINDOW, LANES), BF16)

    low_half = lax.broadcasted_iota(jnp.int32, (1, LANES), 1) < SWA_HEAD_DIM

    def both_halves(t):
        out = []
        for grp in range(SWA_KV // LANES):
            g = t[:, grp * LANES:(grp + 1) * LANES]
            swapped = pltpu.roll(g, SWA_HEAD_DIM, 1)
            out += [jnp.where(low_half, g, swapped), jnp.where(low_half, swapped, g)]
        return out

    keys2, vals2 = both_halves(keys), both_halves(vals)
    scores = []
    for kv in range(SWA_KV_HEADS):
        heads = range(kv * SWA_GROUP, (kv + 1) * SWA_GROUP)
        parts, bias = [], []
        for h in heads:
            g = q[:, (h // 2) * LANES:(h // 2 + 1) * LANES]
            parts.append(jnp.where(low_half, g, 0.0) if h % 2 == 0 else jnp.where(low_half, 0.0, g))
            bias.append(jnp.where(kj == 0, sink_ref[h] * LOG2_E, bias_one))
        scores.append(_dot_nt(jnp.concatenate(parts, axis=0), keys2[kv])
                      + jnp.concatenate(bias, axis=0))
    outs = []
    for kv in range(SWA_KV_HEADS):
        s = scores[kv]
        p = jnp.exp2(s - jnp.max(s, axis=-1, keepdims=True)).astype(BF16)
        denom = jnp.dot(p, ones, preferred_element_type=F32)
        outs.append(_dot(p, vals2[kv]) / denom)
    for kv in range(SWA_KV_HEADS):
        for pair in range(SWA_GROUP // 2):
            even, odd = (outs[kv][g * WINDOW:(g + 1) * WINDOW] for g in (2 * pair, 2 * pair + 1))
            grp = (kv * SWA_GROUP) // 2 + pair
            o_ref[:, grp * LANES:(grp + 1) * LANES] = jnp.where(low_half, even, odd).astype(o_ref.dtype)

    kprev[...] = k_cur
    vprev[...] = v_cur

    @pl.when(n == pl.num_programs(1) - 1)
    def _():
        kc_ref[0] = k_cur
        vc_ref[0] = v_cur


def _swa_prompt(qkv, sinks, tables):
    nb = SEQ // WINDOW
    q_blocks = SWA_Q // SWA_KV
    tab = pl.BlockSpec((WINDOW, LANES), lambda b, n, *_: (n, 0))
    cache = pl.BlockSpec((1, WINDOW, SWA_KV), lambda b, n, *_: (b, 0, 0))
    return pl.pallas_call(
        _swa_prompt_kernel,
        grid_spec=pltpu.PrefetchScalarGridSpec(
            num_scalar_prefetch=1,
            grid=(BATCH, nb),
            in_specs=[pl.BlockSpec((WINDOW, SWA_Q), lambda b, n, *_: (b * nb + n, 0)),
                      pl.BlockSpec((WINDOW, SWA_KV), lambda b, n, *_: (b * nb + n, q_blocks)),
                      pl.BlockSpec((WINDOW, SWA_KV), lambda b, n, *_: (b * nb + n, q_blocks + 1)),
                      tab, tab, tab],
            out_specs=[pl.BlockSpec((WINDOW, SWA_Q), lambda b, n, *_: (b * nb + n, 0)), cache, cache],
            scratch_shapes=[pltpu.VMEM((WINDOW, SWA_KV), F32), pltpu.VMEM((WINDOW, SWA_KV), F32)]),
        out_shape=[jax.ShapeDtypeStruct((N_PROMPT, SWA_Q), BF16),
                   jax.ShapeDtypeStruct((BATCH, WINDOW, SWA_KV), F32),
                   jax.ShapeDtypeStruct((BATCH, WINDOW, SWA_KV), F32)],
        compiler_params=pltpu.CompilerParams(dimension_semantics=("arbitrary", "arbitrary")),
        name="swa_prompt",
    )(sinks, qkv, qkv, qkv, *tables)


def _swa_sample_kernel(q_ref, kn_ref, vn_ref, kt_ref, vt_ref, sink_ref,
                       qc_ref, qlo_ref, qhi_ref, kc_tab, klo_tab, khi_tab,
                       o_ref, kto_ref, vto_ref, knew, vnew):
    rows = SWA_KV_HEADS * DEC_SEQ * SWA_GROUP
    token = lambda w: jnp.bitwise_and(jnp.right_shift(lax.broadcasted_iota(jnp.int32, (rows, w), 0), 2), DEC_SEQ - 1)
    valid_old = lax.broadcasted_iota(jnp.int32, (rows, WINDOW), 1) > token(WINDOW)
    valid_new = lax.broadcasted_iota(jnp.int32, (rows, PAD_CHUNK), 1) <= token(PAD_CHUNK)
    sink = sink_ref[:, 0:1]
    lane_head = jnp.right_shift(lax.broadcasted_iota(jnp.int32, (DEC_SEQ * SWA_GROUP, SWA_KV), 1), 6)
    zero_tail = jnp.zeros((PAD_CHUNK - DEC_SEQ, SWA_KV), F32)
    slot = lax.broadcasted_iota(jnp.int32, (SWA_KV, WINDOW), 1)
    place = (lax.broadcasted_iota(jnp.int32, (PAD_CHUNK, WINDOW), 1)
             == lax.broadcasted_iota(jnp.int32, (PAD_CHUNK, WINDOW), 0) + (WINDOW - DEC_SEQ)).astype(BF16)

    def shifted(old_t, new_rows):
        hi = new_rows.astype(BF16)
        rest = new_rows - hi.astype(F32)
        mid = rest.astype(BF16)
        lo = (rest - mid.astype(F32)).astype(BF16)
        appended = _dot_tn(hi, place) + _dot_tn(mid, place) + _dot_tn(lo, place)
        return jnp.where(slot >= WINDOW - DEC_SEQ, appended, pltpu.roll(old_t, WINDOW - DEC_SEQ, 1))

    seqs = range(SWA_SAMPLE_BLOCK)
    for b in seqs:
        knew[b, 0:DEC_SEQ, :] = _rope(kn_ref[b], kc_tab[...], klo_tab[...], khi_tab[...])
        vnew[b, 0:DEC_SEQ, :] = vn_ref[b]
        knew[b, DEC_SEQ:PAD_CHUNK, :] = zero_tail
        vnew[b, DEC_SEQ:PAD_CHUNK, :] = zero_tail
    k_new = [knew[b] for b in seqs]
    v_new = [vnew[b] for b in seqs]
    q = [_rope(q_ref[b], qc_ref[...], qlo_ref[...], qhi_ref[...]) * (SWA_HEAD_DIM ** -0.5) for b in seqs]
    s_old = [jnp.where(valid_old, _dot(q[b], kt_ref[b]), NEG_INF) for b in seqs]
    s_new = [jnp.where(valid_new, _dot_nt(q[b], k_new[b]), NEG_INF) for b in seqs]
    m = [jnp.maximum(jnp.maximum(jnp.max(s_old[b], axis=-1, keepdims=True),
                                 jnp.max(s_new[b], axis=-1, keepdims=True)), sink) for b in seqs]
    p_old = [jnp.exp(s_old[b] - m[b]) for b in seqs]
    p_new = [jnp.exp(s_new[b] - m[b]) for b in seqs]
    per_head = DEC_SEQ * SWA_GROUP
    for b in seqs:
        denom = (jnp.sum(p_old[b], axis=-1, keepdims=True) + jnp.sum(p_new[b], axis=-1, keepdims=True)
                 + jnp.exp(sink - m[b]))
        o_all = (_dot_nt(p_old[b], vt_ref[b]) + _dot(p_new[b], v_new[b])) / denom
        o = jnp.zeros((per_head, SWA_KV), F32)
        for h in range(SWA_KV_HEADS):
            o = o + jnp.where(lane_head == h, o_all[h * per_head:(h + 1) * per_head, :], 0.0)
        o_ref[b] = o.astype(o_ref.dtype)
    for b in seqs:
        kto_ref[b] = shifted(kt_ref[b], k_new[b])
        vto_ref[b] = shifted(vt_ref[b], v_new[b])


def _swa_sample(qkv_s, layer, cache_kt, cache_vt, sinks, pos):
    q = qkv_s[:, :SWA_Q].reshape(DEC_BATCH, DEC_SEQ, SWA_KV_HEADS, SWA_GROUP, SWA_HEAD_DIM)
    q = jnp.transpose(q, (0, 2, 1, 3, 4))
    eye = jnp.eye(SWA_KV_HEADS, dtype=F32)
    rows = SWA_KV_HEADS * DEC_SEQ * SWA_GROUP
    q_bd = (q[:, :, :, :, None, :] * eye[None, :, None, None, :, None]).reshape(DEC_BATCH, rows, SWA_KV)
    k_new = qkv_s[:, SWA_Q:SWA_Q + SWA_KV].reshape(DEC_BATCH, DEC_SEQ, SWA_KV)
    v_new = qkv_s[:, SWA_Q + SWA_KV:].reshape(DEC_BATCH, DEC_SEQ, SWA_KV)
    row_t = (jnp.arange(rows) // SWA_GROUP) % DEC_SEQ
    q_tabs = _rope_tables(pos[row_t], LANES)
    k_tabs = _rope_tables(pos, LANES)
    row_head = (jnp.arange(rows) // (DEC_SEQ * SWA_GROUP)) * SWA_GROUP + jnp.arange(rows) % SWA_GROUP
    sink_rows = jnp.broadcast_to(sinks[row_head][:, None], (rows, LANES))

    nblk = SWA_SAMPLE_BLOCK
    blk = lambda r, c: pl.BlockSpec((nblk, r, c), lambda i: (i, 0, 0))
    window_in = pl.BlockSpec((None, nblk, SWA_KV, WINDOW), lambda i: (layer, i, 0, 0))
    per_head = DEC_SEQ * SWA_GROUP
    o, ko, vo = pl.pallas_call(
        _swa_sample_kernel,
        grid=(DEC_BATCH // nblk,),
        in_specs=[blk(rows, SWA_KV), blk(DEC_SEQ, SWA_KV), blk(DEC_SEQ, SWA_KV),
                  window_in, window_in, _const_spec((rows, LANES)),
                  _const_spec((rows, LANES)), _const_spec((rows, LANES)), _const_spec((rows, LANES)),
                  _const_spec((DEC_SEQ, LANES)), _const_spec((DEC_SEQ, LANES)), _const_spec((DEC_SEQ, LANES))],
        out_specs=[blk(per_head, SWA_KV), blk(SWA_KV, WINDOW), blk(SWA_KV, WINDOW)],
        out_shape=[jax.ShapeDtypeStruct((DEC_BATCH, per_head, SWA_KV), BF16),
                   jax.ShapeDtypeStruct((DEC_BATCH, SWA_KV, WINDOW), F32),
                   jax.ShapeDtypeStruct((DEC_BATCH, SWA_KV, WINDOW), F32)],
        scratch_shapes=[pltpu.VMEM((nblk, PAD_CHUNK, SWA_KV), F32), pltpu.VMEM((nblk, PAD_CHUNK, SWA_KV), F32)],
        compiler_params=pltpu.CompilerParams(dimension_semantics=("arbitrary",)),
        name="swa_sample",
    )(q_bd, k_new, v_new, cache_kt, cache_vt, sink_rows, *q_tabs, *k_tabs)
    o = o.reshape(DEC_BATCH, DEC_SEQ, SWA_GROUP, SWA_KV_HEADS, SWA_HEAD_DIM)
    o = jnp.transpose(o, (0, 1, 3, 2, 4)).reshape(N_SAMPLE, SWA_Q)
    return o, ko, vo


def _mixer_out_dtype(chunk):
    return BF16 if chunk % (2 * SUBLANES) == 0 else F32


def _tri(n, strict=False):
    i = lax.broadcasted_iota(jnp.int32, (n, n), 0)
    j = lax.broadcasted_iota(jnp.int32, (n, n), 1)
    return (j < i) if strict else (j <= i)


def _spread_rows(blk, group, chunk, n_valid):
    if blk.shape[0] == group * chunk:
        return blk
    return jnp.concatenate([blk if s == 0 else pltpu.roll(blk, chunk - s * n_valid, 0) for s in range(group)], axis=0)


def _gather_rows(parts, chunk, n_valid, packed):
    if not packed:
        return jnp.concatenate(parts, axis=0) if len(parts) > 1 else parts[0]
    row = lax.broadcasted_iota(jnp.int32, (chunk, 1), 0)
    out = parts[0]
    for s in range(1, len(parts)):
        out = jnp.where(row >= s * n_valid, pltpu.roll(parts[s], s * n_valid, 0), out)
    return out


def _gdn_kernel(*refs, chunk, n_valid, has_state, group):
    it = iter(refs)
    x_ref, z_ref, ba_ref = next(it), next(it), next(it)
    cw_ref, alog_ref, dtb_ref, ng_ref = next(it), next(it), next(it), next(it)
    if has_state:
        s0_ref, c0_ref = next(it), next(it)
    o_ref, s_ref, tail_ref = next(it), next(it), next(it)
    prev = next(it)
    g_lane = GDN_V_HEADS
    c = pl.program_id(1)
    seqs = range(group)

    @pl.when(c == 0)
    def _():
        if has_state:
            s_ref[...] = s0_ref[...]
            for s in seqs:
                prev[s, 0:SUBLANES - GDN_CONV + 1, :] = jnp.zeros((SUBLANES - GDN_CONV + 1, GDN_CONV_DIM), F32)
                prev[s, SUBLANES - GDN_CONV + 1:SUBLANES, :] = c0_ref[s]
        else:
            s_ref[...] = jnp.zeros_like(s_ref)
            prev[...] = jnp.zeros_like(prev)

    incl = _tri(chunk)
    strict = _tri(chunk, strict=True)
    incl_f = incl.astype(F32)
    upper_f = (lax.broadcasted_iota(jnp.int32, (chunk, chunk), 0)
               <= lax.broadcasted_iota(jnp.int32, (chunk, chunk), 1)).astype(F32)
    eye = (lax.broadcasted_iota(jnp.int32, (chunk, chunk), 0)
           == lax.broadcasted_iota(jnp.int32, (chunk, chunk), 1)).astype(F32)
    live = lax.broadcasted_iota(jnp.int32, (chunk, LANES), 0) < n_valid
    two_halves = chunk == 2 * GDN_INV_BLOCK
    n_pow = max(2, math.ceil(math.log2(min(chunk, n_valid, GDN_INV_BLOCK))))
    same_half = (lax.broadcasted_iota(jnp.int32, (chunk, chunk), 0) >= GDN_INV_BLOCK) == (
        lax.broadcasted_iota(jnp.int32, (chunk, chunk), 1) >= GDN_INV_BLOCK)
    ng = ng_ref[...]
    rep = GDN_V_HEADS // GDN_QK_HEADS
    head_cols = lambda base, h: slice(base + h * GDN_HEAD_DIM, base + (h + 1) * GDN_HEAD_DIM)

    packed = x_ref.shape[0] != group * chunk
    x_all, z_all, ba_all = (_spread_rows(r[...], group, chunk, n_valid) for r in (x_ref, z_ref, ba_ref))

    xc, beta, gcum, gcum_t = [], [], [], []
    for s in seqs:
        rows = slice(s * chunk, (s + 1) * chunk)
        x = x_all[rows, :]
        full = jnp.concatenate([prev[s], x], axis=0)
        acc = x * cw_ref[GDN_CONV - 1:GDN_CONV, :]
        for tap in range(GDN_CONV - 1):
            back = GDN_CONV - 1 - tap
            acc = acc + pltpu.roll(full, back, 0)[SUBLANES:, :] * cw_ref[tap:tap + 1, :]
        prev[s] = x[chunk - SUBLANES:chunk, :]
        tail_ref[s] = x[chunk - SUBLANES:chunk, :]
        xc.append(_silu(acc))
        beta.append(jnp.where(live, _sigmoid(ba_all[rows, :]), 0.0))
        a = ba_all[rows, :] + dtb_ref[...]
        softplus = jnp.maximum(a, 0.0) + jnp.log(1.0 + jnp.exp(-jnp.abs(a)))
        g = jnp.where(live, -jnp.exp(alog_ref[...]) * softplus, 0.0) * LOG2_E
        gcum.append(_dot_f32(incl_f, g))
        gcum_t.append(lax.dot_general(g, upper_f, (((0,), (0,)), ((), ())), preferred_element_type=F32,
                                      precision=lax.Precision.HIGHEST))

    pairs = [(s, hq) for s in seqs for hq in range(GDN_QK_HEADS)]
    units = [(s, h) for s in seqs for h in range(GDN_V_HEADS)]
    qn, kn, kk, qk, ks, qs = {}, {}, {}, {}, {}, {}
    for s, hq in pairs:
        qh = xc[s][:, head_cols(0, hq)]
        kh = xc[s][:, head_cols(GDN_KEY_DIM, hq)]
        qn[s, hq] = qh * lax.rsqrt(jnp.sum(qh * qh, axis=-1, keepdims=True) + NORM_EPS) * (GDN_HEAD_DIM ** -0.5)
        kn[s, hq] = kh * lax.rsqrt(jnp.sum(kh * kh, axis=-1, keepdims=True) + NORM_EPS)
        kq = jnp.concatenate([kn[s, hq], qn[s, hq]], axis=0)
        scores = _dot_nt(kq, kn[s, hq])
        kk[s, hq] = jnp.where(strict, scores[:chunk], 0.0)
        qk[s, hq] = scores[chunk:]
        s_pair = jnp.concatenate([s_ref[s, hq * rep + r] for r in range(rep)], axis=1)
        against_state = _dot(kq, s_pair)
        for r in range(rep):
            ks[s, hq * rep + r] = against_state[:chunk, head_cols(0, r)]
            qs[s, hq * rep + r] = against_state[chunk:, head_cols(0, r)]

    gc = {(s, h): gcum[s][:, g_lane + h:g_lane + h + 1] for s, h in units}
    bc = {(s, h): beta[s][:, h:h + 1] for s, h in units}
    decay = {(s, h): jnp.exp2(jnp.where(incl, gc[s, h] - gcum_t[s][g_lane + h:g_lane + h + 1, :], NEG_INF))
             for s, h in units}
    q_pow = {(s, h): kk[s, h // rep] * decay[s, h] * (-bc[s, h]) for s, h in units}
    if two_halves:
        q_low = {u: jnp.where(same_half, 0.0, q_pow[u]) for u in units}
        q_pow = {u: jnp.where(same_half, q_pow[u], 0.0) for u in units}
    t_mat = {u: eye + q_pow[u] for u in units}
    q_pow = {u: _dot(q_pow[u], q_pow[u]) for u in units}
    for _ in range(n_pow - 2):
        both = {u: _dot(jnp.concatenate([t_mat[u], q_pow[u]], axis=0), q_pow[u]) for u in units}
        t_mat = {u: t_mat[u] + both[u][:chunk] for u in units}
        q_pow = {u: both[u][chunk:] for u in units}
    t_mat = {u: t_mat[u] + _dot(t_mat[u], q_pow[u]) for u in units}
    if two_halves:
        t_mat = {u: t_mat[u] + _dot(_dot(t_mat[u], q_low[u]), t_mat[u]) for u in units}

    egc = {u: jnp.exp2(gc[u]) for u in units}
    v_new = {}
    for s, h in units:
        vh = xc[s][:, head_cols(2 * GDN_KEY_DIM, h)]
        v_new[s, h] = _dot(t_mat[s, h], bc[s, h] * (vh - egc[s, h] * ks[s, h]))
    gated = {}
    for s, h in units:
        o = egc[s, h] * qs[s, h] + _dot(qk[s, h // rep] * decay[s, h], v_new[s, h])
        g_last = gcum[s][chunk - 1:chunk, g_lane + h:g_lane + h + 1]
        s_ref[s, h] = (s_ref[s, h] * jnp.exp2(g_last)
                       + _dot_tn(kn[s, h // rep] * jnp.exp2(g_last - gc[s, h]), v_new[s, h]))
        zh = z_all[s * chunk:(s + 1) * chunk, head_cols(0, h)]
        gated[s, h] = _rms(o, ng) * _silu(zh)
    for h in range(GDN_V_HEADS):
        both = _gather_rows([gated[s, h] for s in seqs], chunk, n_valid, packed)
        o_ref[:, head_cols(0, h)] = both.astype(o_ref.dtype)


def _seq_blocks(n_chunks, chunk, n_valid, row_offset):
    group = SAMPLE_GROUP if n_chunks == 1 else 1
    block_rows = group * n_valid if n_chunks == 1 else chunk
    first = row_offset // block_rows
    return group, block_rows, lambda s, c: (first + s * n_chunks + c, 0)


def _gdn(x, z, ba, conv_w, a_log, dt_bias, norm_g, n_seq, n_chunks, chunk, n_valid, row_offset=0,
         state=None, conv0=None):
    has_state = state is not None
    rows = n_seq * n_chunks * n_valid
    group, block_rows, row_map = _seq_blocks(n_chunks, chunk, n_valid, row_offset)
    blk = lambda w: pl.BlockSpec((block_rows, w), row_map)
    out_blk = pl.BlockSpec((block_rows, GDN_V_DIM), lambda s, c: (s * n_chunks + c, 0))
    pad_heads = lambda v: jnp.pad(v.astype(F32), (GDN_V_HEADS, LANES - 2 * GDN_V_HEADS)).reshape(1, LANES)
    state_spec = pl.BlockSpec((group, GDN_V_HEADS, GDN_HEAD_DIM, GDN_HEAD_DIM), lambda s, c: (s, 0, 0, 0))
    in_specs = [blk(GDN_CONV_DIM), blk(GDN_V_DIM), blk(LANES),
                _const_spec((GDN_CONV, GDN_CONV_DIM)), _const_spec((1, LANES)), _const_spec((1, LANES)),
                _const_spec((1, GDN_HEAD_DIM))]
    args = [x, z, ba, conv_w, pad_heads(a_log), pad_heads(dt_bias), norm_g.reshape(1, GDN_HEAD_DIM)]
    if has_state:
        in_specs += [state_spec, pl.BlockSpec((group, GDN_CONV - 1, GDN_CONV_DIM), lambda s, c: (s, 0, 0))]
        args += [state, conv0]
    return pl.pallas_call(
        functools.partial(_gdn_kernel, chunk=chunk, n_valid=n_valid, has_state=has_state, group=group),
        grid=(n_seq // group, n_chunks),
        in_specs=in_specs,
        out_specs=[out_blk, state_spec,
                   pl.BlockSpec((group, SUBLANES, GDN_CONV_DIM), lambda s, c: (s, 0, 0))],
        out_shape=[jax.ShapeDtypeStruct((rows, GDN_V_DIM), _mixer_out_dtype(block_rows)),
                   jax.ShapeDtypeStruct((n_seq, GDN_V_HEADS, GDN_HEAD_DIM, GDN_HEAD_DIM), F32),
                   jax.ShapeDtypeStruct((n_seq, SUBLANES, GDN_CONV_DIM), F32)],
        scratch_shapes=[pltpu.VMEM((group, SUBLANES, GDN_CONV_DIM), F32)],
        compiler_params=pltpu.CompilerParams(dimension_semantics=("arbitrary", "arbitrary"),
                                             vmem_limit_bytes=VMEM_LIMIT),
        name="gdn",
    )(*args)


def _hgrn_kernel(*refs, chunk, n_valid, has_state, layer, group):
    it = iter(refs)
    x_ref, lbl_ref, ng_ref = next(it), next(it), next(it)
    if has_state:
        s0_ref = next(it)
    o_ref, s_ref = next(it), next(it)
    st = next(it)
    c = pl.program_id(1)

    units = [(s, h) for s in range(group) for h in range(HG_HEADS)]

    @pl.when(c == 0)
    def _():
        for s, h in units:
            st[s, h] = s0_ref[s, h].T if has_state else jnp.zeros((HG_HEAD_DIM, HG_HEAD_DIM), F32)

    logits = [lbl_ref[i:i + 1, :] for i in range(DEPTH)]
    mx = functools.reduce(jnp.maximum, logits)
    ex = [jnp.exp(l - mx) for l in logits]
    tot = functools.reduce(lambda u, v: u + v, ex)
    probs = [e / tot for e in ex]
    lb = functools.reduce(lambda u, v: u + v, probs[:layer + 1]) - probs[0]

    total = group * chunk
    packed = x_ref.shape[0] != total
    x_all = _spread_rows(x_ref[...], group, chunk, n_valid) if packed else x_ref
    row = lax.broadcasted_iota(jnp.int32, (total, HG_DIM), 0)
    live = jnp.bitwise_and(row, chunk - 1) < n_valid
    fz = x_all[:, HG_DIM:2 * HG_DIM]
    log_f = jnp.where(live, jnp.log(lb + (1.0 - lb) * _sigmoid(fz)), 0.0)
    k_all = jnp.where(live, (1.0 - lb) * _sigmoid(-fz), 0.0)
    q_all = _silu(x_all[:, 0:HG_DIM])
    ri = lax.broadcasted_iota(jnp.int32, (total, total), 0)
    rj = lax.broadcasted_iota(jnp.int32, (total, total), 1)
    incl_f = jnp.logical_and(rj <= ri, rj >= ri - jnp.bitwise_and(ri, chunk - 1)).astype(F32)
    b2_all = _dot_f32(incl_f, log_f) * LOG2_E
    ones_l = jnp.ones((LANES, LANES), BF16)
    ng = ng_ref[...]
    n_blk = chunk // SUBLANES
    sub = lax.broadcasted_iota(jnp.int32, (SUBLANES, LANES), 0)

    rows8 = lambda t, bi: t[bi * SUBLANES:(bi + 1) * SUBLANES, :]
    seq_rows = lambda s: slice(s * chunk, (s + 1) * chunk)
    head_cols = lambda base, h: slice(base + h * HG_HEAD_DIM, base + (h + 1) * HG_HEAD_DIM)
    q_h = {(s, h): q_all[seq_rows(s), head_cols(0, h)] for s, h in units}
    k_h = {(s, h): k_all[seq_rows(s), head_cols(0, h)] for s, h in units}
    b_h = {(s, h): b2_all[seq_rows(s), head_cols(0, h)] for s, h in units}
    v_h = {(s, h): x_all[seq_rows(s), head_cols(2 * HG_DIM, h)] for s, h in units}
    seg = [0]
    for bi in range(1, n_blk):
        seg.append(seg[-1] + bi * SUBLANES)

    o_inter, w_sum, p_all = {}, {}, {}
    for u in units:
        q, k, b2 = q_h[u], k_h[u], b_h[u]
        o_inter[u] = _dot_nt(q * jnp.exp2(b2), st[u])
        tiles = []
        for bi in range(n_blk):
            for jj in range(SUBLANES):
                j = bi * SUBLANES + jj
                e = jnp.where(sub >= jj, rows8(b2, bi) - b2[j:j + 1, :], NEG_INF)
                tiles.append(rows8(q, bi) * k[j:j + 1, :] * jnp.exp2(e))
        w_sum[u] = jnp.dot(jnp.concatenate(tiles, axis=0).astype(BF16), ones_l, preferred_element_type=F32)
        if n_blk > 1:
            q_t, k_t = [], []
            for bi in range(1, n_blk):
                r = b2[bi * SUBLANES - 1:bi * SUBLANES, :]
                q_t.append(rows8(q, bi) * jnp.exp2(rows8(b2, bi) - r))
                k_t.append(k[0:bi * SUBLANES, :] * jnp.exp2(r - b2[0:bi * SUBLANES, :]))
            p_all[u] = _dot_nt(jnp.concatenate(q_t, axis=0), jnp.concatenate(k_t, axis=0))

    if n_blk > 1:
        p_shape = (SUBLANES * (n_blk - 1), seg[-1])
        prow = jnp.right_shift(lax.broadcasted_iota(jnp.int32, p_shape, 0), 3)
        pcol = lax.broadcasted_iota(jnp.int32, p_shape, 1)
        own = functools.reduce(jnp.logical_or, [
            jnp.logical_and(prow == bi - 1, jnp.logical_and(pcol >= seg[bi - 1], pcol < seg[bi]))
            for bi in range(1, n_blk)])
    o_intra = {}
    for u in units:
        v = v_h[u]
        o_blocks = []
        for bi in range(n_blk):
            terms = [rows8(w_sum[u], bi * SUBLANES + jj) * v[bi * SUBLANES + jj:bi * SUBLANES + jj + 1, :]
                     for jj in range(SUBLANES)]
            o_blocks.append(functools.reduce(lambda a, w: a + w, terms))
        if n_blk > 1:
            v_t = jnp.concatenate([v[0:bi * SUBLANES, :] for bi in range(1, n_blk)], axis=0)
            o_off = _dot(jnp.where(own, p_all[u], 0.0), v_t)
            for bi in range(1, n_blk):
                o_blocks[bi] = o_blocks[bi] + rows8(o_off, bi - 1)
        o_intra[u] = o_blocks[0] if n_blk == 1 else jnp.concatenate(o_blocks, axis=0)

    gated = {}
    for u in units:
        s, h = u
        k, b2 = k_h[u], b_h[u]
        b_last = b2[chunk - 1:chunk, :]
        st[u] = st[u] * jnp.exp2(b_last) + _dot_tn(v_h[u], k * jnp.exp2(b_last - b2))
        gate = x_all[seq_rows(s), head_cols(3 * HG_DIM, h)]
        gated[u] = _rms(o_inter[u] + o_intra[u], ng) * _silu(gate)
    for h in range(HG_HEADS):
        both = _gather_rows([gated[s, h] for s in range(group)], chunk, n_valid, packed)
        o_ref[:, head_cols(0, h)] = both.astype(o_ref.dtype)

    @pl.when(c == pl.num_programs(1) - 1)
    def _():
        for s, h in units:
            s_ref[s, h] = st[s, h].T


def _hgrn(x, lb_logits, norm_g, layer, n_seq, n_chunks, chunk, n_valid, row_offset=0, state=None):
    has_state = state is not None
    rows = n_seq * n_chunks * n_valid
    group, block_rows, row_map = _seq_blocks(n_chunks, chunk, n_valid, row_offset)
    state_spec = pl.BlockSpec((group, HG_HEADS, HG_HEAD_DIM, HG_HEAD_DIM), lambda s, c: (s, 0, 0, 0))
    in_specs = [pl.BlockSpec((block_rows, 4 * HG_DIM), row_map),
                _const_spec((DEPTH, HG_DIM)), _const_spec((1, HG_HEAD_DIM))]
    args = [x, lb_logits, norm_g.reshape(1, HG_HEAD_DIM)]
    if has_state:
        in_specs.append(state_spec)
        args.append(state)
    return pl.pallas_call(
        functools.partial(_hgrn_kernel, chunk=chunk, n_valid=n_valid, has_state=has_state, layer=layer, group=group),
        grid=(n_seq // group, n_chunks),
        in_specs=in_specs,
        out_specs=[pl.BlockSpec((block_rows, HG_DIM), lambda s, c: (s * n_chunks + c, 0)), state_spec],
        out_shape=[jax.ShapeDtypeStruct((rows, HG_DIM), _mixer_out_dtype(block_rows)),
                   jax.ShapeDtypeStruct((n_seq, HG_HEADS, HG_HEAD_DIM, HG_HEAD_DIM), F32)],
        scratch_shapes=[pltpu.VMEM((group, HG_HEADS, HG_HEAD_DIM, HG_HEAD_DIM), F32)],
        compiler_params=pltpu.CompilerParams(dimension_semantics=("arbitrary", "arbitrary"),
                                             vmem_limit_bytes=VMEM_LIMIT),
        name="hgrn",
    )(*args)


def kernel(x_prompt, x_sample, cache_swa_k, cache_swa_v, state_gdn, state_gdn_conv, state_hgrn, norm_ffn, ffn_w_in, ffn_w_out, norm_mix, swa_w_in, swa_w_out, swa_sinks, gdn_w_in, gdn_conv_w, gdn_a_log, gdn_dt_bias, gdn_norm, gdn_w_out, hgrn_w_in, hgrn_lb_logits, hgrn_norm, hgrn_w_out, final_norm):
    x = (x_prompt.reshape(N_PROMPT, D_MODEL), x_sample.reshape(N_SAMPLE, D_MODEL))
    ffn = functools.partial(_ffn, norm_ffn=norm_ffn.reshape(DEPTH, 2, 1, D_MODEL),
                            w_in=ffn_w_in, w_out=ffn_w_out)
    prompt_tables = _rope_tables(jnp.arange(SEQ), LANES)
    sample_pos = PAST_LEN + jnp.arange(DEC_SEQ)
    window_cols = lambda c: jnp.transpose(c, (0, 1, 3, 4, 2)).reshape(c.shape[0], DEC_BATCH, SWA_KV, WINDOW)
    window_rows = lambda c: jnp.transpose(c.reshape(DEC_BATCH, SWA_KV_HEADS, SWA_HEAD_DIM, WINDOW), (0, 3, 1, 2))
    cache_kt, cache_vt = window_cols(cache_swa_k), window_cols(cache_swa_v)

    outs = {k: [] for k in ("pk", "pv", "pg", "pc", "ph", "sk", "sv", "sg", "sc", "sh")}
    proj = None
    for i in range(DEPTH):
        kind, j = i % N_MIXERS, i // N_MIXERS
        if proj is not None:
            x = ffn(x, i - 1, 1, proj=proj)
        x = ffn(x, i, 0)
        if kind == 0:
            (qkv,) = _norm_proj(x, norm_mix[i], swa_w_in, j, [(0, SWA_IN, SWA_IN)], TOKEN_TILE)
            o_p, pk, pv = _swa_prompt(qkv, swa_sinks[j], prompt_tables)
            o_s, sk, sv = _swa_sample(qkv[N_PROMPT:], j, cache_kt, cache_vt, swa_sinks[j], sample_pos)
            kv_shape = (WINDOW, SWA_KV_HEADS, SWA_HEAD_DIM)
            outs["pk"].append(pk.reshape((BATCH,) + kv_shape))
            outs["pv"].append(pv.reshape((BATCH,) + kv_shape))
            outs["sk"].append(window_rows(sk))
            outs["sv"].append(window_rows(sv))
            proj = (o_p, o_s, swa_w_out[j].astype(BF16))
        elif kind == 1:
            segments = [(0, GDN_CONV_DIM, GDN_CONV_DIM), (GDN_CONV_DIM, GDN_V_DIM, GDN_V_DIM),
                        (GDN_CONV_DIM + GDN_V_DIM, 2 * GDN_V_HEADS, LANES)]
            qkv, z, ba = _norm_proj(x, norm_mix[i], gdn_w_in, j, segments, TOKEN_TILE // 2)
            o_p, pg, p_tail = _gdn(qkv, z, ba, gdn_conv_w[j], gdn_a_log[j], gdn_dt_bias[j], gdn_norm[j],
                                   BATCH, SEQ // GDN_CHUNK, GDN_CHUNK, GDN_CHUNK)
            o_s, sg, s_tail = _gdn(qkv, z, ba, gdn_conv_w[j], gdn_a_log[j], gdn_dt_bias[j], gdn_norm[j],
                                   DEC_BATCH, 1, PAD_CHUNK, DEC_SEQ, row_offset=N_PROMPT,
                                   state=state_gdn[j], conv0=state_gdn_conv[j])
            outs["pg"].append(pg)
            outs["sg"].append(sg)
            keep = GDN_CONV - 1
            outs["pc"].append(p_tail[:, SUBLANES - keep:])
            outs["sc"].append(s_tail[:, DEC_SEQ - keep:DEC_SEQ])
            proj = (o_p, o_s.astype(BF16), gdn_w_out[j].astype(BF16))
        else:
            (xin,) = _norm_proj(x, norm_mix[i], hgrn_w_in, j, [(0, 4 * HG_DIM, 4 * HG_DIM)], TOKEN_TILE)
            o_p, ph = _hgrn(xin, hgrn_lb_logits, hgrn_norm[j], i, BATCH, SEQ // HG_CHUNK, HG_CHUNK, HG_CHUNK)
            o_s, sh = _hgrn(xin, hgrn_lb_logits, hgrn_norm[j], i, DEC_BATCH, 1, PAD_CHUNK, DEC_SEQ,
                            row_offset=N_PROMPT, state=state_hgrn[j])
            outs["ph"].append(ph)
            outs["sh"].append(sh)
            proj = (o_p, o_s.astype(BF16), hgrn_w_out[j].astype(BF16))
    y_prompt, y_sample = ffn(x, DEPTH - 1, 1, proj=proj, final_g=final_norm)
    y_prompt = y_prompt.reshape(BATCH, SEQ, D_MODEL)
    y_sample = y_sample.reshape(DEC_BATCH, DEC_SEQ, D_MODEL)
    st = lambda k: jnp.stack(outs[k])
    return (y_prompt, y_sample, st("pk"), st("pv"), st("pg"), st("pc"), st("ph"),
            st("sk"), st("sv"), st("sg"), st("sc"), st("sh"))
```
